```python
import jax, jax.numpy as jnp
from jax import lax
import numpy as np

D_MODEL = 2048
BATCH = 4
SEQ = 2048
DEPTH = 2
DEC_BATCH = 128
DEC_SEQ = 8
PAST_LEN = 16384
PAGE_SIZE = 128

W_A = D_MODEL
N_HEADS_A = 16
HEAD_A = W_A // N_HEADS_A
CONV_W = 4
C_GATE = 8.0
W_B = D_MODEL
N_GROUPS_B = 16
GROUP_B = W_B // N_GROUPS_B
CHUNK = 128
D_FF = 3 * D_MODEL
N_EXPERTS = 8
TOP_K = 2
D_EXPERT = D_MODEL
N_DENSE = (DEPTH + 1) // 2
N_MOE = DEPTH // 2
EPS = 1e-6
IN_COLS = 2 * W_A + 2 * W_B + 2 * D_MODEL
SPLITS = (W_A, 2 * W_A, 2 * W_A + W_B, 2 * W_A + 2 * W_B, 2 * W_A + 2 * W_B + D_MODEL)

kernel_name = 'hybrid_rglru_chunkmlp_moe_decode_step'


def rms_norm(x, g):
    xf = x.astype(jnp.float32)
    y = xf * lax.rsqrt(jnp.mean(xf * xf, axis=-1, keepdims=True) + EPS)
    return (y * g.astype(jnp.float32)).astype(x.dtype)


def causal_conv(x, buf, w, b):
    t = x.shape[1]
    xp = jnp.concatenate([buf.astype(x.dtype), x], axis=1)
    out = b
    for k in range(CONV_W):
        out = out + xp[:, k:k + t] * w[k]
    return out, xp[:, -(CONV_W - 1):]


def _lin_combine(left, right):
    a_l, b_l = left
    a_r, b_r = right
    return a_l * a_r, a_r * b_l + b_r


def rg_lru(x, h0, w_r, b_r, w_i, b_i, lam):
    bsz, t, _ = x.shape
    xf = x.astype(jnp.float32)
    xh = xf.reshape(bsz, t, N_HEADS_A, HEAD_A)
    r = jax.nn.sigmoid(jnp.einsum('bthd,hde->bthe', xh, w_r.astype(jnp.float32)).reshape(bsz, t, W_A) + b_r.astype(jnp.float32))
    i = jax.nn.sigmoid(jnp.einsum('bthd,hde->bthe', xh, w_i.astype(jnp.float32)).reshape(bsz, t, W_A) + b_i.astype(jnp.float32))
    log_a = -C_GATE * r * jax.nn.softplus(-lam.astype(jnp.float32))
    a = jnp.exp(log_a)
    b = jnp.sqrt(-jnp.expm1(2.0 * log_a)) * (i * xf)
    b = b.at[:, 0].add(a[:, 0] * h0.astype(jnp.float32))
    _, h = lax.associative_scan(_lin_combine, (a, b), axis=1)
    return h.astype(x.dtype), h[:, -1].astype(x.dtype)


def chunk_mix(u, v, w_s, b_s):
    bsz, t, _ = v.shape
    pad = (-t) % CHUNK
    vp = jnp.pad(v, ((0, 0), (0, pad), (0, 0)))
    n_c = (t + pad) // CHUNK
    vc = vp.reshape(bsz, n_c, CHUNK, N_GROUPS_B, GROUP_B)
    mask = jnp.tril(jnp.ones((CHUNK, CHUNK), dtype=w_s.dtype))
    mixed = jnp.einsum('gts,bcsgd->bctgd', w_s * mask, vc) + b_s.T[None, None, :, :, None]
    mixed = mixed.reshape(bsz, n_c * CHUNK, W_B)[:, :t]
    return u * mixed


def swiglu(x, wg, wu, wd):
    return (jax.nn.silu(x @ wg) * (x @ wu)) @ wd


def moe(x, router_w, router_b, wg, wu, wd):
    logits = x.astype(jnp.float32) @ router_w.astype(jnp.float32) + router_b.astype(jnp.float32)
    probs = jax.nn.softmax(logits, axis=-1)
    top_v, top_i = lax.top_k(probs, TOP_K)
    top_v = top_v / jnp.sum(top_v, axis=-1, keepdims=True)
    gates = jnp.sum(jax.nn.one_hot(top_i, N_EXPERTS, dtype=jnp.float32) * top_v[..., None], axis=-2)
    gates = gates.astype(x.dtype)
    y = jnp.zeros_like(x)
    for e in range(N_EXPERTS):
        y = y + gates[..., e:e + 1] * swiglu(x, wg[e], wu[e], wd[e])
    return y


def mixer(xm, h0, conv0, l, p):
    proj = xm @ p['w_in'][l]
    xa, ga, u, v, mg_a, mg_b = jnp.split(proj, SPLITS, axis=-1)
    xc, conv_new = causal_conv(xa, conv0, p['conv_w'][l], p['conv_b'][l])
    h, h_last = rg_lru(xc, h0, p['w_r'][l], p['b_r'][l], p['w_i'][l], p['b_i'][l], p['lru_lambda'][l])
    ya = jax.nn.gelu(ga) * h
    vn = rms_norm(jax.nn.gelu(v), p['g_v'][l])
    yb = chunk_mix(jax.nn.gelu(u), vn, p['w_s'][l], p['b_s'][l])
    m = jax.nn.sigmoid(mg_a) * (ya @ p['w_pa'][l]) + jax.nn.sigmoid(mg_b) * (yb @ p['w_pb'][l])
    return m @ p['w_o'][l], h_last, conv_new, vn


def trunk(x, c, h0s, conv0s, p):
    hs, convs, vs = [], [], []
    for l in range(DEPTH):
        mod = jax.nn.silu(c) @ p['w_ada'][l] + p['b_ada'][l]
        sh1, sc1, gt1, sh2, sc2, gt2 = [m[:, None, :] for m in jnp.split(mod, 6, axis=-1)]
        xn = rms_norm(x, p['g_pre_mix'][l]) * (1.0 + sc1) + sh1
        y, h_last, conv_new, vn = mixer(xn, h0s[l], conv0s[l], l, p)
        x = x + gt1 * rms_norm(y, p['g_post_mix'][l])
        xn = rms_norm(x, p['g_pre_ffn'][l]) * (1.0 + sc2) + sh2
        j = l // 2
        if l % 2 == 0:
            y = swiglu(xn, p['ffn_wg'][j], p['ffn_wu'][j], p['ffn_wd'][j])
        else:
            y = moe(xn, p['router_w'][j], p['router_b'][j], p['moe_wg'][j], p['moe_wu'][j], p['moe_wd'][j])
        x = x + gt2 * rms_norm(y, p['g_post_ffn'][l])
        hs.append(h_last)
        convs.append(conv_new)
        vs.append(vn)
    return x, jnp.stack(hs), jnp.stack(convs), jnp.stack(vs)


def setup_inputs(seed: int = 0) -> dict:
    key = jax.random.key(seed)
    ks = iter(jax.random.split(key, 40))

    def nrm(shape, s):
        return s * jax.random.normal(next(ks), shape, jnp.float32)

    u = jax.random.uniform(next(ks), (DEPTH, W_A), jnp.float32, minval=0.9, maxval=0.999)
    a_base = u ** (1.0 / C_GATE)
    lru_lambda = jnp.log(a_base) - jnp.log1p(-a_base)
    d = D_MODEL
    return {
        'x_prompt': nrm((BATCH, SEQ, d), 1.0),
        'x_sample': nrm((DEC_BATCH, DEC_SEQ, d), 1.0),
        'c_prompt': nrm((BATCH, d), 1.0),
        'c_sample': nrm((DEC_BATCH, d), 1.0),
        'state_lru_h': nrm((DEPTH, DEC_BATCH, W_A), 0.5),
        'state_lru_conv': nrm((DEPTH, DEC_BATCH, CONV_W - 1, W_A), 1.0),
        'w_ada': nrm((DEPTH, d, 6 * d), 0.5 * d ** -0.5),
        'b_ada': nrm((DEPTH, 6 * d), 0.02),
        'g_pre_mix': 1.0 + nrm((DEPTH, d), 0.05),
        'g_post_mix': 1.0 + nrm((DEPTH, d), 0.05),
        'g_pre_ffn': 1.0 + nrm((DEPTH, d), 0.05),
        'g_post_ffn': 1.0 + nrm((DEPTH, d), 0.05),
        'w_in': nrm((DEPTH, d, IN_COLS), d ** -0.5),
        'conv_w': nrm((DEPTH, CONV_W, W_A), CONV_W ** -0.5),
        'conv_b': nrm((DEPTH, W_A), 0.02),
        'w_r': nrm((DEPTH, N_HEADS_A, HEAD_A, HEAD_A), HEAD_A ** -0.5),
        'b_r': nrm((DEPTH, W_A), 0.02),
        'w_i': nrm((DEPTH, N_HEADS_A, HEAD_A, HEAD_A), HEAD_A ** -0.5),
        'b_i': nrm((DEPTH, W_A), 0.02),
        'lru_lambda': lru_lambda,
        'g_v': 1.0 + nrm((DEPTH, W_B), 0.05),
        'w_s': nrm((DEPTH, N_GROUPS_B, CHUNK, CHUNK), CHUNK ** -0.5),
        'b_s': 1.0 + nrm((DEPTH, N_GROUPS_B, CHUNK), 0.1),
        'w_pa': nrm((DEPTH, W_A, d), W_A ** -0.5),
        'w_pb': nrm((DEPTH, W_B, d), W_B ** -0.5),
        'w_o': nrm((DEPTH, d, d), d ** -0.5),
        'ffn_wg': nrm((N_DENSE, d, D_FF), d ** -0.5),
        'ffn_wu': nrm((N_DENSE, d, D_FF), d ** -0.5),
        'ffn_wd': nrm((N_DENSE, D_FF, d), D_FF ** -0.5),
        'router_w': nrm((N_MOE, d, N_EXPERTS), d ** -0.5),
        'router_b': nrm((N_MOE, N_EXPERTS), 0.01),
        'moe_wg': nrm((N_MOE, N_EXPERTS, d, D_EXPERT), d ** -0.5),
        'moe_wu': nrm((N_MOE, N_EXPERTS, d, D_EXPERT), d ** -0.5),
        'moe_wd': nrm((N_MOE, N_EXPERTS, D_EXPERT, d), D_EXPERT ** -0.5),
    }


def reference(x_prompt, x_sample, c_prompt, c_sample, state_lru_h, state_lru_conv,
              w_ada, b_ada, g_pre_mix, g_post_mix, g_pre_ffn, g_post_ffn,
              w_in, conv_w, conv_b, w_r, b_r, w_i, b_i, lru_lambda, g_v, w_s, b_s,
              w_pa, w_pb, w_o, ffn_wg, ffn_wu, ffn_wd,
              router_w, router_b, moe_wg, moe_wu, moe_wd):
    p = dict(w_ada=w_ada, b_ada=b_ada, g_pre_mix=g_pre_mix, g_post_mix=g_post_mix,
             g_pre_ffn=g_pre_ffn, g_post_ffn=g_post_ffn, w_in=w_in, conv_w=conv_w,
             conv_b=conv_b, w_r=w_r, b_r=b_r, w_i=w_i, b_i=b_i, lru_lambda=lru_lambda,
             g_v=g_v, w_s=w_s, b_s=b_s, w_pa=w_pa, w_pb=w_pb, w_o=w_o,
             ffn_wg=ffn_wg, ffn_wu=ffn_wu, ffn_wd=ffn_wd, router_w=router_w,
             router_b=router_b, moe_wg=moe_wg, moe_wu=moe_wu, moe_wd=moe_wd)
    h0_p = jnp.zeros((DEPTH, x_prompt.shape[0], W_A), x_prompt.dtype)
    conv0_p = jnp.zeros((DEPTH, x_prompt.shape[0], CONV_W - 1, W_A), x_prompt.dtype)
    y_prompt, h_p, conv_p, _ = trunk(x_prompt, c_prompt, h0_p, conv0_p, p)
    y_sample, h_s, conv_s, v_s = trunk(x_sample, c_sample, state_lru_h, state_lru_conv, p)
    return (y_prompt, y_sample, h_p, conv_p, h_s, conv_s, v_s)
```

```python
import functools

import jax
import jax.numpy as jnp
from jax import lax
from jax.experimental import pallas as pl
from jax.experimental.pallas import tpu as pltpu

F32 = jnp.float32
BF16 = jnp.bfloat16

D_MODEL = 2048
N_HEADS_A = 16
HEAD_A = D_MODEL // N_HEADS_A
CONV_W = 4
C_GATE = 8.0
N_GROUPS_B = 16
GROUP_B = D_MODEL // N_GROUPS_B
CHUNK = 128
N_EXPERTS = 8
EPS = 1e-6

LANES = 128
SUBLANES = 8
VMEM_CAP = 58 * 2**20
CONV_PAD = SUBLANES


def _nbytes(shape, dtype):
    n = 1
    for s in shape:
        n *= s
    return n * jnp.dtype(dtype).itemsize


def _params(sem, blocks, scratch=()):
    need = 2 * sum(_nbytes(s, d) for s, d in blocks) + sum(_nbytes(s, d) for s, d in scratch)
    limit = min(VMEM_CAP, need + 16 * 2**20)
    return pltpu.CompilerParams(dimension_semantics=sem, vmem_limit_bytes=limit)


def _rms(x, g):
    return x * lax.rsqrt(jnp.mean(x * x, axis=-1, keepdims=True) + EPS) * g


def _sigmoid(x):
    return 1.0 / (1.0 + jnp.exp(-x))


def _silu(x):
    return x * _sigmoid(x)


def _gelu(x):
    return jax.nn.gelu(x)


class _Group:
    def __init__(self, b, t, per_token_mod):
        self.b, self.t, self.m = b, t, b * t
        self.per_token_mod = per_token_mod

    def mod_spec(self, tm, nd_grid=1, axis=0):
        if self.per_token_mod:
            def imap(*ids):
                return (ids[axis], 0)
            return pl.BlockSpec((tm, D_MODEL), imap)
        t = self.t

        def imap(*ids):
            return ((ids[axis] * tm) // t, 0, 0)
        return pl.BlockSpec((None, 1, D_MODEL), imap)


def _ada_kernel(c_ref, w_ref, b_ref, o_ref):
    s = _silu(c_ref[...]).astype(BF16)
    o_ref[...] = jnp.dot(s, w_ref[...].astype(BF16), preferred_element_type=F32) + b_ref[...]


def _ada(c, w_ada, b_ada):
    depth, _, n = w_ada.shape
    r = c.shape[0]
    tn = 1024
    return pl.pallas_call(
        _ada_kernel,
        grid=(depth, n // tn),
        in_specs=[
            pl.BlockSpec((r, D_MODEL), lambda l, j: (0, 0)),
            pl.BlockSpec((None, D_MODEL, tn), lambda l, j: (l, 0, j)),
            pl.BlockSpec((None, 1, tn), lambda l, j: (l, 0, j)),
        ],
        out_specs=pl.BlockSpec((None, r, tn), lambda l, j: (l, 0, j)),
        out_shape=jax.ShapeDtypeStruct((depth, r, n), F32),
        compiler_params=_params(("arbitrary", "arbitrary"),
                                [((r, D_MODEL), F32), ((D_MODEL, tn), F32), ((r, tn), F32)],
                                [((D_MODEL, tn), BF16)]),
        name="ada_mod",
    )(c, w_ada, b_ada.reshape(depth, 1, n))


def _prenorm_kernel(x_ref, g_ref, sc_ref, sh_ref, o_ref):
    o_ref[...] = (_rms(x_ref[...], g_ref[...]) * (1.0 + sc_ref[...]) + sh_ref[...]).astype(BF16)


def _prenorm(grp, x, g, sc, sh):
    tm = 512
    return pl.pallas_call(
        _prenorm_kernel,
        grid=(grp.m // tm,),
        in_specs=[
            pl.BlockSpec((tm, D_MODEL), lambda i: (i, 0)),
            pl.BlockSpec((1, D_MODEL), lambda i: (0, 0)),
            grp.mod_spec(tm), grp.mod_spec(tm),
        ],
        out_specs=pl.BlockSpec((tm, D_MODEL), lambda i: (i, 0)),
        out_shape=jax.ShapeDtypeStruct((grp.m, D_MODEL), BF16),
        compiler_params=_params(("arbitrary",), [((tm, D_MODEL), F32)] * 4),
        name="prenorm",
    )(x, g.reshape(1, D_MODEL), sc, sh)


def _mm_act_kernel(x_ref, w_ref, o_ref, wbf_ref, *, act):
    @pl.when(pl.program_id(1) == 0)
    def _():
        wbf_ref[...] = w_ref[...].astype(BF16)

    y = jnp.dot(x_ref[...], wbf_ref[...], preferred_element_type=F32)
    o_ref[...] = act(y).astype(o_ref.dtype)


def _mm_act(grp, x, w, layer, col0, ncols, act, out_dtype, name):
    tm, tn = 1024, 1024
    c0 = col0 // tn
    return pl.pallas_call(
        functools.partial(_mm_act_kernel, act=act),
        grid=(ncols // tn, grp.m // tm),
        in_specs=[
            pl.BlockSpec((tm, D_MODEL), lambda j, i: (i, 0)),
            pl.BlockSpec((None, D_MODEL, tn), lambda j, i: (layer, 0, c0 + j)),
        ],
        out_specs=pl.BlockSpec((tm, tn), lambda j, i: (i, j)),
        out_shape=jax.ShapeDtypeStruct((grp.m, ncols), out_dtype),
        scratch_shapes=[pltpu.VMEM((D_MODEL, tn), BF16)],
        compiler_params=_params(("arbitrary", "arbitrary"),
                                [((tm, D_MODEL), BF16), ((D_MODEL, tn), F32), ((tm, tn), F32)],
                                [((D_MODEL, tn), BF16), ((tm, tn), F32)]),
        name=name,
    )(x, w)


def _branch_a_kernel(xa_ref, ga_ref, conv0_ref, h0_ref, cw_ref, cb_ref, wr_ref, wi_ref,
                     br_ref, bi_ref, lam_ref, ya_ref, hlast_ref, convnew_ref, xp_ref, h_ref,
                     *, nb, tt, n_t):
    t_idx = pl.program_id(1)
    hist = CONV_W - 1
    lo = CONV_PAD - hist

    @pl.when(t_idx == 0)
    def _():
        xp_ref[:, lo:CONV_PAD, :] = conv0_ref[...]
        h_ref[...] = h0_ref[...]

    xp_ref[:, CONV_PAD:CONV_PAD + tt, :] = xa_ref[...]
    rows = nb * tt
    t_in_seq = lax.broadcasted_iota(jnp.int32, (rows, HEAD_A), 0) & (tt - 1)

    for hd in range(N_HEADS_A):
        sl = slice(hd * HEAD_A, (hd + 1) * HEAD_A)
        xc = cb_ref[:, sl][None]
        for k in range(CONV_W):
            xc = xc + cw_ref[k:k + 1, sl][None] * xp_ref[:, lo + k:lo + k + tt, sl]
        xc = xc.reshape(rows, HEAD_A)
        xcb = xc.astype(BF16)
        r = _sigmoid(jnp.dot(xcb, wr_ref[hd].astype(BF16), preferred_element_type=F32) + br_ref[:, sl])
        i = _sigmoid(jnp.dot(xcb, wi_ref[hd].astype(BF16), preferred_element_type=F32) + bi_ref[:, sl])
        z = -lam_ref[:, sl]
        softplus = jnp.maximum(z, 0.0) + jnp.log1p(jnp.exp(-jnp.abs(z)))
        log_a = (-C_GATE * r) * softplus
        a = jnp.exp(log_a)
        th = jnp.tanh(log_a)
        b = jnp.sqrt(-2.0 * th / (1.0 - th)) * (i * xc)
        d = 1
        while d < tt:
            a_sh = pltpu.roll(a, d, axis=0)
            b_sh = pltpu.roll(b, d, axis=0)
            keep = t_in_seq >= d
            b = jnp.where(keep, a * b_sh + b, b)
            a = jnp.where(keep, a * a_sh, a)
            d *= 2
        h = a.reshape(nb, tt, HEAD_A) * h_ref[:, :, sl] + b.reshape(nb, tt, HEAD_A)
        h_ref[:, :, sl] = h[:, tt - 1:tt, :]
        ya_ref[:, sl] = (ga_ref[:, sl].astype(F32) * h.reshape(rows, HEAD_A)).astype(BF16)

    xp_ref[:, lo:CONV_PAD, :] = xp_ref[:, lo + tt:CONV_PAD + tt, :]

    @pl.when(t_idx == n_t - 1)
    def _():
        hlast_ref[...] = h_ref[...]
        convnew_ref[...] = xp_ref[:, lo:CONV_PAD, :]


def _branch_a(grp, nb, tt, xa, gel, conv0, h0, cw, cb, wr, wi, br, bi, lam):
    b, t = grp.b, grp.t
    n_t = t // tt
    rows = nb * tt
    hist = CONV_W - 1
    vec = lambda: pl.BlockSpec((1, D_MODEL), lambda bi_, ti: (0, 0))
    head_w = lambda: pl.BlockSpec((N_HEADS_A, HEAD_A, HEAD_A), lambda bi_, ti: (0, 0, 0))
    return pl.pallas_call(
        functools.partial(_branch_a_kernel, nb=nb, tt=tt, n_t=n_t),
        grid=(b // nb, n_t),
        in_specs=[
            pl.BlockSpec((nb, tt, D_MODEL), lambda bi_, ti: (bi_, ti, 0)),
            pl.BlockSpec((rows, D_MODEL), lambda bi_, ti: (bi_ * n_t + ti, 0)),
            pl.BlockSpec((nb, hist, D_MODEL), lambda bi_, ti: (bi_, 0, 0)),
            pl.BlockSpec((nb, 1, D_MODEL), lambda bi_, ti: (bi_, 0, 0)),
            pl.BlockSpec((CONV_W, D_MODEL), lambda bi_, ti: (0, 0)),
            vec(), head_w(), head_w(), vec(), vec(), vec(),
        ],
        out_specs=[
            pl.BlockSpec((rows, D_MODEL), lambda bi_, ti: (bi_ * n_t + ti, 0)),
            pl.BlockSpec((nb, 1, D_MODEL), lambda bi_, ti: (bi_, 0, 0)),
            pl.BlockSpec((nb, hist, D_MODEL), lambda bi_, ti: (bi_, 0, 0)),
        ],
        out_shape=[
            jax.ShapeDtypeStruct((grp.m, D_MODEL), BF16),
            jax.ShapeDtypeStruct((b, 1, D_MODEL), F32),
            jax.ShapeDtypeStruct((b, hist, D_MODEL), F32),
        ],
        scratch_shapes=[
            pltpu.VMEM((nb, CONV_PAD + tt, D_MODEL), F32),
            pltpu.VMEM((nb, 1, D_MODEL), F32),
        ],
        compiler_params=_params(("arbitrary", "arbitrary"),
                                [((rows, D_MODEL), F32), ((rows, D_MODEL), BF16), ((rows, D_MODEL), BF16),
                                 ((nb, 8, D_MODEL), F32), ((nb, 8, D_MODEL), F32),
                                 ((2 * N_HEADS_A, HEAD_A, HEAD_A), F32)],
                                [((nb, CONV_PAD + tt, D_MODEL), F32), ((nb, 8, D_MODEL), F32)]),
        name="branch_a",
    )(xa.reshape(b, t, D_MODEL), gel, conv0, h0.reshape(b, 1, D_MODEL), cw, cb.reshape(1, D_MODEL),
      wr, wi, br.reshape(1, D_MODEL), bi.reshape(1, D_MODEL), lam.reshape(1, D_MODEL))


def _branch_b_kernel(u_ref, v_ref, gv_ref, ws_ref, bias_ref, yb_ref, wm_ref, *, n_chunks):
    @pl.when(pl.program_id(0) == 0)
    def _():
        tri = (lax.broadcasted_iota(jnp.int32, (CHUNK, CHUNK), 0)
               >= lax.broadcasted_iota(jnp.int32, (CHUNK, CHUNK), 1))
        for g in range(N_GROUPS_B):
            wm_ref[g] = jnp.where(tri, ws_ref[g], 0.0).astype(BF16)

    vn = _rms(v_ref[...].astype(F32), gv_ref[...]).astype(BF16)
    for c in range(n_chunks):
        rs = slice(c * CHUNK, (c + 1) * CHUNK)
        for g in range(N_GROUPS_B):
            cs = slice(g * GROUP_B, (g + 1) * GROUP_B)
            mixed = jnp.dot(wm_ref[g], vn[rs, cs], preferred_element_type=F32) + bias_ref[:, cs]
            yb_ref[rs, cs] = (u_ref[rs, cs].astype(F32) * mixed).astype(BF16)


def _branch_b(grp, gel, gv, ws, bias_full):
    tm = 256
    return pl.pallas_call(
        functools.partial(_branch_b_kernel, n_chunks=tm // CHUNK),
        grid=(grp.m // tm,),
        in_specs=[
            pl.BlockSpec((tm, D_MODEL), lambda i: (i, 1)),
            pl.BlockSpec((tm, D_MODEL), lambda i: (i, 2)),
            pl.BlockSpec((1, D_MODEL), lambda i: (0, 0)),
            pl.BlockSpec((N_GROUPS_B, CHUNK, CHUNK), lambda i: (0, 0, 0)),
            pl.BlockSpec((CHUNK, D_MODEL), lambda i: (0, 0)),
        ],
        out_specs=pl.BlockSpec((tm, D_MODEL), lambda i: (i, 0)),
        out_shape=jax.ShapeDtypeStruct((grp.m, D_MODEL), BF16),
        scratch_shapes=[pltpu.VMEM((N_GROUPS_B, CHUNK, CHUNK), BF16)],
        compiler_params=_params(("arbitrary",),
                                [((tm, D_MODEL), BF16)] * 3 + [((N_GROUPS_B, CHUNK, CHUNK), F32),
                                                                ((CHUNK, D_MODEL), F32)],
                                [((tm, D_MODEL), F32)] * 2),
        name="branch_b",
    )(gel, gel, gv.reshape(1, D_MODEL), ws, bias_full)


def _branch_b_short_kernel(u_ref, v_ref, gv_ref, wc_ref, bias_ref, yb_ref, vn_ref, *, nb, t):
    vn = _rms(v_ref[...].astype(F32), gv_ref[...]).reshape(nb, t, D_MODEL)
    vn_ref[...] = vn
    mixed = bias_ref[...][None]
    for s in range(t):
        mixed = mixed + wc_ref[s][None] * vn[:, s:s + 1, :]
    yb = u_ref[...].astype(F32).reshape(nb, t, D_MODEL) * mixed
    yb_ref[...] = yb.reshape(nb * t, D_MODEL).astype(BF16)


def _branch_b_short(grp, gel, gv, ws, bs):
    b, t = grp.b, grp.t
    nb = 16
    rows = nb * t
    tri = jnp.tril(jnp.ones((t, t), F32))
    wc = jnp.repeat(jnp.transpose(ws[:, :t, :t] * tri, (2, 1, 0)), GROUP_B, axis=-1)
    bias = jnp.repeat(bs[:, :t].T, GROUP_B, axis=-1)
    return pl.pallas_call(
        functools.partial(_branch_b_short_kernel, nb=nb, t=t),
        grid=(b // nb,),
        in_specs=[
            pl.BlockSpec((rows, D_MODEL), lambda i: (i, 1)),
            pl.BlockSpec((rows, D_MODEL), lambda i: (i, 2)),
            pl.BlockSpec((1, D_MODEL), lambda i: (0, 0)),
            pl.BlockSpec((t, t, D_MODEL), lambda i: (0, 0, 0)),
            pl.BlockSpec((t, D_MODEL), lambda i: (0, 0)),
        ],
        out_specs=[
            pl.BlockSpec((rows, D_MODEL), lambda i: (i, 0)),
            pl.BlockSpec((nb, t, D_MODEL), lambda i: (i, 0, 0)),
        ],
        out_shape=[
            jax.ShapeDtypeStruct((grp.m, D_MODEL), BF16),
            jax.ShapeDtypeStruct((b, t, D_MODEL), F32),
        ],
        compiler_params=_params(("arbitrary",),
                                [((rows, D_MODEL), BF16)] * 3 + [((rows, D_MODEL), F32),
                                                                  ((t, t, D_MODEL), F32)],
                                [((rows, D_MODEL), F32)] * 3),
        name="branch_b_short",
    )(gel, gel, gv.reshape(1, D_MODEL), wc, bias)


def _merge_kernel(ya_ref, yb_ref, sa_ref, sb_ref, wpa_ref, wpb_ref, o_ref, wa_ref, wb_ref):
    @pl.when(pl.program_id(1) == 0)
    def _():
        wa_ref[...] = wpa_ref[...].astype(BF16)
        wb_ref[...] = wpb_ref[...].astype(BF16)

    pa = jnp.dot(ya_ref[...], wa_ref[...], preferred_element_type=F32)
    pb = jnp.dot(yb_ref[...], wb_ref[...], preferred_element_type=F32)
    o_ref[...] = (sa_ref[...].astype(F32) * pa + sb_ref[...].astype(F32) * pb).astype(BF16)


def _merge(grp, ya, yb, sig, wpa, wpb, layer):
    tm, tn = 1024, 512
    nj = D_MODEL // tn
    return pl.pallas_call(
        _merge_kernel,
        grid=(nj, grp.m // tm),
        in_specs=[
            pl.BlockSpec((tm, D_MODEL), lambda j, i: (i, 0)),
            pl.BlockSpec((tm, D_MODEL), lambda j, i: (i, 0)),
            pl.BlockSpec((tm, tn), lambda j, i: (i, j)),
            pl.BlockSpec((tm, tn), lambda j, i: (i, nj + j)),
            pl.BlockSpec((None, D_MODEL, tn), lambda j, i: (layer, 0, j)),
            pl.BlockSpec((None, D_MODEL, tn), lambda j, i: (layer, 0, j)),
        ],
        out_specs=pl.BlockSpec((tm, tn), lambda j, i: (i, j)),
        out_shape=jax.ShapeDtypeStruct((grp.m, D_MODEL), BF16),
        scratch_shapes=[pltpu.VMEM((D_MODEL, tn), BF16), pltpu.VMEM((D_MODEL, tn), BF16)],
        compiler_params=_params(("arbitrary", "arbitrary"),
                                [((tm, D_MODEL), BF16)] * 2 + [((tm, tn), BF16)] * 3
                                + [((D_MODEL, tn), F32)] * 2,
                                [((D_MODEL, tn), BF16)] * 2 + [((tm, tn), F32)] * 2),
        name="merge",
    )(ya, yb, sig, sig, wpa, wpb)


def _post_math(y, x, gpost, gt, nxt):
    x1 = x + gt * _rms(y, gpost)
    if nxt is None:
        return x1, None
    gpre, sc, sh = nxt
    return x1, (_rms(x1, gpre) * (1.0 + sc) + sh).astype(BF16)


def _oproj_post_kernel(m_ref, wo_ref, x_ref, gpost_ref, gt_ref, gpre_ref, sc_ref, sh_ref,
                       xo_ref, xn_ref, wbf_ref):
    @pl.when(pl.program_id(0) == 0)
    def _():
        wbf_ref[...] = wo_ref[...].astype(BF16)

    y = jnp.dot(m_ref[...], wbf_ref[...], preferred_element_type=F32)
    x1, xn = _post_math(y, x_ref[...], gpost_ref[...], gt_ref[...],
                        (gpre_ref[...], sc_ref[...], sh_ref[...]))
    xo_ref[...] = x1
    xn_ref[...] = xn


def _oproj_post(grp, m, wo, layer, x, gpost, gt, gpre, sc, sh):
    tm = 256
    vec = lambda: pl.BlockSpec((1, D_MODEL), lambda i: (0, 0))
    row = lambda: pl.BlockSpec((tm, D_MODEL), lambda i: (i, 0))
    return pl.pallas_call(
        _oproj_post_kernel,
        grid=(grp.m // tm,),
        in_specs=[
            row(),
            pl.BlockSpec((None, D_MODEL, D_MODEL), lambda i: (layer, 0, 0),
                         pipeline_mode=pl.Buffered(1)),
            row(), vec(), grp.mod_spec(tm), vec(), grp.mod_spec(tm), grp.mod_spec(tm),
        ],
        out_specs=[row(), row()],
        out_shape=[
            jax.ShapeDtypeStruct((grp.m, D_MODEL), F32),
            jax.ShapeDtypeStruct((grp.m, D_MODEL), BF16),
        ],
        scratch_shapes=[pltpu.VMEM((D_MODEL, D_MODEL), BF16)],
        compiler_params=_params(("arbitrary",),
                                [((tm, D_MODEL), F32)] * 6,
                                [((D_MODEL, D_MODEL), F32), ((D_MODEL, D_MODEL), BF16)]),
        name="oproj_post",
    )(m, wo, x, gpost.reshape(1, D_MODEL), gt, gpre.reshape(1, D_MODEL), sc, sh)


def _post_kernel(y_ref, x_ref, gpost_ref, gt_ref, *rest, with_next):
    if with_next:
        gpre_ref, sc_ref, sh_ref, xo_ref, xn_ref = rest
        nxt = (gpre_ref[...], sc_ref[...], sh_ref[...])
    else:
        (xo_ref,) = rest
        nxt = None
    x1, xn = _post_math(y_ref[...], x_ref[...], gpost_ref[...], gt_ref[...], nxt)
    xo_ref[...] = x1
    if with_next:
        xn_ref[...] = xn


def _post(grp, y, x, gpost, gt, nxt):
    tm = 256
    vec = lambda: pl.BlockSpec((1, D_MODEL), lambda i: (0, 0))
    row = lambda: pl.BlockSpec((tm, D_MODEL), lambda i: (i, 0))
    in_specs = [row(), row(), vec(), grp.mod_spec(tm)]
    args = [y, x, gpost.reshape(1, D_MODEL), gt]
    out_specs = [row()]
    out_shape = [jax.ShapeDtypeStruct((grp.m, D_MODEL), F32)]
    if nxt is not None:
        gpre, sc, sh = nxt
        in_specs += [vec(), grp.mod_spec(tm), grp.mod_spec(tm)]
        args += [gpre.reshape(1, D_MODEL), sc, sh]
        out_specs.append(row())
        out_shape.append(jax.ShapeDtypeStruct((grp.m, D_MODEL), BF16))
    out = pl.pallas_call(
        functools.partial(_post_kernel, with_next=nxt is not None),
        grid=(grp.m // tm,),
        in_specs=in_specs,
        out_specs=out_specs,
        out_shape=out_shape,
        compiler_params=_params(("arbitrary",), [((tm, D_MODEL), F32)] * 8),
        name="post",
    )(*args)
    return (out[0], out[1]) if nxt is not None else (out[0], None)


def _swiglu_kernel(*refs, gated):
    if gated:
        x_ref, gates_ref, wg_ref, wu_ref, wd_ref, o_ref = refs
    else:
        x_ref, wg_ref, wu_ref, wd_ref, o_ref = refs
    e, j = pl.program_id(1), pl.program_id(2)

    @pl.when((e == 0) & (j == 0))
    def _():
        o_ref[...] = jnp.zeros_like(o_ref)

    x = x_ref[...]
    g = jnp.dot(x, wg_ref[...], preferred_element_type=F32)
    u = jnp.dot(x, wu_ref[...], preferred_element_type=F32)
    h = _silu(g) * u
    if gated:
        gates = gates_ref[...]
        lane = lax.broadcasted_iota(jnp.int32, gates.shape, 1)
        h = h * jnp.sum(jnp.where(lane == e, gates, 0.0), axis=-1, keepdims=True)
    o_ref[...] += jnp.dot(h.astype(BF16), wd_ref[...], preferred_element_type=F32)


def _swiglu(grp, x, wg, wu, wd, gates=None):
    n_e, _, f = wg.shape
    tm, tf = 1024, 512
    gated = gates is not None
    in_specs = [pl.BlockSpec((tm, D_MODEL), lambda i, e, j: (i, 0))]
    args = [x]
    if gated:
        in_specs.append(pl.BlockSpec((tm, LANES), lambda i, e, j: (i, 0)))
        args.append(gates)
    in_specs += [
        pl.BlockSpec((None, D_MODEL, tf), lambda i, e, j: (e, 0, j)),
        pl.BlockSpec((None, D_MODEL, tf), lambda i, e, j: (e, 0, j)),
        pl.BlockSpec((None, tf, D_MODEL), lambda i, e, j: (e, j, 0)),
    ]
    args += [wg, wu, wd]
    return pl.pallas_call(
        functools.partial(_swiglu_kernel, gated=gated),
        grid=(grp.m // tm, n_e, f // tf),
        in_specs=in_specs,
        out_specs=pl.BlockSpec((tm, D_MODEL), lambda i, e, j: (i, 0)),
        out_shape=jax.ShapeDtypeStruct((grp.m, D_MODEL), F32),
        compiler_params=_params(("arbitrary", "arbitrary", "arbitrary"),
                                [((tm, D_MODEL), BF16), ((tm, D_MODEL), F32), ((tm, LANES), F32)]
                                + [((D_MODEL, tf), BF16)] * 3,
                                [((tm, tf), F32)] * 4),
        name="swiglu_moe" if gated else "swiglu_dense",
    )(*args)


def _router_kernel(x_ref, rw_ref, rb_ref, gates_ref):
    logits = jnp.dot(x_ref[...], rw_ref[...].astype(BF16), preferred_element_type=F32) + rb_ref[...]
    lane = lax.broadcasted_iota(jnp.int32, logits.shape, 1)
    logits = jnp.where(lane < N_EXPERTS, logits, -jnp.inf)
    ex = jnp.exp(logits - jnp.max(logits, axis=-1, keepdims=True))
    p = ex / jnp.sum(ex, axis=-1, keepdims=True)
    p1 = jnp.max(p, axis=-1, keepdims=True)
    i1 = jnp.min(jnp.where(p == p1, lane, LANES), axis=-1, keepdims=True)
    rest = jnp.where(lane == i1, -1.0, p)
    p2 = jnp.max(rest, axis=-1, keepdims=True)
    i2 = jnp.min(jnp.where(rest == p2, lane, LANES), axis=-1, keepdims=True)
    denom = p1 + p2
    gates_ref[...] = jnp.where(lane == i1, p1 / denom, 0.0) + jnp.where(lane == i2, p2 / denom, 0.0)


def _router(grp, x, rw, rb):
    tm = 512
    rw_pad = jnp.pad(rw, ((0, 0), (0, LANES - N_EXPERTS)))
    rb_pad = jnp.pad(rb, (0, LANES - N_EXPERTS)).reshape(1, LANES)
    return pl.pallas_call(
        _router_kernel,
        grid=(grp.m // tm,),
        in_specs=[
            pl.BlockSpec((tm, D_MODEL), lambda i: (i, 0)),
            pl.BlockSpec((D_MODEL, LANES), lambda i: (0, 0)),
            pl.BlockSpec((1, LANES), lambda i: (0, 0)),
        ],
        out_specs=pl.BlockSpec((tm, LANES), lambda i: (i, 0)),
        out_shape=jax.ShapeDtypeStruct((grp.m, LANES), F32),
        compiler_params=_params(("arbitrary",),
                                [((tm, D_MODEL), BF16), ((D_MODEL, LANES), F32), ((tm, LANES), F32)]),
        name="router",
    )(x, rw_pad, rb_pad)


def _trunk(grp, x, mods, h0s, conv0s, p, nb, tt):
    depth = p['w_in'].shape[0]
    hs, convs, vs = [], [], []
    sh1, sc1 = mods[0][0], mods[0][1]
    xn = _prenorm(grp, x, p['g_pre_mix'][0], sc1, sh1)
    for l in range(depth):
        _, _, gt1, sh2, sc2, gt2 = mods[l]
        w_in = p['w_in']
        xa = _mm_act(grp, xn, w_in, l, 0, D_MODEL, lambda y: y, F32, "in_proj_xa")
        gel = _mm_act(grp, xn, w_in, l, D_MODEL, 3 * D_MODEL, _gelu, BF16, "in_proj_gelu")
        sig = _mm_act(grp, xn, w_in, l, 4 * D_MODEL, 2 * D_MODEL, _sigmoid, BF16, "in_proj_sigmoid")
        ya, h_last, conv_new = _branch_a(
            grp, nb, tt, xa, gel, conv0s[l], h0s[l], p['conv_w'][l], p['conv_b'][l],
            p['w_r'][l], p['w_i'][l], p['b_r'][l], p['b_i'][l], p['lru_lambda'][l])
        if grp.t >= CHUNK:
            bias_full = jnp.repeat(p['b_s'][l].T, GROUP_B, axis=-1)
            yb = _branch_b(grp, gel, p['g_v'][l], p['w_s'][l], bias_full)
            vn = None
        else:
            yb, vn = _branch_b_short(grp, gel, p['g_v'][l], p['w_s'][l], p['b_s'][l])
        m = _merge(grp, ya, yb, sig, p['w_pa'], p['w_pb'], l)
        x, xn2 = _oproj_post(grp, m, p['w_o'], l, x, p['g_post_mix'][l], gt1,
                             p['g_pre_ffn'][l], sc2, sh2)
        j = l // 2
        if l % 2 == 0:
            y = _swiglu(grp, xn2, p['ffn_wg_bf'][j:j + 1], p['ffn_wu_bf'][j:j + 1], p['ffn_wd_bf'][j:j + 1])
        else:
            gates = _router(grp, xn2, p['router_w'][j], p['router_b'][j])
            y = _swiglu(grp, xn2, p['moe_wg_bf'][j], p['moe_wu_bf'][j], p['moe_wd_bf'][j], gates)
        if l + 1 < depth:
            nsh1, nsc1 = mods[l + 1][0], mods[l + 1][1]
            nxt = (p['g_pre_mix'][l + 1], nsc1, nsh1)
        else:
            nxt = None
        x, xn = _post(grp, y, x, p['g_post_ffn'][l], gt2, nxt)
        hs.append(h_last.reshape(grp.b, D_MODEL))
        convs.append(conv_new)
        vs.append(vn)
    return x, jnp.stack(hs), jnp.stack(convs), vs


def kernel(x_prompt, x_sample, c_prompt, c_sample, state_lru_h, state_lru_conv, w_ada, b_ada, g_pre_mix, g_post_mix, g_pre_ffn, g_post_ffn, w_in, conv_w, conv_b, w_r, b_r, w_i, b_i, lru_lambda, g_v, w_s, b_s, w_pa, w_pb, w_o, ffn_wg, ffn_wu, ffn_wd, router_w, router_b, moe_wg, moe_wu, moe_wd):
    p = dict(g_pre_mix=g_pre_mix, g_post_mix=g_post_mix, g_pre_ffn=g_pre_ffn, g_post_ffn=g_post_ffn,
             w_in=w_in, conv_w=conv_w, conv_b=conv_b, w_r=w_r, b_r=b_r, w_i=w_i, b_i=b_i,
             lru_lambda=lru_lambda, g_v=g_v, w_s=w_s, b_s=b_s, w_pa=w_pa, w_pb=w_pb, w_o=w_o,
             router_w=router_w, router_b=router_b,
             ffn_wg_bf=ffn_wg.astype(BF16), ffn_wu_bf=ffn_wu.astype(BF16), ffn_wd_bf=ffn_wd.astype(BF16),
             moe_wg_bf=moe_wg.astype(BF16), moe_wu_bf=moe_wu.astype(BF16), moe_wd_bf=moe_wd.astype(BF16))
    depth = w_in.shape[0]
    bp, tp, _ = x_prompt.shape
    bs, ts, _ = x_sample.shape
    assert tp % CHUNK == 0 and ts < CHUNK and (ts & (ts - 1)) == 0

    n_c = bp + bs
    r_pad = -n_c % 16
    c_all = jnp.concatenate([c_prompt, c_sample, jnp.zeros((r_pad, D_MODEL), F32)], axis=0)
    mod = _ada(c_all, w_ada, b_ada).reshape(depth, n_c + r_pad, 6, D_MODEL)
    mods_p = [[mod[l, :bp, k].reshape(bp, 1, D_MODEL) for k in range(6)] for l in range(depth)]
    mods_s = [[jnp.repeat(mod[l, bp:n_c, k], ts, axis=0) for k in range(6)] for l in range(depth)]

    grp_p = _Group(bp, tp, per_token_mod=False)
    grp_s = _Group(bs, ts, per_token_mod=True)
    zeros_h = jnp.zeros((depth, bp, D_MODEL), F32)
    zeros_conv = jnp.zeros((depth, bp, CONV_W - 1, D_MODEL), F32)

    y_p, h_p, conv_p, _ = _trunk(grp_p, x_prompt.reshape(bp * tp, D_MODEL), mods_p, zeros_h, zeros_conv,
                                 p, nb=1, tt=256)
    y_s, h_s, conv_s, v_s = _trunk(grp_s, x_sample.reshape(bs * ts, D_MODEL), mods_s, state_lru_h,
                                   state_lru_conv, p, nb=16, tt=ts)
    return (y_p.reshape(bp, tp, D_MODEL), y_s.reshape(bs, ts, D_MODEL), h_p, conv_p, h_s, conv_s,
            jnp.stack(v_s))
```

```python
import functools

import jax
import jax.numpy as jnp
from jax import lax
from jax.experimental import pallas as pl
from jax.experimental.pallas import tpu as pltpu

F32 = jnp.float32
BF16 = jnp.bfloat16

D_MODEL = 2048
N_HEADS_A = 16
HEAD_A = D_MODEL // N_HEADS_A
CONV_W = 4
C_GATE = 8.0
N_GROUPS_B = 16
GROUP_B = D_MODEL // N_GROUPS_B
CHUNK = 128
N_EXPERTS = 8
EPS = 1e-6

LANES = 128
SUBLANES = 8
VMEM_CAP = 58 * 2**20
CONV_PAD = SUBLANES


def _nbytes(shape, dtype):
    n = 1
    for s in shape:
        n *= s
    return n * jnp.dtype(dtype).itemsize


def _params(sem, blocks, scratch=()):
    need = 2 * sum(_nbytes(s, d) for s, d in blocks) + sum(_nbytes(s, d) for s, d in scratch)
    limit = min(VMEM_CAP, need + 16 * 2**20)
    return pltpu.CompilerParams(dimension_semantics=sem, vmem_limit_bytes=limit)


def _rms(x, g):
    return x * lax.rsqrt(jnp.mean(x * x, axis=-1, keepdims=True) + EPS) * g


def _sigmoid(x):
    return 1.0 / (1.0 + jnp.exp(-x))


def _silu(x):
    return x * _sigmoid(x)


def _gelu(x):
    return jax.nn.gelu(x)


class _Group:
    def __init__(self, b, t, per_token_mod):
        self.b, self.t, self.m = b, t, b * t
        self.per_token_mod = per_token_mod

    def mod_spec(self, tm, nd_grid=1, axis=0):
        if self.per_token_mod:
            def imap(*ids):
                return (ids[axis], 0)
            return pl.BlockSpec((tm, D_MODEL), imap)
        t = self.t

        def imap(*ids):
            return ((ids[axis] * tm) // t, 0, 0)
        return pl.BlockSpec((None, 1, D_MODEL), imap)


def _ada_kernel(c_ref, w_ref, b_ref, o_ref):
    s = _silu(c_ref[...]).astype(BF16)
    o_ref[...] = jnp.dot(s, w_ref[...].astype(BF16), preferred_element_type=F32) + b_ref[...]


def _ada(c, w_ada, b_ada):
    depth, _, n = w_ada.shape
    r = c.shape[0]
    tn = 1024
    return pl.pallas_call(
        _ada_kernel,
        grid=(depth, n // tn),
        in_specs=[
            pl.BlockSpec((r, D_MODEL), lambda l, j: (0, 0)),
            pl.BlockSpec((None, D_MODEL, tn), lambda l, j: (l, 0, j)),
            pl.BlockSpec((None, 1, tn), lambda l, j: (l, 0, j)),
        ],
        out_specs=pl.BlockSpec((None, r, tn), lambda l, j: (l, 0, j)),
        out_shape=jax.ShapeDtypeStruct((depth, r, n), F32),
        compiler_params=_params(("arbitrary", "arbitrary"),
                                [((r, D_MODEL), F32), ((D_MODEL, tn), F32), ((r, tn), F32)],
                                [((D_MODEL, tn), BF16)]),
        name="ada_mod",
    )(c, w_ada, b_ada.reshape(depth, 1, n))


def _prenorm_kernel(x_ref, g_ref, sc_ref, sh_ref, o_ref):
    o_ref[...] = (_rms(x_ref[...], g_ref[...]) * (1.0 + sc_ref[...]) + sh_ref[...]).astype(BF16)


def _prenorm(grp, x, g, sc, sh):
    tm = 512
    return pl.pallas_call(
        _prenorm_kernel,
        grid=(grp.m // tm,),
        in_specs=[
            pl.BlockSpec((tm, D_MODEL), lambda i: (i, 0)),
            pl.BlockSpec((1, D_MODEL), lambda i: (0, 0)),
            grp.mod_spec(tm), grp.mod_spec(tm),
        ],
        out_specs=pl.BlockSpec((tm, D_MODEL), lambda i: (i, 0)),
        out_shape=jax.ShapeDtypeStruct((grp.m, D_MODEL), BF16),
        compiler_params=_params(("arbitrary",), [((tm, D_MODEL), F32)] * 4),
        name="prenorm",
    )(x, g.reshape(1, D_MODEL), sc, sh)


def _mm_act_kernel(x_ref, w_ref, o_ref, wbf_ref, *, act):
    @pl.when(pl.program_id(1) == 0)
    def _():
        wbf_ref[...] = w_ref[...].astype(BF16)

    y = jnp.dot(x_ref[...], wbf_ref[...], preferred_element_type=F32)
    o_ref[...] = act(y).astype(o_ref.dtype)


def _mm_act(grp, x, w, layer, col0, ncols, act, out_dtype, name):
    tm, tn = 1024, 1024
    c0 = col0 // tn
    return pl.pallas_call(
        functools.partial(_mm_act_kernel, act=act),
        grid=(ncols // tn, grp.m // tm),
        in_specs=[
            pl.BlockSpec((tm, D_MODEL), lambda j, i: (i, 0)),
            pl.BlockSpec((None, D_MODEL, tn), lambda j, i: (layer, 0, c0 + j)),
        ],
        out_specs=pl.BlockSpec((tm, tn), lambda j, i: (i, j)),
        out_shape=jax.ShapeDtypeStruct((grp.m, ncols), out_dtype),
        scratch_shapes=[pltpu.VMEM((D_MODEL, tn), BF16)],
        compiler_params=_params(("arbitrary", "arbitrary"),
                                [((tm, D_MODEL), BF16), ((D_MODEL, tn), F32), ((tm, tn), F32)],
                                [((D_MODEL, tn), BF16), ((tm, tn), F32)]),
        name=name,
    )(x, w)


def _branch_a_kernel(xa_ref, ga_ref, conv0_ref, h0_ref, cw_ref, cb_ref, wr_ref, wi_ref,
                     br_ref, bi_ref, lam_ref, ya_ref, hlast_ref, convnew_ref, xp_ref, h_ref,
                     *, nb, tt, n_t):
    t_idx = pl.program_id(1)
    hist = CONV_W - 1
    lo = CONV_PAD - hist

    @pl.when(t_idx == 0)
    def _():
        xp_ref[:, lo:CONV_PAD, :] = conv0_ref[...]
        h_ref[...] = h0_ref[...]

    xp_ref[:, CONV_PAD:CONV_PAD + tt, :] = xa_ref[...]
    rows = nb * tt
    n_blk = tt // SUBLANES
    t_in_blk = lax.broadcasted_iota(jnp.int32, (rows, HEAD_A), 0) & (SUBLANES - 1)

    for hd in range(N_HEADS_A):
        sl = slice(hd * HEAD_A, (hd + 1) * HEAD_A)
        xc = cb_ref[:, sl][None]
        for k in range(CONV_W):
            xc = xc + cw_ref[k:k + 1, sl][None] * xp_ref[:, lo + k:lo + k + tt, sl]
        xc = xc.reshape(rows, HEAD_A)
        xcb = xc.astype(BF16)
        r = _sigmoid(jnp.dot(xcb, wr_ref[hd].astype(BF16), preferred_element_type=F32) + br_ref[:, sl])
        i = _sigmoid(jnp.dot(xcb, wi_ref[hd].astype(BF16), preferred_element_type=F32) + bi_ref[:, sl])
        z = -lam_ref[:, sl]
        softplus = jnp.maximum(z, 0.0) + jnp.log1p(jnp.exp(-jnp.abs(z)))
        log_a = (-C_GATE * r) * softplus
        a = jnp.exp(log_a)
        th = jnp.tanh(log_a)
        one_m_a2 = -2.0 * th / (1.0 - th)
        b = jnp.where(one_m_a2 > 0.0, one_m_a2 * lax.rsqrt(one_m_a2), 0.0) * (i * xc)
        d = 1
        while d < SUBLANES:
            a_sh = pltpu.roll(a, d, axis=0)
            b_sh = pltpu.roll(b, d, axis=0)
            keep = t_in_blk >= d
            b = jnp.where(keep, a * b_sh + b, b)
            a = jnp.where(keep, a * a_sh, a)
            d *= 2
        a3 = a.reshape(nb, tt, HEAD_A)
        b3 = b.reshape(nb, tt, HEAD_A)
        carry = h_ref[:, :, sl]
        h_blocks = []
        for k in range(n_blk):
            ks = slice(k * SUBLANES, (k + 1) * SUBLANES)
            h_k = a3[:, ks, :] * carry + b3[:, ks, :]
            carry = h_k[:, SUBLANES - 1:SUBLANES, :]
            h_blocks.append(h_k)
        h_ref[:, :, sl] = carry
        h = jnp.concatenate(h_blocks, axis=1) if n_blk > 1 else h_blocks[0]
        ya_ref[:, sl] = (ga_ref[:, sl].astype(F32) * h.reshape(rows, HEAD_A)).astype(BF16)

    xp_ref[:, lo:CONV_PAD, :] = xp_ref[:, lo + tt:CONV_PAD + tt, :]

    @pl.when(t_idx == n_t - 1)
    def _():
        hlast_ref[...] = h_ref[...]
        convnew_ref[...] = xp_ref[:, lo:CONV_PAD, :]


def _branch_a(grp, nb, tt, xa, gel, conv0, h0, cw, cb, wr, wi, br, bi, lam):
    b, t = grp.b, grp.t
    n_t = t // tt
    rows = nb * tt
    hist = CONV_W - 1
    vec = lambda: pl.BlockSpec((1, D_MODEL), lambda bi_, ti: (0, 0))
    head_w = lambda: pl.BlockSpec((N_HEADS_A, HEAD_A, HEAD_A), lambda bi_, ti: (0, 0, 0))
    return pl.pallas_call(
        functools.partial(_branch_a_kernel, nb=nb, tt=tt, n_t=n_t),
        grid=(b // nb, n_t),
        in_specs=[
            pl.BlockSpec((nb, tt, D_MODEL), lambda bi_, ti: (bi_, ti, 0)),
            pl.BlockSpec((rows, D_MODEL), lambda bi_, ti: (bi_ * n_t + ti, 0)),
            pl.BlockSpec((nb, hist, D_MODEL), lambda bi_, ti: (bi_, 0, 0)),
            pl.BlockSpec((nb, 1, D_MODEL), lambda bi_, ti: (bi_, 0, 0)),
            pl.BlockSpec((CONV_W, D_MODEL), lambda bi_, ti: (0, 0)),
            vec(), head_w(), head_w(), vec(), vec(), vec(),
        ],
        out_specs=[
            pl.BlockSpec((rows, D_MODEL), lambda bi_, ti: (bi_ * n_t + ti, 0)),
            pl.BlockSpec((nb, 1, D_MODEL), lambda bi_, ti: (bi_, 0, 0)),
            pl.BlockSpec((nb, hist, D_MODEL), lambda bi_, ti: (bi_, 0, 0)),
        ],
        out_shape=[
            jax.ShapeDtypeStruct((grp.m, D_MODEL), BF16),
            jax.ShapeDtypeStruct((b, 1, D_MODEL), F32),
            jax.ShapeDtypeStruct((b, hist, D_MODEL), F32),
        ],
        scratch_shapes=[
            pltpu.VMEM((nb, CONV_PAD + tt, D_MODEL), F32),
            pltpu.VMEM((nb, 1, D_MODEL), F32),
        ],
        compiler_params=_params(("arbitrary", "arbitrary"),
                                [((rows, D_MODEL), F32), ((rows, D_MODEL), BF16), ((rows, D_MODEL), BF16),
                                 ((nb, 8, D_MODEL), F32), ((nb, 8, D_MODEL), F32),
                                 ((2 * N_HEADS_A, HEAD_A, HEAD_A), F32)],
                                [((nb, CONV_PAD + tt, D_MODEL), F32), ((nb, 8, D_MODEL), F32)]),
        name="branch_a",
    )(xa.reshape(b, t, D_MODEL), gel, conv0, h0.reshape(b, 1, D_MODEL), cw, cb.reshape(1, D_MODEL),
      wr, wi, br.reshape(1, D_MODEL), bi.reshape(1, D_MODEL), lam.reshape(1, D_MODEL))


def _branch_b_kernel(u_ref, v_ref, gv_ref, ws_ref, bias_ref, yb_ref, wm_ref, *, n_chunks):
    @pl.when(pl.program_id(0) == 0)
    def _():
        tri = (lax.broadcasted_iota(jnp.int32, (CHUNK, CHUNK), 0)
               >= lax.broadcasted_iota(jnp.int32, (CHUNK, CHUNK), 1))
        for g in range(N_GROUPS_B):
            wm_ref[g] = jnp.where(tri, ws_ref[g], 0.0).astype(BF16)

    vn = _rms(v_ref[...].astype(F32), gv_ref[...]).astype(BF16)
    for c in range(n_chunks):
        rs = slice(c * CHUNK, (c + 1) * CHUNK)
        for g in range(N_GROUPS_B):
            cs = slice(g * GROUP_B, (g + 1) * GROUP_B)
            mixed = jnp.dot(wm_ref[g], vn[rs, cs], preferred_element_type=F32) + bias_ref[:, cs]
            yb_ref[rs, cs] = (u_ref[rs, cs].astype(F32) * mixed).astype(BF16)


def _branch_b(grp, gel, gv, ws, bias_full):
    tm = 256
    return pl.pallas_call(
        functools.partial(_branch_b_kernel, n_chunks=tm // CHUNK),
        grid=(grp.m // tm,),
        in_specs=[
            pl.BlockSpec((tm, D_MODEL), lambda i: (i, 1)),
            pl.BlockSpec((tm, D_MODEL), lambda i: (i, 2)),
            pl.BlockSpec((1, D_MODEL), lambda i: (0, 0)),
            pl.BlockSpec((N_GROUPS_B, CHUNK, CHUNK), lambda i: (0, 0, 0)),
            pl.BlockSpec((CHUNK, D_MODEL), lambda i: (0, 0)),
        ],
        out_specs=pl.BlockSpec((tm, D_MODEL), lambda i: (i, 0)),
        out_shape=jax.ShapeDtypeStruct((grp.m, D_MODEL), BF16),
        scratch_shapes=[pltpu.VMEM((N_GROUPS_B, CHUNK, CHUNK), BF16)],
        compiler_params=_params(("arbitrary",),
                                [((tm, D_MODEL), BF16)] * 3 + [((N_GROUPS_B, CHUNK, CHUNK), F32),
                                                                ((CHUNK, D_MODEL), F32)],
                                [((tm, D_MODEL), F32)] * 2),
        name="branch_b",
    )(gel, gel, gv.reshape(1, D_MODEL), ws, bias_full)


def _branch_b_short_kernel(u_ref, v_ref, gv_ref, wc_ref, bias_ref, yb_ref, vn_ref, *, nb, t):
    vn = _rms(v_ref[...].astype(F32), gv_ref[...]).reshape(nb, t, D_MODEL)
    vn_ref[...] = vn
    mixed = bias_ref[...][None]
    for s in range(t):
        mixed = mixed + wc_ref[s][None] * vn[:, s:s + 1, :]
    yb = u_ref[...].astype(F32).reshape(nb, t, D_MODEL) * mixed
    yb_ref[...] = yb.reshape(nb * t, D_MODEL).astype(BF16)


def _branch_b_short(grp, gel, gv, ws, bs):
    b, t = grp.b, grp.t
    nb = 16
    rows = nb * t
    tri = jnp.tril(jnp.ones((t, t), F32))
    wc = jnp.repeat(jnp.transpose(ws[:, :t, :t] * tri, (2, 1, 0)), GROUP_B, axis=-1)
    bias = jnp.repeat(bs[:, :t].T, GROUP_B, axis=-1)
    return pl.pallas_call(
        functools.partial(_branch_b_short_kernel, nb=nb, t=t),
        grid=(b // nb,),
        in_specs=[
            pl.BlockSpec((rows, D_MODEL), lambda i: (i, 1)),
            pl.BlockSpec((rows, D_MODEL), lambda i: (i, 2)),
            pl.BlockSpec((1, D_MODEL), lambda i: (0, 0)),
            pl.BlockSpec((t, t, D_MODEL), lambda i: (0, 0, 0)),
            pl.BlockSpec((t, D_MODEL), lambda i: (0, 0)),
        ],
        out_specs=[
            pl.BlockSpec((rows, D_MODEL), lambda i: (i, 0)),
            pl.BlockSpec((nb, t, D_MODEL), lambda i: (i, 0, 0)),
        ],
        out_shape=[
            jax.ShapeDtypeStruct((grp.m, D_MODEL), BF16),
            jax.ShapeDtypeStruct((b, t, D_MODEL), F32),
        ],
        compiler_params=_params(("arbitrary",),
                                [((rows, D_MODEL), BF16)] * 3 + [((rows, D_MODEL), F32),
                                                                  ((t, t, D_MODEL), F32)],
                                [((rows, D_MODEL), F32)] * 3),
        name="branch_b_short",
    )(gel, gel, gv.reshape(1, D_MODEL), wc, bias)


def _merge_kernel(ya_ref, yb_ref, sa_ref, sb_ref, wpa_ref, wpb_ref, o_ref, wa_ref, wb_ref):
    @pl.when(pl.program_id(1) == 0)
    def _():
        wa_ref[...] = wpa_ref[...].astype(BF16)
        wb_ref[...] = wpb_ref[...].astype(BF16)

    pa = jnp.dot(ya_ref[...], wa_ref[...], preferred_element_type=F32)
    pb = jnp.dot(yb_ref[...], wb_ref[...], preferred_element_type=F32)
    o_ref[...] = (sa_ref[...].astype(F32) * pa + sb_ref[...].astype(F32) * pb).astype(BF16)


def _merge(grp, ya, yb, sig, wpa, wpb, layer):
    tm, tn = 1024, 512
    nj = D_MODEL // tn
    return pl.pallas_call(
        _merge_kernel,
        grid=(nj, grp.m // tm),
        in_specs=[
            pl.BlockSpec((tm, D_MODEL), lambda j, i: (i, 0)),
            pl.BlockSpec((tm, D_MODEL), lambda j, i: (i, 0)),
            pl.BlockSpec((tm, tn), lambda j, i: (i, j)),
            pl.BlockSpec((tm, tn), lambda j, i: (i, nj + j)),
            pl.BlockSpec((None, D_MODEL, tn), lambda j, i: (layer, 0, j)),
            pl.BlockSpec((None, D_MODEL, tn), lambda j, i: (layer, 0, j)),
        ],
        out_specs=pl.BlockSpec((tm, tn), lambda j, i: (i, j)),
        out_shape=jax.ShapeDtypeStruct((grp.m, D_MODEL), BF16),
        scratch_shapes=[pltpu.VMEM((D_MODEL, tn), BF16), pltpu.VMEM((D_MODEL, tn), BF16)],
        compiler_params=_params(("arbitrary", "arbitrary"),
                                [((tm, D_MODEL), BF16)] * 2 + [((tm, tn), BF16)] * 3
                                + [((D_MODEL, tn), F32)] * 2,
                                [((D_MODEL, tn), BF16)] * 2 + [((tm, tn), F32)] * 2),
        name="merge",
    )(ya, yb, sig, sig, wpa, wpb)


def _post_math(y, x, gpost, gt, nxt):
    x1 = x + gt * _rms(y, gpost)
    if nxt is None:
        return x1, None
    gpre, sc, sh = nxt
    return x1, _rms(x1, gpre) * (1.0 + sc) + sh


def _oproj_post_kernel(m_ref, wo_ref, x_ref, gpost_ref, gt_ref, gpre_ref, sc_ref, sh_ref,
                       xo_ref, xn_ref, wbf_ref):
    @pl.when(pl.program_id(0) == 0)
    def _():
        wbf_ref[...] = wo_ref[...].astype(BF16)

    y = jnp.dot(m_ref[...], wbf_ref[...], preferred_element_type=F32)
    x1, xn = _post_math(y, x_ref[...], gpost_ref[...], gt_ref[...],
                        (gpre_ref[...], sc_ref[...], sh_ref[...]))
    xo_ref[...] = x1
    xn_ref[...] = xn.astype(xn_ref.dtype)


def _oproj_post(grp, m, wo, layer, x, gpost, gt, gpre, sc, sh, xn_dtype):
    tm = 256
    vec = lambda: pl.BlockSpec((1, D_MODEL), lambda i: (0, 0))
    row = lambda: pl.BlockSpec((tm, D_MODEL), lambda i: (i, 0))
    return pl.pallas_call(
        _oproj_post_kernel,
        grid=(grp.m // tm,),
        in_specs=[
            row(),
            pl.BlockSpec((None, D_MODEL, D_MODEL), lambda i: (layer, 0, 0),
                         pipeline_mode=pl.Buffered(1)),
            row(), vec(), grp.mod_spec(tm), vec(), grp.mod_spec(tm), grp.mod_spec(tm),
        ],
        out_specs=[row(), row()],
        out_shape=[
            jax.ShapeDtypeStruct((grp.m, D_MODEL), F32),
            jax.ShapeDtypeStruct((grp.m, D_MODEL), xn_dtype),
        ],
        scratch_shapes=[pltpu.VMEM((D_MODEL, D_MODEL), BF16)],
        compiler_params=_params(("arbitrary",),
                                [((tm, D_MODEL), F32)] * 6,
                                [((D_MODEL, D_MODEL), F32), ((D_MODEL, D_MODEL), BF16)]),
        name="oproj_post",
    )(m, wo, x, gpost.reshape(1, D_MODEL), gt, gpre.reshape(1, D_MODEL), sc, sh)


def _post_kernel(y_ref, x_ref, gpost_ref, gt_ref, *rest, with_next):
    if with_next:
        gpre_ref, sc_ref, sh_ref, xo_ref, xn_ref = rest
        nxt = (gpre_ref[...], sc_ref[...], sh_ref[...])
    else:
        (xo_ref,) = rest
        nxt = None
    x1, xn = _post_math(y_ref[...], x_ref[...], gpost_ref[...], gt_ref[...], nxt)
    xo_ref[...] = x1
    if with_next:
        xn_ref[...] = xn.astype(xn_ref.dtype)


def _post_specs(grp, tm, x, gpost, gt, nxt, imap):
    vec = lambda: pl.BlockSpec((1, D_MODEL), lambda *ids: (0, 0))
    row = lambda: pl.BlockSpec((tm, D_MODEL), imap)
    in_specs = [row(), vec(), grp.mod_spec(tm)]
    args = [x, gpost.reshape(1, D_MODEL), gt]
    out_specs = [row()]
    out_shape = [jax.ShapeDtypeStruct((grp.m, D_MODEL), F32)]
    if nxt is not None:
        gpre, sc, sh = nxt
        in_specs += [vec(), grp.mod_spec(tm), grp.mod_spec(tm)]
        args += [gpre.reshape(1, D_MODEL), sc, sh]
        out_specs.append(row())
        out_shape.append(jax.ShapeDtypeStruct((grp.m, D_MODEL), BF16))
    return in_specs, args, out_specs, out_shape


def _post(grp, y, x, gpost, gt, nxt):
    tm = 256
    imap = lambda i: (i, 0)
    in_specs, args, out_specs, out_shape = _post_specs(grp, tm, x, gpost, gt, nxt, imap)
    out = pl.pallas_call(
        functools.partial(_post_kernel, with_next=nxt is not None),
        grid=(grp.m // tm,),
        in_specs=[pl.BlockSpec((tm, D_MODEL), imap)] + in_specs,
        out_specs=out_specs,
        out_shape=out_shape,
        compiler_params=_params(("arbitrary",), [((tm, D_MODEL), F32)] * 8),
        name="post",
    )(y, *args)
    return (out[0], out[1]) if nxt is not None else (out[0], None)


def _swiglu_tile(x, wg_ref, wu_ref, wd_ref):
    g = jnp.dot(x, wg_ref[...], preferred_element_type=F32)
    u = jnp.dot(x, wu_ref[...], preferred_element_type=F32)
    h = _silu(g) * u
    return jnp.dot(h.astype(BF16), wd_ref[...], preferred_element_type=F32)


def _swiglu_kernel(x_ref, wg_ref, wu_ref, wd_ref, o_ref):
    @pl.when(pl.program_id(1) == 0)
    def _():
        o_ref[...] = jnp.zeros_like(o_ref)

    o_ref[...] += _swiglu_tile(x_ref[...], wg_ref, wu_ref, wd_ref)


def _swiglu(grp, x, wg, wu, wd):
    f = wg.shape[-1]
    tm, tf = 1024, 512
    return pl.pallas_call(
        _swiglu_kernel,
        grid=(grp.m // tm, f // tf),
        in_specs=[
            pl.BlockSpec((tm, D_MODEL), lambda i, j: (i, 0)),
            pl.BlockSpec((D_MODEL, tf), lambda i, j: (0, j)),
            pl.BlockSpec((D_MODEL, tf), lambda i, j: (0, j)),
            pl.BlockSpec((tf, D_MODEL), lambda i, j: (j, 0)),
        ],
        out_specs=pl.BlockSpec((tm, D_MODEL), lambda i, j: (i, 0)),
        out_shape=jax.ShapeDtypeStruct((grp.m, D_MODEL), F32),
        compiler_params=_params(("arbitrary", "arbitrary"),
                                [((tm, D_MODEL), BF16), ((tm, D_MODEL), F32)]
                                + [((D_MODEL, tf), BF16)] * 3,
                                [((tm, tf), F32)] * 4),
        name="swiglu_dense",
    )(x, wg, wu, wd)


META_E1, META_E2, META_R1, META_R2, META_W1, META_W2 = range(6)


def _split_bf16(v):
    hi = v.astype(BF16)
    return hi, (v - hi.astype(F32)).astype(BF16)


def _router_kernel(x_ref, rw_ref, rb_ref, meta_ref, cnt_ref, tri_ref):
    tm = x_ref.shape[0]

    @pl.when(pl.program_id(0) == 0)
    def _():
        cnt_ref[...] = jnp.zeros_like(cnt_ref)
        tri_ref[...] = (lax.broadcasted_iota(jnp.int32, (tm, tm), 0)
                        >= lax.broadcasted_iota(jnp.int32, (tm, tm), 1)).astype(BF16)

    xh, xl = _split_bf16(x_ref[...])
    wh, wl = _split_bf16(rw_ref[...])
    logits = (jnp.dot(xh, wh, preferred_element_type=F32) + jnp.dot(xl, wh, preferred_element_type=F32)
              + jnp.dot(xh, wl, preferred_element_type=F32)) + rb_ref[...]
    lane = lax.broadcasted_iota(jnp.int32, logits.shape, 1)
    logits = jnp.where(lane < N_EXPERTS, logits, -jnp.inf)
    ex = jnp.exp(logits - jnp.max(logits, axis=-1, keepdims=True))
    p = ex / jnp.sum(ex, axis=-1, keepdims=True)
    p1 = jnp.max(p, axis=-1, keepdims=True)
    i1 = jnp.min(jnp.where(p == p1, lane, LANES), axis=-1, keepdims=True)
    rest = jnp.where(lane == i1, -1.0, p)
    p2 = jnp.max(rest, axis=-1, keepdims=True)
    i2 = jnp.min(jnp.where(rest == p2, lane, LANES), axis=-1, keepdims=True)
    denom = p1 + p2
    onehot = jnp.where(lane == i1, 1.0, 0.0) + jnp.where(lane == i2, 1.0, 0.0)
    cum = jnp.dot(tri_ref[...], onehot.astype(BF16), preferred_element_type=F32) + cnt_ref[...]
    cnt_ref[...] = cum[tm - 1:tm, :]
    r1 = jnp.sum(jnp.where(lane == i1, cum, 0.0), axis=-1, keepdims=True) - 1.0
    r2 = jnp.sum(jnp.where(lane == i2, cum, 0.0), axis=-1, keepdims=True) - 1.0
    meta = jnp.zeros_like(logits)
    for k, v in ((META_E1, i1.astype(F32)), (META_E2, i2.astype(F32)), (META_R1, r1), (META_R2, r2),
                 (META_W1, p1 / denom), (META_W2, p2 / denom)):
        meta = jnp.where(lane == k, v, meta)
    meta_ref[...] = meta


def _router(grp, x, rw, rb):
    tm = 512
    rw_pad = jnp.pad(rw, ((0, 0), (0, LANES - N_EXPERTS)))
    rb_pad = jnp.pad(rb, (0, LANES - N_EXPERTS)).reshape(1, LANES)
    return pl.pallas_call(
        _router_kernel,
        grid=(grp.m // tm,),
        in_specs=[
            pl.BlockSpec((tm, D_MODEL), lambda i: (i, 0)),
            pl.BlockSpec((D_MODEL, LANES), lambda i: (0, 0)),
            pl.BlockSpec((1, LANES), lambda i: (0, 0)),
        ],
        out_specs=[
            pl.BlockSpec((tm, LANES), lambda i: (i, 0)),
            pl.BlockSpec((1, LANES), lambda i: (0, 0)),
        ],
        out_shape=[
            jax.ShapeDtypeStruct((grp.m, LANES), F32),
            jax.ShapeDtypeStruct((1, LANES), F32),
        ],
        scratch_shapes=[pltpu.VMEM((tm, tm), BF16)],
        compiler_params=_params(("arbitrary",),
                                [((tm, D_MODEL), F32), ((D_MODEL, LANES), F32), ((tm, LANES), F32)],
                                [((tm, tm), BF16), ((tm, D_MODEL), F32)]),
        name="router",
    )(x, rw_pad, rb_pad)


def _route_plan(meta, counts, tg, n_tiles):
    col = lambda k: meta[:, k].astype(jnp.int32)
    cnt = counts[0, :N_EXPERTS].astype(jnp.int32)
    padded = (cnt + tg - 1) // tg * tg
    ends = jnp.cumsum(padded)
    offs = ends - padded
    pos1 = offs[col(META_E1)] + col(META_R1)
    pos2 = offs[col(META_E2)] + col(META_R2)
    n_used = ends[-1] // tg
    tile = jnp.arange(n_tiles, dtype=jnp.int32)
    tile_c = jnp.minimum(tile, n_used - 1)
    tile_expert = jnp.sum(tile_c[:, None] * tg >= ends[None, :], axis=1).astype(jnp.int32)
    return pos1, pos2, tile_expert, n_used.reshape(1).astype(jnp.int32)


def _dispatch_kernel(pos1_ref, pos2_ref, x_ref, xs_in_ref, xs_ref, sem):
    del xs_in_ref
    tm = x_ref.shape[0]
    base = pl.program_id(0) * tm

    def row_copy(t, slot):
        return pltpu.make_async_copy(x_ref.at[pl.ds(t, 1)], xs_ref.at[pl.ds(slot, 1)], sem)

    def start(t, c):
        row_copy(t, pos1_ref[base + t]).start()
        row_copy(t, pos2_ref[base + t]).start()
        return c

    def wait(t, c):
        row_copy(t, 0).wait()
        row_copy(t, 0).wait()
        return c

    lax.fori_loop(0, tm, start, 0)
    lax.fori_loop(0, tm, wait, 0)


def _dispatch(grp, x, pos1, pos2, n_rows):
    tm = 256
    return pl.pallas_call(
        _dispatch_kernel,
        grid_spec=pltpu.PrefetchScalarGridSpec(
            num_scalar_prefetch=2,
            grid=(grp.m // tm,),
            in_specs=[
                pl.BlockSpec((tm, D_MODEL), lambda i, p1, p2: (i, 0)),
                pl.BlockSpec(memory_space=pl.ANY),
            ],
            out_specs=pl.BlockSpec(memory_space=pl.ANY),
            scratch_shapes=[pltpu.SemaphoreType.DMA(())],
        ),
        out_shape=jax.ShapeDtypeStruct((n_rows, D_MODEL), F32),
        input_output_aliases={3: 0},
        compiler_params=_params(("arbitrary",), [((tm, D_MODEL), F32)]),
        name="moe_dispatch",
    )(pos1, pos2, x, jnp.zeros((n_rows, D_MODEL), F32))


def _experts_kernel(te_ref, nu_ref, xs_ref, wg_ref, wu_ref, wd_ref, o_ref):
    del te_ref
    r, j = pl.program_id(0), pl.program_id(1)

    @pl.when(j == 0)
    def _():
        o_ref[...] = jnp.zeros_like(o_ref)

    @pl.when(r < nu_ref[0])
    def _():
        o_ref[...] += _swiglu_tile(xs_ref[...].astype(BF16), wg_ref, wu_ref, wd_ref)


def _experts(xs, tile_expert, n_used, wg, wu, wd, tg):
    n_rows = xs.shape[0]
    f = wg.shape[-1]
    tf = 512
    n_j = f // tf

    def used(r, nu):
        return r < nu[0]

    def x_map(r, j, te, nu):
        return (jnp.minimum(r, nu[0] - 1), 0)

    def up_map(r, j, te, nu):
        return (te[r], 0, jnp.where(used(r, nu), j, n_j - 1))

    def down_map(r, j, te, nu):
        return (te[r], jnp.where(used(r, nu), j, n_j - 1), 0)

    return pl.pallas_call(
        _experts_kernel,
        grid_spec=pltpu.PrefetchScalarGridSpec(
            num_scalar_prefetch=2,
            grid=(n_rows // tg, n_j),
            in_specs=[
                pl.BlockSpec((tg, D_MODEL), x_map),
                pl.BlockSpec((None, D_MODEL, tf), up_map),
                pl.BlockSpec((None, D_MODEL, tf), up_map),
                pl.BlockSpec((None, tf, D_MODEL), down_map),
            ],
            out_specs=pl.BlockSpec((tg, D_MODEL), lambda r, j, te, nu: (r, 0)),
        ),
        out_shape=jax.ShapeDtypeStruct((n_rows, D_MODEL), F32),
        compiler_params=_params(("arbitrary", "arbitrary"),
                                [((tg, D_MODEL), F32)] * 2 + [((D_MODEL, tf), BF16)] * 3,
                                [((tg, tf), F32)] * 4 + [((tg, D_MODEL), BF16)]),
        name="moe_experts",
    )(tile_expert, n_used, xs, wg, wu, wd)


def _combine_post_kernel(pos1_ref, pos2_ref, ys_ref, meta_ref, x_ref, gpost_ref, gt_ref, *rest,
                         with_next):
    *rest, buf1_ref, buf2_ref, sem = rest
    tm = x_ref.shape[0]
    base = pl.program_id(0) * tm

    def row_copy(slot, buf_ref, t):
        return pltpu.make_async_copy(ys_ref.at[pl.ds(slot, 1)], buf_ref.at[pl.ds(t, 1)], sem)

    def start(t, c):
        row_copy(pos1_ref[base + t], buf1_ref, t).start()
        row_copy(pos2_ref[base + t], buf2_ref, t).start()
        return c

    def wait(t, c):
        row_copy(0, buf1_ref, t).wait()
        row_copy(0, buf2_ref, t).wait()
        return c

    lax.fori_loop(0, tm, start, 0)
    lax.fori_loop(0, tm, wait, 0)
    meta = meta_ref[...]
    y = meta[:, META_W1:META_W1 + 1] * buf1_ref[...] + meta[:, META_W2:META_W2 + 1] * buf2_ref[...]
    if with_next:
        gpre_ref, sc_ref, sh_ref, xo_ref, xn_ref = rest
        nxt = (gpre_ref[...], sc_ref[...], sh_ref[...])
    else:
        (xo_ref,) = rest
        nxt = None
    x1, xn = _post_math(y, x_ref[...], gpost_ref[...], gt_ref[...], nxt)
    xo_ref[...] = x1
    if with_next:
        xn_ref[...] = xn.astype(xn_ref.dtype)


def _combine_post(grp, ys, pos1, pos2, meta, x, gpost, gt, nxt):
    tm = 256
    imap = lambda i, p1, p2: (i, 0)
    in_specs, args, out_specs, out_shape = _post_specs(grp, tm, x, gpost, gt, nxt, imap)
    out = pl.pallas_call(
        functools.partial(_combine_post_kernel, with_next=nxt is not None),
        grid_spec=pltpu.PrefetchScalarGridSpec(
            num_scalar_prefetch=2,
            grid=(grp.m // tm,),
            in_specs=[pl.BlockSpec(memory_space=pl.ANY), pl.BlockSpec((tm, LANES), imap)] + in_specs,
            out_specs=out_specs,
            scratch_shapes=[pltpu.VMEM((tm, D_MODEL), F32), pltpu.VMEM((tm, D_MODEL), F32),
                            pltpu.SemaphoreType.DMA(())],
        ),
        out_shape=out_shape,
        compiler_params=_params(("arbitrary",), [((tm, D_MODEL), F32)] * 7,
                                [((tm, D_MODEL), F32)] * 2),
        name="moe_combine_post",
    )(pos1, pos2, ys, meta, *args)
    return (out[0], out[1]) if nxt is not None else (out[0], None)


def _moe_post(grp, xn, p, j, tg, x, gpost, gt, nxt):
    n_tiles = 2 * grp.m // tg + N_EXPERTS
    meta, counts = _router(grp, xn, p['router_w'][j], p['router_b'][j])
    pos1, pos2, tile_expert, n_used = _route_plan(meta, counts, tg, n_tiles)
    xs = _dispatch(grp, xn, pos1, pos2, n_tiles * tg)
    ys = _experts(xs, tile_expert, n_used, p['moe_wg_bf'][j], p['moe_wu_bf'][j], p['moe_wd_bf'][j], tg)
    return _combine_post(grp, ys, pos1, pos2, meta, x, gpost, gt, nxt)


def _trunk(grp, x, mods, h0s, conv0s, p, nb, tt, tg):
    depth = p['w_in'].shape[0]
    hs, convs, vs = [], [], []
    sh1, sc1 = mods[0][0], mods[0][1]
    xn = _prenorm(grp, x, p['g_pre_mix'][0], sc1, sh1)
    for l in range(depth):
        _, _, gt1, sh2, sc2, gt2 = mods[l]
        w_in = p['w_in']
        xa = _mm_act(grp, xn, w_in, l, 0, D_MODEL, lambda y: y, F32, "in_proj_xa")
        gel = _mm_act(grp, xn, w_in, l, D_MODEL, 3 * D_MODEL, _gelu, BF16, "in_proj_gelu")
        sig = _mm_act(grp, xn, w_in, l, 4 * D_MODEL, 2 * D_MODEL, _sigmoid, BF16, "in_proj_sigmoid")
        ya, h_last, conv_new = _branch_a(
            grp, nb, tt, xa, gel, conv0s[l], h0s[l], p['conv_w'][l], p['conv_b'][l],
            p['w_r'][l], p['w_i'][l], p['b_r'][l], p['b_i'][l], p['lru_lambda'][l])
        if grp.t >= CHUNK:
            bias_full = jnp.repeat(p['b_s'][l].T, GROUP_B, axis=-1)
            yb = _branch_b(grp, gel, p['g_v'][l], p['w_s'][l], bias_full)
            vn = None
        else:
            yb, vn = _branch_b_short(grp, gel, p['g_v'][l], p['w_s'][l], p['b_s'][l])
        m = _merge(grp, ya, yb, sig, p['w_pa'], p['w_pb'], l)
        dense = l % 2 == 0
        x, xn2 = _oproj_post(grp, m, p['w_o'], l, x, p['g_post_mix'][l], gt1,
                             p['g_pre_ffn'][l], sc2, sh2, BF16 if dense else F32)
        if l + 1 < depth:
            nsh1, nsc1 = mods[l + 1][0], mods[l + 1][1]
            nxt = (p['g_pre_mix'][l + 1], nsc1, nsh1)
        else:
            nxt = None
        j = l // 2
        if dense:
            y = _swiglu(grp, xn2, p['ffn_wg_bf'][j], p['ffn_wu_bf'][j], p['ffn_wd_bf'][j])
            x, xn = _post(grp, y, x, p['g_post_ffn'][l], gt2, nxt)
        else:
            x, xn = _moe_post(grp, xn2, p, j, tg, x, p['g_post_ffn'][l], gt2, nxt)
        hs.append(h_last.reshape(grp.b, D_MODEL))
        convs.append(conv_new)
        vs.append(vn)
    return x, jnp.stack(hs), jnp.stack(convs), vs


def kernel(x_prompt, x_sample, c_prompt, c_sample, state_lru_h, state_lru_conv, w_ada, b_ada, g_pre_mix, g_post_mix, g_pre_ffn, g_post_ffn, w_in, conv_w, conv_b, w_r, b_r, w_i, b_i, lru_lambda, g_v, w_s, b_s, w_pa, w_pb, w_o, ffn_wg, ffn_wu, ffn_wd, router_w, router_b, moe_wg, moe_wu, moe_wd):
    p = dict(g_pre_mix=g_pre_mix, g_post_mix=g_post_mix, g_pre_ffn=g_pre_ffn, g_post_ffn=g_post_ffn,
             w_in=w_in, conv_w=conv_w, conv_b=conv_b, w_r=w_r, b_r=b_r, w_i=w_i, b_i=b_i,
             lru_lambda=lru_lambda, g_v=g_v, w_s=w_s, b_s=b_s, w_pa=w_pa, w_pb=w_pb, w_o=w_o,
             router_w=router_w, router_b=router_b,
             ffn_wg_bf=ffn_wg.astype(BF16), ffn_wu_bf=ffn_wu.astype(BF16), ffn_wd_bf=ffn_wd.astype(BF16),
             moe_wg_bf=moe_wg.astype(BF16), moe_wu_bf=moe_wu.astype(BF16), moe_wd_bf=moe_wd.astype(BF16))
    depth = w_in.shape[0]
    bp, tp, _ = x_prompt.shape
    bs, ts, _ = x_sample.shape
    assert tp % CHUNK == 0 and ts < CHUNK and ts % SUBLANES == 0

    n_c = bp + bs
    r_pad = -n_c % 16
    c_all = jnp.concatenate([c_prompt, c_sample, jnp.zeros((r_pad, D_MODEL), F32)], axis=0)
    mod = _ada(c_all, w_ada, b_ada).reshape(depth, n_c + r_pad, 6, D_MODEL)
    mods_p = [[mod[l, :bp, k].reshape(bp, 1, D_MODEL) for k in range(6)] for l in range(depth)]
    mods_s = [[jnp.repeat(mod[l, bp:n_c, k], ts, axis=0) for k in range(6)] for l in range(depth)]

    grp_p = _Group(bp, tp, per_token_mod=False)
    grp_s = _Group(bs, ts, per_token_mod=True)
    zeros_h = jnp.zeros((depth, bp, D_MODEL), F32)
    zeros_conv = jnp.zeros((depth, bp, CONV_W - 1, D_MODEL), F32)

    y_p, h_p, conv_p, _ = _trunk(grp_p, x_prompt.reshape(bp * tp, D_MODEL), mods_p, zeros_h, zeros_conv,
                                 p, nb=1, tt=256, tg=512)
    y_s, h_s, conv_s, v_s = _trunk(grp_s, x_sample.reshape(bs * ts, D_MODEL), mods_s, state_lru_h,
                                   state_lru_conv, p, nb=16, tt=ts, tg=256)
    return (y_p.reshape(bp, tp, D_MODEL), y_s.reshape(bs, ts, D_MODEL), h_p, conv_p, h_s, conv_s,
            jnp.stack(v_s))
```

```python
import functools

import jax
import jax.numpy as jnp
from jax import lax
from jax.experimental import pallas as pl
from jax.experimental.pallas import tpu as pltpu

F32 = jnp.float32
BF16 = jnp.bfloat16

D_MODEL = 2048
N_HEADS_A = 16
HEAD_A = D_MODEL // N_HEADS_A
CONV_W = 4
C_GATE = 8.0
N_GROUPS_B = 16
GROUP_B = D_MODEL // N_GROUPS_B
CHUNK = 128
N_EXPERTS = 8
EPS = 1e-6

LANES = 128
SUBLANES = 8
VMEM_CAP = 58 * 2**20
CONV_PAD = SUBLANES


def _nbytes(shape, dtype):
    n = 1
    for s in shape:
        n *= s
    return n * jnp.dtype(dtype).itemsize


def _params(sem, blocks, scratch=()):
    need = 2 * sum(_nbytes(s, d) for s, d in blocks) + sum(_nbytes(s, d) for s, d in scratch)
    limit = min(VMEM_CAP, need + 16 * 2**20)
    return pltpu.CompilerParams(dimension_semantics=sem, vmem_limit_bytes=limit)


def _rms(x, g):
    return x * lax.rsqrt(jnp.mean(x * x, axis=-1, keepdims=True) + EPS) * g


def _sigmoid(x):
    return 1.0 / (1.0 + jnp.exp(-x))


def _silu(x):
    return x * _sigmoid(x)


def _gelu(x):
    return jax.nn.gelu(x)


class _Group:
    def __init__(self, b, t, per_token_mod):
        self.b, self.t, self.m = b, t, b * t
        self.per_token_mod = per_token_mod

    def mod_spec(self, tm, nd_grid=1, axis=0):
        if self.per_token_mod:
            def imap(*ids):
                return (ids[axis], 0)
            return pl.BlockSpec((tm, D_MODEL), imap)
        t = self.t

        def imap(*ids):
            return ((ids[axis] * tm) // t, 0, 0)
        return pl.BlockSpec((None, 1, D_MODEL), imap)


def _ada_kernel(c_ref, w_ref, b_ref, o_ref):
    s = _silu(c_ref[...]).astype(BF16)
    o_ref[...] = jnp.dot(s, w_ref[...].astype(BF16), preferred_element_type=F32) + b_ref[...]


def _ada(c, w_ada, b_ada):
    depth, _, n = w_ada.shape
    r = c.shape[0]
    tn = 1024
    return pl.pallas_call(
        _ada_kernel,
        grid=(depth, n // tn),
        in_specs=[
            pl.BlockSpec((r, D_MODEL), lambda l, j: (0, 0)),
            pl.BlockSpec((None, D_MODEL, tn), lambda l, j: (l, 0, j)),
            pl.BlockSpec((None, 1, tn), lambda l, j: (l, 0, j)),
        ],
        out_specs=pl.BlockSpec((None, r, tn), lambda l, j: (l, 0, j)),
        out_shape=jax.ShapeDtypeStruct((depth, r, n), F32),
        compiler_params=_params(("arbitrary", "arbitrary"),
                                [((r, D_MODEL), F32), ((D_MODEL, tn), F32), ((r, tn), F32)],
                                [((D_MODEL, tn), BF16)]),
        name="ada_mod",
    )(c, w_ada, b_ada.reshape(depth, 1, n))


def _prenorm_kernel(x_ref, g_ref, sc_ref, sh_ref, o_ref):
    o_ref[...] = (_rms(x_ref[...], g_ref[...]) * (1.0 + sc_ref[...]) + sh_ref[...]).astype(BF16)


def _prenorm(grp, x, g, sc, sh):
    tm = 512
    return pl.pallas_call(
        _prenorm_kernel,
        grid=(grp.m // tm,),
        in_specs=[
            pl.BlockSpec((tm, D_MODEL), lambda i: (i, 0)),
            pl.BlockSpec((1, D_MODEL), lambda i: (0, 0)),
            grp.mod_spec(tm), grp.mod_spec(tm),
        ],
        out_specs=pl.BlockSpec((tm, D_MODEL), lambda i: (i, 0)),
        out_shape=jax.ShapeDtypeStruct((grp.m, D_MODEL), BF16),
        compiler_params=_params(("arbitrary",), [((tm, D_MODEL), F32)] * 4),
        name="prenorm",
    )(x, g.reshape(1, D_MODEL), sc, sh)


def _mm_act_kernel(x_ref, w_ref, o_ref, wbf_ref, *, act):
    @pl.when(pl.program_id(1) == 0)
    def _():
        wbf_ref[...] = w_ref[...].astype(BF16)

    y = jnp.dot(x_ref[...], wbf_ref[...], preferred_element_type=F32)
    o_ref[...] = act(y).astype(o_ref.dtype)


def _mm_act(grp, x, w, layer, col0, ncols, act, out_dtype, name):
    tm, tn = 1024, 1024
    c0 = col0 // tn
    return pl.pallas_call(
        functools.partial(_mm_act_kernel, act=act),
        grid=(ncols // tn, grp.m // tm),
        in_specs=[
            pl.BlockSpec((tm, D_MODEL), lambda j, i: (i, 0)),
            pl.BlockSpec((None, D_MODEL, tn), lambda j, i: (layer, 0, c0 + j)),
        ],
        out_specs=pl.BlockSpec((tm, tn), lambda j, i: (i, j)),
        out_shape=jax.ShapeDtypeStruct((grp.m, ncols), out_dtype),
        scratch_shapes=[pltpu.VMEM((D_MODEL, tn), BF16)],
        compiler_params=_params(("arbitrary", "arbitrary"),
                                [((tm, D_MODEL), BF16), ((D_MODEL, tn), F32), ((tm, tn), F32)],
                                [((D_MODEL, tn), BF16), ((tm, tn), F32)]),
        name=name,
    )(x, w)


def _branch_a_kernel(xa_ref, ga_ref, conv0_ref, h0_ref, cw_ref, cb_ref, wr_ref, wi_ref,
                     br_ref, bi_ref, lam_ref, ya_ref, hlast_ref, convnew_ref, xp_ref, h_ref,
                     *, nb, tt, n_t):
    t_idx = pl.program_id(1)
    hist = CONV_W - 1
    lo = CONV_PAD - hist

    @pl.when(t_idx == 0)
    def _():
        xp_ref[:, lo:CONV_PAD, :] = conv0_ref[...]
        h_ref[...] = h0_ref[...]

    xp_ref[:, CONV_PAD:CONV_PAD + tt, :] = xa_ref[...]
    rows = nb * tt
    n_blk = tt // SUBLANES
    blk_shape = (nb * n_blk, SUBLANES, HEAD_A)
    t_in_blk = lax.broadcasted_iota(jnp.int32, blk_shape, 1)

    for hd in range(N_HEADS_A):
        sl = slice(hd * HEAD_A, (hd + 1) * HEAD_A)
        xc = cb_ref[:, sl][None]
        for k in range(CONV_W):
            xc = xc + cw_ref[k:k + 1, sl][None] * xp_ref[:, lo + k:lo + k + tt, sl]
        xc = xc.reshape(rows, HEAD_A)
        xcb = xc.astype(BF16)
        r = _sigmoid(jnp.dot(xcb, wr_ref[hd].astype(BF16), preferred_element_type=F32) + br_ref[:, sl])
        i = _sigmoid(jnp.dot(xcb, wi_ref[hd].astype(BF16), preferred_element_type=F32) + bi_ref[:, sl])
        z = -lam_ref[:, sl]
        softplus = jnp.maximum(z, 0.0) + jnp.log1p(jnp.exp(-jnp.abs(z)))
        log_a = (-C_GATE * r) * softplus
        a = jnp.exp(log_a)
        th = jnp.tanh(log_a)
        one_m_a2 = -2.0 * th / (1.0 - th)
        b = jnp.where(one_m_a2 > 0.0, one_m_a2 * lax.rsqrt(one_m_a2), 0.0) * (i * xc)
        a = a.reshape(blk_shape)
        b = b.reshape(blk_shape)
        d = 1
        while d < SUBLANES:
            a_sh = pltpu.roll(a, d, axis=1)
            b_sh = pltpu.roll(b, d, axis=1)
            keep = t_in_blk >= d
            b = jnp.where(keep, a * b_sh + b, b)
            a = jnp.where(keep, a * a_sh, a)
            d *= 2
        a3 = a.reshape(nb, tt, HEAD_A)
        b3 = b.reshape(nb, tt, HEAD_A)
        carry = h_ref[:, :, sl]
        h_blocks = []
        for k in range(n_blk):
            ks = slice(k * SUBLANES, (k + 1) * SUBLANES)
            h_k = a3[:, ks, :] * carry + b3[:, ks, :]
            carry = h_k[:, SUBLANES - 1:SUBLANES, :]
            h_blocks.append(h_k)
        h_ref[:, :, sl] = carry
        h = jnp.concatenate(h_blocks, axis=1) if n_blk > 1 else h_blocks[0]
        ya_ref[:, sl] = (ga_ref[:, sl].astype(F32) * h.reshape(rows, HEAD_A)).astype(BF16)

    xp_ref[:, lo:CONV_PAD, :] = xp_ref[:, lo + tt:CONV_PAD + tt, :]

    @pl.when(t_idx == n_t - 1)
    def _():
        hlast_ref[...] = h_ref[...]
        convnew_ref[...] = xp_ref[:, lo:CONV_PAD, :]


def _branch_a(grp, nb, tt, xa, gel, conv0, h0, cw, cb, wr, wi, br, bi, lam):
    b, t = grp.b, grp.t
    n_t = t // tt
    rows = nb * tt
    hist = CONV_W - 1
    vec = lambda: pl.BlockSpec((1, D_MODEL), lambda bi_, ti: (0, 0))
    head_w = lambda: pl.BlockSpec((N_HEADS_A, HEAD_A, HEAD_A), lambda bi_, ti: (0, 0, 0))
    return pl.pallas_call(
        functools.partial(_branch_a_kernel, nb=nb, tt=tt, n_t=n_t),
        grid=(b // nb, n_t),
        in_specs=[
            pl.BlockSpec((nb, tt, D_MODEL), lambda bi_, ti: (bi_, ti, 0)),
            pl.BlockSpec((rows, D_MODEL), lambda bi_, ti: (bi_ * n_t + ti, 0)),
            pl.BlockSpec((nb, hist, D_MODEL), lambda bi_, ti: (bi_, 0, 0)),
            pl.BlockSpec((nb, 1, D_MODEL), lambda bi_, ti: (bi_, 0, 0)),
            pl.BlockSpec((CONV_W, D_MODEL), lambda bi_, ti: (0, 0)),
            vec(), head_w(), head_w(), vec(), vec(), vec(),
        ],
        out_specs=[
            pl.BlockSpec((rows, D_MODEL), lambda bi_, ti: (bi_ * n_t + ti, 0)),
            pl.BlockSpec((nb, 1, D_MODEL), lambda bi_, ti: (bi_, 0, 0)),
            pl.BlockSpec((nb, hist, D_MODEL), lambda bi_, ti: (bi_, 0, 0)),
        ],
        out_shape=[
            jax.ShapeDtypeStruct((grp.m, D_MODEL), BF16),
            jax.ShapeDtypeStruct((b, 1, D_MODEL), F32),
            jax.ShapeDtypeStruct((b, hist, D_MODEL), F32),
        ],
        scratch_shapes=[
            pltpu.VMEM((nb, CONV_PAD + tt, D_MODEL), F32),
            pltpu.VMEM((nb, 1, D_MODEL), F32),
        ],
        compiler_params=_params(("arbitrary", "arbitrary"),
                                [((rows, D_MODEL), F32), ((rows, D_MODEL), BF16), ((rows, D_MODEL), BF16),
                                 ((nb, 8, D_MODEL), F32), ((nb, 8, D_MODEL), F32),
                                 ((2 * N_HEADS_A, HEAD_A, HEAD_A), F32)],
                                [((nb, CONV_PAD + tt, D_MODEL), F32), ((nb, 8, D_MODEL), F32)]),
        name="branch_a",
    )(xa.reshape(b, t, D_MODEL), gel, conv0, h0.reshape(b, 1, D_MODEL), cw, cb.reshape(1, D_MODEL),
      wr, wi, br.reshape(1, D_MODEL), bi.reshape(1, D_MODEL), lam.reshape(1, D_MODEL))


def _branch_b_kernel(u_ref, v_ref, gv_ref, ws_ref, bias_ref, yb_ref, wm_ref, *, n_chunks):
    @pl.when(pl.program_id(0) == 0)
    def _():
        tri = (lax.broadcasted_iota(jnp.int32, (CHUNK, CHUNK), 0)
               >= lax.broadcasted_iota(jnp.int32, (CHUNK, CHUNK), 1))
        for g in range(N_GROUPS_B):
            wm_ref[g] = jnp.where(tri, ws_ref[g], 0.0).astype(BF16)

    vn = _rms(v_ref[...].astype(F32), gv_ref[...]).astype(BF16)
    for c in range(n_chunks):
        rs = slice(c * CHUNK, (c + 1) * CHUNK)
        for g in range(N_GROUPS_B):
            cs = slice(g * GROUP_B, (g + 1) * GROUP_B)
            mixed = jnp.dot(wm_ref[g], vn[rs, cs], preferred_element_type=F32) + bias_ref[:, cs]
            yb_ref[rs, cs] = (u_ref[rs, cs].astype(F32) * mixed).astype(BF16)


def _branch_b(grp, gel, gv, ws, bias_full):
    tm = 256
    return pl.pallas_call(
        functools.partial(_branch_b_kernel, n_chunks=tm // CHUNK),
        grid=(grp.m // tm,),
        in_specs=[
            pl.BlockSpec((tm, D_MODEL), lambda i: (i, 1)),
            pl.BlockSpec((tm, D_MODEL), lambda i: (i, 2)),
            pl.BlockSpec((1, D_MODEL), lambda i: (0, 0)),
            pl.BlockSpec((N_GROUPS_B, CHUNK, CHUNK), lambda i: (0, 0, 0)),
            pl.BlockSpec((CHUNK, D_MODEL), lambda i: (0, 0)),
        ],
        out_specs=pl.BlockSpec((tm, D_MODEL), lambda i: (i, 0)),
        out_shape=jax.ShapeDtypeStruct((grp.m, D_MODEL), BF16),
        scratch_shapes=[pltpu.VMEM((N_GROUPS_B, CHUNK, CHUNK), BF16)],
        compiler_params=_params(("arbitrary",),
                                [((tm, D_MODEL), BF16)] * 3 + [((N_GROUPS_B, CHUNK, CHUNK), F32),
                                                                ((CHUNK, D_MODEL), F32)],
                                [((tm, D_MODEL), F32)] * 2),
        name="branch_b",
    )(gel, gel, gv.reshape(1, D_MODEL), ws, bias_full)


def _branch_b_short_kernel(u_ref, v_ref, gv_ref, wc_ref, bias_ref, yb_ref, vn_ref, *, nb, t):
    vn = _rms(v_ref[...].astype(F32), gv_ref[...]).reshape(nb, t, D_MODEL)
    vn_ref[...] = vn
    mixed = bias_ref[...][None]
    for s in range(t):
        mixed = mixed + wc_ref[s][None] * vn[:, s:s + 1, :]
    yb = u_ref[...].astype(F32).reshape(nb, t, D_MODEL) * mixed
    yb_ref[...] = yb.reshape(nb * t, D_MODEL).astype(BF16)


def _branch_b_short(grp, gel, gv, ws, bs):
    b, t = grp.b, grp.t
    nb = 16
    rows = nb * t
    tri = jnp.tril(jnp.ones((t, t), F32))
    wc = jnp.repeat(jnp.transpose(ws[:, :t, :t] * tri, (2, 1, 0)), GROUP_B, axis=-1)
    bias = jnp.repeat(bs[:, :t].T, GROUP_B, axis=-1)
    return pl.pallas_call(
        functools.partial(_branch_b_short_kernel, nb=nb, t=t),
        grid=(b // nb,),
        in_specs=[
            pl.BlockSpec((rows, D_MODEL), lambda i: (i, 1)),
            pl.BlockSpec((rows, D_MODEL), lambda i: (i, 2)),
            pl.BlockSpec((1, D_MODEL), lambda i: (0, 0)),
            pl.BlockSpec((t, t, D_MODEL), lambda i: (0, 0, 0)),
            pl.BlockSpec((t, D_MODEL), lambda i: (0, 0)),
        ],
        out_specs=[
            pl.BlockSpec((rows, D_MODEL), lambda i: (i, 0)),
            pl.BlockSpec((nb, t, D_MODEL), lambda i: (i, 0, 0)),
        ],
        out_shape=[
            jax.ShapeDtypeStruct((grp.m, D_MODEL), BF16),
            jax.ShapeDtypeStruct((b, t, D_MODEL), F32),
        ],
        compiler_params=_params(("arbitrary",),
                                [((rows, D_MODEL), BF16)] * 3 + [((rows, D_MODEL), F32),
                                                                  ((t, t, D_MODEL), F32)],
                                [((rows, D_MODEL), F32)] * 3),
        name="branch_b_short",
    )(gel, gel, gv.reshape(1, D_MODEL), wc, bias)


def _merge_kernel(ya_ref, yb_ref, sa_ref, sb_ref, wpa_ref, wpb_ref, o_ref, wa_ref, wb_ref):
    @pl.when(pl.program_id(1) == 0)
    def _():
        wa_ref[...] = wpa_ref[...].astype(BF16)
        wb_ref[...] = wpb_ref[...].astype(BF16)

    pa = jnp.dot(ya_ref[...], wa_ref[...], preferred_element_type=F32)
    pb = jnp.dot(yb_ref[...], wb_ref[...], preferred_element_type=F32)
    o_ref[...] = (sa_ref[...].astype(F32) * pa + sb_ref[...].astype(F32) * pb).astype(BF16)


def _merge(grp, ya, yb, sig, wpa, wpb, layer):
    tm, tn = 1024, 512
    nj = D_MODEL // tn
    return pl.pallas_call(
        _merge_kernel,
        grid=(nj, grp.m // tm),
        in_specs=[
            pl.BlockSpec((tm, D_MODEL), lambda j, i: (i, 0)),
            pl.BlockSpec((tm, D_MODEL), lambda j, i: (i, 0)),
            pl.BlockSpec((tm, tn), lambda j, i: (i, j)),
            pl.BlockSpec((tm, tn), lambda j, i: (i, nj + j)),
            pl.BlockSpec((None, D_MODEL, tn), lambda j, i: (layer, 0, j)),
            pl.BlockSpec((None, D_MODEL, tn), lambda j, i: (layer, 0, j)),
        ],
        out_specs=pl.BlockSpec((tm, tn), lambda j, i: (i, j)),
        out_shape=jax.ShapeDtypeStruct((grp.m, D_MODEL), BF16),
        scratch_shapes=[pltpu.VMEM((D_MODEL, tn), BF16), pltpu.VMEM((D_MODEL, tn), BF16)],
        compiler_params=_params(("arbitrary", "arbitrary"),
                                [((tm, D_MODEL), BF16)] * 2 + [((tm, tn), BF16)] * 3
                                + [((D_MODEL, tn), F32)] * 2,
                                [((D_MODEL, tn), BF16)] * 2 + [((tm, tn), F32)] * 2),
        name="merge",
    )(ya, yb, sig, sig, wpa, wpb)


def _post_math(y, x, gpost, gt, nxt):
    x1 = x + gt * _rms(y, gpost)
    if nxt is None:
        return x1, None
    gpre, sc, sh = nxt
    return x1, _rms(x1, gpre) * (1.0 + sc) + sh


def _oproj_post_kernel(m_ref, wo_ref, x_ref, gpost_ref, gt_ref, gpre_ref, sc_ref, sh_ref,
                       xo_ref, xn_ref):
    y = jnp.dot(m_ref[...], wo_ref[...], preferred_element_type=F32)
    x1, xn = _post_math(y, x_ref[...], gpost_ref[...], gt_ref[...],
                        (gpre_ref[...], sc_ref[...], sh_ref[...]))
    xo_ref[...] = x1
    xn_ref[...] = xn.astype(xn_ref.dtype)


def _oproj_post(grp, m, wo, layer, x, gpost, gt, gpre, sc, sh, xn_dtype):
    tm = 256 if grp.per_token_mod else 512
    vec = lambda: pl.BlockSpec((1, D_MODEL), lambda i: (0, 0))
    row = lambda: pl.BlockSpec((tm, D_MODEL), lambda i: (i, 0))
    return pl.pallas_call(
        _oproj_post_kernel,
        grid=(grp.m // tm,),
        in_specs=[
            row(),
            pl.BlockSpec((None, D_MODEL, D_MODEL), lambda i: (layer, 0, 0),
                         pipeline_mode=pl.Buffered(1)),
            row(), vec(), grp.mod_spec(tm), vec(), grp.mod_spec(tm), grp.mod_spec(tm),
        ],
        out_specs=[row(), row()],
        out_shape=[
            jax.ShapeDtypeStruct((grp.m, D_MODEL), F32),
            jax.ShapeDtypeStruct((grp.m, D_MODEL), xn_dtype),
        ],
        compiler_params=_params(("arbitrary",),
                                [((tm, D_MODEL), F32)] * (8 if grp.per_token_mod else 5),
                                [((D_MODEL, D_MODEL), BF16), ((tm, D_MODEL), F32)]),
        name="oproj_post",
    )(m, wo, x, gpost.reshape(1, D_MODEL), gt, gpre.reshape(1, D_MODEL), sc, sh)


def _post_kernel(y_ref, x_ref, gpost_ref, gt_ref, *rest, with_next):
    if with_next:
        gpre_ref, sc_ref, sh_ref, xo_ref, xn_ref = rest
        nxt = (gpre_ref[...], sc_ref[...], sh_ref[...])
    else:
        (xo_ref,) = rest
        nxt = None
    x1, xn = _post_math(y_ref[...], x_ref[...], gpost_ref[...], gt_ref[...], nxt)
    xo_ref[...] = x1
    if with_next:
        xn_ref[...] = xn.astype(xn_ref.dtype)


def _post_specs(grp, tm, x, gpost, gt, nxt, imap):
    vec = lambda: pl.BlockSpec((1, D_MODEL), lambda *ids: (0, 0))
    row = lambda: pl.BlockSpec((tm, D_MODEL), imap)
    in_specs = [row(), vec(), grp.mod_spec(tm)]
    args = [x, gpost.reshape(1, D_MODEL), gt]
    out_specs = [row()]
    out_shape = [jax.ShapeDtypeStruct((grp.m, D_MODEL), F32)]
    if nxt is not None:
        gpre, sc, sh = nxt
        in_specs += [vec(), grp.mod_spec(tm), grp.mod_spec(tm)]
        args += [gpre.reshape(1, D_MODEL), sc, sh]
        out_specs.append(row())
        out_shape.append(jax.ShapeDtypeStruct((grp.m, D_MODEL), BF16))
    return in_specs, args, out_specs, out_shape


def _post(grp, y, x, gpost, gt, nxt):
    tm = 256
    imap = lambda i: (i, 0)
    in_specs, args, out_specs, out_shape = _post_specs(grp, tm, x, gpost, gt, nxt, imap)
    out = pl.pallas_call(
        functools.partial(_post_kernel, with_next=nxt is not None),
        grid=(grp.m // tm,),
        in_specs=[pl.BlockSpec((tm, D_MODEL), imap)] + in_specs,
        out_specs=out_specs,
        out_shape=out_shape,
        compiler_params=_params(("arbitrary",), [((tm, D_MODEL), F32)] * 8),
        name="post",
    )(y, *args)
    return (out[0], out[1]) if nxt is not None else (out[0], None)


def _swiglu_tile(x, wg_ref, wu_ref, wd_ref):
    g = jnp.dot(x, wg_ref[...], preferred_element_type=F32)
    u = jnp.dot(x, wu_ref[...], preferred_element_type=F32)
    h = _silu(g) * u
    return jnp.dot(h.astype(BF16), wd_ref[...], preferred_element_type=F32)


def _swiglu_kernel(x_ref, wg_ref, wu_ref, wd_ref, o_ref):
    @pl.when(pl.program_id(1) == 0)
    def _():
        o_ref[...] = jnp.zeros_like(o_ref)

    o_ref[...] += _swiglu_tile(x_ref[...], wg_ref, wu_ref, wd_ref)


def _swiglu(grp, x, wg, wu, wd):
    f = wg.shape[-1]
    tm, tf = 1024, 512
    return pl.pallas_call(
        _swiglu_kernel,
        grid=(grp.m // tm, f // tf),
        in_specs=[
            pl.BlockSpec((tm, D_MODEL), lambda i, j: (i, 0)),
            pl.BlockSpec((D_MODEL, tf), lambda i, j: (0, j)),
            pl.BlockSpec((D_MODEL, tf), lambda i, j: (0, j)),
            pl.BlockSpec((tf, D_MODEL), lambda i, j: (j, 0)),
        ],
        out_specs=pl.BlockSpec((tm, D_MODEL), lambda i, j: (i, 0)),
        out_shape=jax.ShapeDtypeStruct((grp.m, D_MODEL), F32),
        compiler_params=_params(("arbitrary", "arbitrary"),
                                [((tm, D_MODEL), BF16), ((tm, D_MODEL), F32)]
                                + [((D_MODEL, tf), BF16)] * 3,
                                [((tm, tf), F32)] * 4),
        name="swiglu_dense",
    )(x, wg, wu, wd)


META_E1, META_E2, META_R1, META_R2, META_W1, META_W2 = range(6)
DMA_UNROLL = 8


def _split_bf16(v):
    hi = v.astype(BF16)
    return hi, (v - hi.astype(F32)).astype(BF16)


def _router_kernel(x_ref, rw_ref, rb_ref, meta_ref, cnt_ref, tri_ref):
    tm = x_ref.shape[0]

    @pl.when(pl.program_id(0) == 0)
    def _():
        cnt_ref[...] = jnp.zeros_like(cnt_ref)
        tri_ref[...] = (lax.broadcasted_iota(jnp.int32, (tm, tm), 0)
                        >= lax.broadcasted_iota(jnp.int32, (tm, tm), 1)).astype(BF16)

    xh, xl = _split_bf16(x_ref[...])
    wh, wl = _split_bf16(rw_ref[...])
    logits = (jnp.dot(xh, wh, preferred_element_type=F32) + jnp.dot(xl, wh, preferred_element_type=F32)
              + jnp.dot(xh, wl, preferred_element_type=F32)) + rb_ref[...]
    lane = lax.broadcasted_iota(jnp.int32, logits.shape, 1)
    logits = jnp.where(lane < N_EXPERTS, logits, -jnp.inf)
    ex = jnp.exp(logits - jnp.max(logits, axis=-1, keepdims=True))
    p = ex / jnp.sum(ex, axis=-1, keepdims=True)
    p1 = jnp.max(p, axis=-1, keepdims=True)
    i1 = jnp.min(jnp.where(p == p1, lane, LANES), axis=-1, keepdims=True)
    rest = jnp.where(lane == i1, -1.0, p)
    p2 = jnp.max(rest, axis=-1, keepdims=True)
    i2 = jnp.min(jnp.where(rest == p2, lane, LANES), axis=-1, keepdims=True)
    denom = p1 + p2
    onehot = jnp.where(lane == i1, 1.0, 0.0) + jnp.where(lane == i2, 1.0, 0.0)
    cum = jnp.dot(tri_ref[...], onehot.astype(BF16), preferred_element_type=F32) + cnt_ref[...]
    cnt_ref[...] = cum[tm - 1:tm, :]
    r1 = jnp.sum(jnp.where(lane == i1, cum, 0.0), axis=-1, keepdims=True) - 1.0
    r2 = jnp.sum(jnp.where(lane == i2, cum, 0.0), axis=-1, keepdims=True) - 1.0
    meta = jnp.zeros_like(logits)
    for k, v in ((META_E1, i1.astype(F32)), (META_E2, i2.astype(F32)), (META_R1, r1), (META_R2, r2),
                 (META_W1, p1 / denom), (META_W2, p2 / denom)):
        meta = jnp.where(lane == k, v, meta)
    meta_ref[...] = meta


def _router(grp, x, rw, rb):
    tm = 512
    rw_pad = jnp.pad(rw, ((0, 0), (0, LANES - N_EXPERTS)))
    rb_pad = jnp.pad(rb, (0, LANES - N_EXPERTS)).reshape(1, LANES)
    return pl.pallas_call(
        _router_kernel,
        grid=(grp.m // tm,),
        in_specs=[
            pl.BlockSpec((tm, D_MODEL), lambda i: (i, 0)),
            pl.BlockSpec((D_MODEL, LANES), lambda i: (0, 0)),
            pl.BlockSpec((1, LANES), lambda i: (0, 0)),
        ],
        out_specs=[
            pl.BlockSpec((tm, LANES), lambda i: (i, 0)),
            pl.BlockSpec((1, LANES), lambda i: (0, 0)),
        ],
        out_shape=[
            jax.ShapeDtypeStruct((grp.m, LANES), F32),
            jax.ShapeDtypeStruct((1, LANES), F32),
        ],
        scratch_shapes=[pltpu.VMEM((tm, tm), BF16)],
        compiler_params=_params(("arbitrary",),
                                [((tm, D_MODEL), F32), ((D_MODEL, LANES), F32), ((tm, LANES), F32)],
                                [((tm, tm), BF16), ((tm, D_MODEL), F32)]),
        name="router",
    )(x, rw_pad, rb_pad)


def _route_plan(meta, counts, tg, n_tiles):
    col = lambda k: meta[:, k].astype(jnp.int32)
    cnt = counts[0, :N_EXPERTS].astype(jnp.int32)
    padded = (cnt + tg - 1) // tg * tg
    ends = jnp.cumsum(padded)
    offs = ends - padded
    pos1 = offs[col(META_E1)] + col(META_R1)
    pos2 = offs[col(META_E2)] + col(META_R2)
    n_used = ends[-1] // tg
    tile = jnp.arange(n_tiles, dtype=jnp.int32)
    tile_c = jnp.minimum(tile, n_used - 1)
    tile_expert = jnp.sum(tile_c[:, None] * tg >= ends[None, :], axis=1).astype(jnp.int32)
    return pos1, pos2, tile_expert, n_used.reshape(1).astype(jnp.int32)


def _dispatch_kernel(pos1_ref, pos2_ref, x_ref, xs_in_ref, xs_ref, sem):
    del xs_in_ref
    tm = x_ref.shape[0]
    base = pl.program_id(0) * tm

    def row_copy(t, slot):
        return pltpu.make_async_copy(x_ref.at[pl.ds(t, 1)], xs_ref.at[pl.ds(slot, 1)], sem)

    def start(t, c):
        row_copy(t, pos1_ref[base + t]).start()
        row_copy(t, pos2_ref[base + t]).start()
        return c

    def wait(t, c):
        row_copy(t, 0).wait()
        row_copy(t, 0).wait()
        return c

    lax.fori_loop(0, tm, start, 0, unroll=DMA_UNROLL)
    lax.fori_loop(0, tm, wait, 0, unroll=DMA_UNROLL)


def _dispatch(grp, x, pos1, pos2, n_rows):
    tm = 256
    return pl.pallas_call(
        _dispatch_kernel,
        grid_spec=pltpu.PrefetchScalarGridSpec(
            num_scalar_prefetch=2,
            grid=(grp.m // tm,),
            in_specs=[
                pl.BlockSpec((tm, D_MODEL), lambda i, p1, p2: (i, 0)),
                pl.BlockSpec(memory_space=pl.ANY),
            ],
            out_specs=pl.BlockSpec(memory_space=pl.ANY),
            scratch_shapes=[pltpu.SemaphoreType.DMA(())],
        ),
        out_shape=jax.ShapeDtypeStruct((n_rows, D_MODEL), F32),
        input_output_aliases={3: 0},
        compiler_params=_params(("arbitrary",), [((tm, D_MODEL), F32)]),
        name="moe_dispatch",
    )(pos1, pos2, x, jnp.zeros((n_rows, D_MODEL), F32))


def _experts_kernel(te_ref, nu_ref, xs_ref, wg_ref, wu_ref, wd_ref, o_ref):
    del te_ref
    r, j = pl.program_id(0), pl.program_id(1)

    @pl.when(j == 0)
    def _():
        o_ref[...] = jnp.zeros_like(o_ref)

    @pl.when(r < nu_ref[0])
    def _():
        o_ref[...] += _swiglu_tile(xs_ref[...].astype(BF16), wg_ref, wu_ref, wd_ref)


def _experts(xs, tile_expert, n_used, wg, wu, wd, tg):
    n_rows = xs.shape[0]
    f = wg.shape[-1]
    tf = 512
    n_j = f // tf

    def used(r, nu):
        return r < nu[0]

    def x_map(r, j, te, nu):
        return (jnp.minimum(r, nu[0] - 1), 0)

    def up_map(r, j, te, nu):
        return (te[r], 0, jnp.where(used(r, nu), j, n_j - 1))

    def down_map(r, j, te, nu):
        return (te[r], jnp.where(used(r, nu), j, n_j - 1), 0)

    return pl.pallas_call(
        _experts_kernel,
        grid_spec=pltpu.PrefetchScalarGridSpec(
            num_scalar_prefetch=2,
            grid=(n_rows // tg, n_j),
            in_specs=[
                pl.BlockSpec((tg, D_MODEL), x_map),
                pl.BlockSpec((None, D_MODEL, tf), up_map),
                pl.BlockSpec((None, D_MODEL, tf), up_map),
                pl.BlockSpec((None, tf, D_MODEL), down_map),
            ],
            out_specs=pl.BlockSpec((tg, D_MODEL), lambda r, j, te, nu: (r, 0)),
        ),
        out_shape=jax.ShapeDtypeStruct((n_rows, D_MODEL), F32),
        compiler_params=_params(("arbitrary", "arbitrary"),
                                [((tg, D_MODEL), F32)] * 2 + [((D_MODEL, tf), BF16)] * 3,
                                [((tg, tf), F32)] * 4 + [((tg, D_MODEL), BF16)]),
        name="moe_experts",
    )(tile_expert, n_used, xs, wg, wu, wd)


def _combine_post_kernel(pos1_ref, pos2_ref, ys_ref, meta_ref, x_ref, gpost_ref, gt_ref, *rest,
                         with_next):
    *rest, buf1_ref, buf2_ref, sem = rest
    tm = x_ref.shape[0]
    step, n_steps = pl.program_id(0), pl.num_programs(0)
    cur = step % 2

    def row_copy(row, buf_ref, half, t):
        return pltpu.make_async_copy(ys_ref.at[pl.ds(row, 1)], buf_ref.at[half, pl.ds(t, 1)],
                                     sem.at[half])

    def gather(tile, half):
        def start(t, c):
            row_copy(pos1_ref[tile * tm + t], buf1_ref, half, t).start()
            row_copy(pos2_ref[tile * tm + t], buf2_ref, half, t).start()
            return c
        lax.fori_loop(0, tm, start, 0, unroll=DMA_UNROLL)

    @pl.when(step == 0)
    def _():
        gather(0, 0)

    @pl.when(step + 1 < n_steps)
    def _():
        gather(step + 1, 1 - cur)

    def wait(t, c):
        row_copy(0, buf1_ref, cur, t).wait()
        row_copy(0, buf2_ref, cur, t).wait()
        return c

    lax.fori_loop(0, tm, wait, 0, unroll=DMA_UNROLL)
    meta = meta_ref[...]
    y = meta[:, META_W1:META_W1 + 1] * buf1_ref[cur] + meta[:, META_W2:META_W2 + 1] * buf2_ref[cur]
    if with_next:
        gpre_ref, sc_ref, sh_ref, xo_ref, xn_ref = rest
        nxt = (gpre_ref[...], sc_ref[...], sh_ref[...])
    else:
        (xo_ref,) = rest
        nxt = None
    x1, xn = _post_math(y, x_ref[...], gpost_ref[...], gt_ref[...], nxt)
    xo_ref[...] = x1
    if with_next:
        xn_ref[...] = xn.astype(xn_ref.dtype)


def _combine_post(grp, ys, pos1, pos2, meta, x, gpost, gt, nxt):
    tm = 256
    imap = lambda i, p1, p2: (i, 0)
    in_specs, args, out_specs, out_shape = _post_specs(grp, tm, x, gpost, gt, nxt, imap)
    out = pl.pallas_call(
        functools.partial(_combine_post_kernel, with_next=nxt is not None),
        grid_spec=pltpu.PrefetchScalarGridSpec(
            num_scalar_prefetch=2,
            grid=(grp.m // tm,),
            in_specs=[pl.BlockSpec(memory_space=pl.ANY), pl.BlockSpec((tm, LANES), imap)] + in_specs,
            out_specs=out_specs,
            scratch_shapes=[pltpu.VMEM((2, tm, D_MODEL), F32), pltpu.VMEM((2, tm, D_MODEL), F32),
                            pltpu.SemaphoreType.DMA((2,))],
        ),
        out_shape=out_shape,
        compiler_params=_params(("arbitrary",), [((tm, D_MODEL), F32)] * 7,
                                [((tm, D_MODEL), F32)] * 4),
        name="moe_combine_post",
    )(pos1, pos2, ys, meta, *args)
    return (out[0], out[1]) if nxt is not None else (out[0], None)


def _moe_post(grp, xn, p, j, tg, x, gpost, gt, nxt):
    n_tiles = 2 * grp.m // tg + N_EXPERTS
    meta, counts = _router(grp, xn, p['router_w'][j], p['router_b'][j])
    pos1, pos2, tile_expert, n_used = _route_plan(meta, counts, tg, n_tiles)
    xs = _dispatch(grp, xn, pos1, pos2, n_tiles * tg)
    ys = _experts(xs, tile_expert, n_used, p['moe_wg_bf'][j], p['moe_wu_bf'][j], p['moe_wd_bf'][j], tg)
    return _combine_post(grp, ys, pos1, pos2, meta, x, gpost, gt, nxt)


def _trunk(grp, x, mods, h0s, conv0s, p, nb, tt, tg):
    depth = p['w_in'].shape[0]
    hs, convs, vs = [], [], []
    sh1, sc1 = mods[0][0], mods[0][1]
    xn = _prenorm(grp, x, p['g_pre_mix'][0], sc1, sh1)
    for l in range(depth):
        _, _, gt1, sh2, sc2, gt2 = mods[l]
        w_in = p['w_in']
        xa = _mm_act(grp, xn, w_in, l, 0, D_MODEL, lambda y: y, F32, "in_proj_xa")
        gel = _mm_act(grp, xn, w_in, l, D_MODEL, 3 * D_MODEL, _gelu, BF16, "in_proj_gelu")
        sig = _mm_act(grp, xn, w_in, l, 4 * D_MODEL, 2 * D_MODEL, _sigmoid, BF16, "in_proj_sigmoid")
        ya, h_last, conv_new = _branch_a(
            grp, nb, tt, xa, gel, conv0s[l], h0s[l], p['conv_w'][l], p['conv_b'][l],
            p['w_r'][l], p['w_i'][l], p['b_r'][l], p['b_i'][l], p['lru_lambda'][l])
        if grp.t >= CHUNK:
            bias_full = jnp.repeat(p['b_s'][l].T, GROUP_B, axis=-1)
            yb = _branch_b(grp, gel, p['g_v'][l], p['w_s'][l], bias_full)
            vn = None
        else:
            yb, vn = _branch_b_short(grp, gel, p['g_v'][l], p['w_s'][l], p['b_s'][l])
        m = _merge(grp, ya, yb, sig, p['w_pa'], p['w_pb'], l)
        dense = l % 2 == 0
        x, xn2 = _oproj_post(grp, m, p['w_o_bf'], l, x, p['g_post_mix'][l], gt1,
                             p['g_pre_ffn'][l], sc2, sh2, BF16 if dense else F32)
        if l + 1 < depth:
            nsh1, nsc1 = mods[l + 1][0], mods[l + 1][1]
            nxt = (p['g_pre_mix'][l + 1], nsc1, nsh1)
        else:
            nxt = None
        j = l // 2
        if dense:
            y = _swiglu(grp, xn2, p['ffn_wg_bf'][j], p['ffn_wu_bf'][j], p['ffn_wd_bf'][j])
            x, xn = _post(grp, y, x, p['g_post_ffn'][l], gt2, nxt)
        else:
            x, xn = _moe_post(grp, xn2, p, j, tg, x, p['g_post_ffn'][l], gt2, nxt)
        hs.append(h_last.reshape(grp.b, D_MODEL))
        convs.append(conv_new)
        vs.append(vn)
    return x, jnp.stack(hs), jnp.stack(convs), vs


def kernel(x_prompt, x_sample, c_prompt, c_sample, state_lru_h, state_lru_conv, w_ada, b_ada, g_pre_mix, g_post_mix, g_pre_ffn, g_post_ffn, w_in, conv_w, conv_b, w_r, b_r, w_i, b_i, lru_lambda, g_v, w_s, b_s, w_pa, w_pb, w_o, ffn_wg, ffn_wu, ffn_wd, router_w, router_b, moe_wg, moe_wu, moe_wd):
    p = dict(g_pre_mix=g_pre_mix, g_post_mix=g_post_mix, g_pre_ffn=g_pre_ffn, g_post_ffn=g_post_ffn,
             w_in=w_in, conv_w=conv_w, conv_b=conv_b, w_r=w_r, b_r=b_r, w_i=w_i, b_i=b_i,
             lru_lambda=lru_lambda, g_v=g_v, w_s=w_s, b_s=b_s, w_pa=w_pa, w_pb=w_pb,
             router_w=router_w, router_b=router_b, w_o_bf=w_o.astype(BF16),
             ffn_wg_bf=ffn_wg.astype(BF16), ffn_wu_bf=ffn_wu.astype(BF16), ffn_wd_bf=ffn_wd.astype(BF16),
             moe_wg_bf=moe_wg.astype(BF16), moe_wu_bf=moe_wu.astype(BF16), moe_wd_bf=moe_wd.astype(BF16))
    depth = w_in.shape[0]
    bp, tp, _ = x_prompt.shape
    bs, ts, _ = x_sample.shape
    assert tp % CHUNK == 0 and ts < CHUNK and ts % SUBLANES == 0

    n_c = bp + bs
    r_pad = -n_c % 16
    c_all = jnp.concatenate([c_prompt, c_sample, jnp.zeros((r_pad, D_MODEL), F32)], axis=0)
    mod = _ada(c_all, w_ada, b_ada).reshape(depth, n_c + r_pad, 6, D_MODEL)
    mods_p = [[mod[l, :bp, k].reshape(bp, 1, D_MODEL) for k in range(6)] for l in range(depth)]
    mods_s = [[jnp.repeat(mod[l, bp:n_c, k], ts, axis=0) for k in range(6)] for l in range(depth)]

    grp_p = _Group(bp, tp, per_token_mod=False)
    grp_s = _Group(bs, ts, per_token_mod=True)
    zeros_h = jnp.zeros((depth, bp, D_MODEL), F32)
    zeros_conv = jnp.zeros((depth, bp, CONV_W - 1, D_MODEL), F32)

    y_p, h_p, conv_p, _ = _trunk(grp_p, x_prompt.reshape(bp * tp, D_MODEL), mods_p, zeros_h, zeros_conv,
                                 p, nb=1, tt=256, tg=512)
    y_s, h_s, conv_s, v_s = _trunk(grp_s, x_sample.reshape(bs * ts, D_MODEL), mods_s, state_lru_h,
                                   state_lru_conv, p, nb=16, tt=ts, tg=256)
    return (y_p.reshape(bp, tp, D_MODEL), y_s.reshape(bs, ts, D_MODEL), h_p, conv_p, h_s, conv_s,
            jnp.stack(v_s))
```

```python
import functools

import jax
import jax.numpy as jnp
from jax import lax
from jax.experimental import pallas as pl
from jax.experimental.pallas import tpu as pltpu

F32 = jnp.float32
BF16 = jnp.bfloat16

D_MODEL = 2048
N_HEADS_A = 16
HEAD_A = D_MODEL // N_HEADS_A
CONV_W = 4
C_GATE = 8.0
N_GROUPS_B = 16
GROUP_B = D_MODEL // N_GROUPS_B
CHUNK = 128
N_EXPERTS = 8
EPS = 1e-6

LANES = 128
SUBLANES = 8
VMEM_CAP = 58 * 2**20
CONV_PAD = SUBLANES


def _nbytes(shape, dtype):
    n = 1
    for s in shape:
        n *= s
    return n * jnp.dtype(dtype).itemsize


def _params(sem, blocks, scratch=()):
    need = 2 * sum(_nbytes(s, d) for s, d in blocks) + sum(_nbytes(s, d) for s, d in scratch)
    limit = min(VMEM_CAP, need + 16 * 2**20)
    return pltpu.CompilerParams(dimension_semantics=sem, vmem_limit_bytes=limit)


def _rms(x, g):
    return x * lax.rsqrt(jnp.mean(x * x, axis=-1, keepdims=True) + EPS) * g


def _sigmoid(x):
    return 1.0 / (1.0 + jnp.exp(-x))


def _silu(x):
    return x * _sigmoid(x)


def _gelu(x):
    return jax.nn.gelu(x)


class _Group:
    def __init__(self, b, t, short):
        self.b, self.t, self.m = b, t, b * t
        self.short = short

    def mod_spec(self, tm):
        if self.short:
            return pl.BlockSpec((tm // self.t, D_MODEL), lambda *ids: (ids[0], 0))
        t = self.t
        return pl.BlockSpec((None, 1, D_MODEL), lambda *ids: ((ids[0] * tm) // t, 0, 0))


def _mod_rows(ref, rows, offset=None):
    n = ref.shape[0]
    vals = ref[...] if offset is None else ref[...] + offset
    if n == 1:
        return vals
    rep = rows // n
    return jnp.concatenate([jnp.broadcast_to(vals[r:r + 1, :], (rep, D_MODEL)) for r in range(n)], axis=0)


def _ada_kernel(c_ref, w_ref, b_ref, o_ref):
    s = _silu(c_ref[...]).astype(BF16)
    o_ref[...] = jnp.dot(s, w_ref[...].astype(BF16), preferred_element_type=F32) + b_ref[...]


def _ada(c, w_ada, b_ada):
    depth, _, n = w_ada.shape
    r = c.shape[0]
    tn = 1024
    return pl.pallas_call(
        _ada_kernel,
        grid=(depth, n // tn),
        in_specs=[
            pl.BlockSpec((r, D_MODEL), lambda l, j: (0, 0)),
            pl.BlockSpec((None, D_MODEL, tn), lambda l, j: (l, 0, j)),
            pl.BlockSpec((None, 1, tn), lambda l, j: (l, 0, j)),
        ],
        out_specs=pl.BlockSpec((None, r, tn), lambda l, j: (l, 0, j)),
        out_shape=jax.ShapeDtypeStruct((depth, r, n), F32),
        compiler_params=_params(("arbitrary", "arbitrary"),
                                [((r, D_MODEL), F32), ((D_MODEL, tn), F32), ((r, tn), F32)],
                                [((D_MODEL, tn), BF16)]),
        name="ada_mod",
    )(c, w_ada, b_ada.reshape(depth, 1, n))


def _prenorm_kernel(x_ref, g_ref, sc_ref, sh_ref, o_ref):
    rows = x_ref.shape[0]
    o_ref[...] = (_rms(x_ref[...], g_ref[...]) * _mod_rows(sc_ref, rows, 1.0)
                  + _mod_rows(sh_ref, rows)).astype(BF16)


def _prenorm(grp, x, g, sc, sh):
    tm = 256 if grp.short else 512
    return pl.pallas_call(
        _prenorm_kernel,
        grid=(grp.m // tm,),
        in_specs=[
            pl.BlockSpec((tm, D_MODEL), lambda i: (i, 0)),
            pl.BlockSpec((1, D_MODEL), lambda i: (0, 0)),
            grp.mod_spec(tm), grp.mod_spec(tm),
        ],
        out_specs=pl.BlockSpec((tm, D_MODEL), lambda i: (i, 0)),
        out_shape=jax.ShapeDtypeStruct((grp.m, D_MODEL), BF16),
        compiler_params=_params(("arbitrary",), [((tm, D_MODEL), F32)] * 4),
        name="prenorm",
    )(x, g.reshape(1, D_MODEL), sc, sh)


def _mm_act_kernel(x_ref, w_ref, o_ref, wbf_ref, *, act):
    @pl.when(pl.program_id(1) == 0)
    def _():
        wbf_ref[...] = w_ref[...].astype(BF16)

    y = jnp.dot(x_ref[...], wbf_ref[...], preferred_element_type=F32)
    o_ref[...] = act(y).astype(o_ref.dtype)


def _mm_act(grp, x, w, layer, col0, ncols, act, out_dtype, name):
    tm, tn = 1024, 1024
    c0 = col0 // tn
    return pl.pallas_call(
        functools.partial(_mm_act_kernel, act=act),
        grid=(ncols // tn, grp.m // tm),
        in_specs=[
            pl.BlockSpec((tm, D_MODEL), lambda j, i: (i, 0)),
            pl.BlockSpec((None, D_MODEL, tn), lambda j, i: (layer, 0, c0 + j)),
        ],
        out_specs=pl.BlockSpec((tm, tn), lambda j, i: (i, j)),
        out_shape=jax.ShapeDtypeStruct((grp.m, ncols), out_dtype),
        scratch_shapes=[pltpu.VMEM((D_MODEL, tn), BF16)],
        compiler_params=_params(("arbitrary", "arbitrary"),
                                [((tm, D_MODEL), BF16), ((D_MODEL, tn), F32), ((tm, tn), F32)],
                                [((D_MODEL, tn), BF16), ((tm, tn), F32)]),
        name=name,
    )(x, w)


def _branch_a_kernel(xa_ref, ga_ref, conv0_ref, h0_ref, cw_ref, cb_ref, wr_ref, wi_ref,
                     br_ref, bi_ref, lam_ref, ya_ref, hlast_ref, convnew_ref, xp_ref, h_ref,
                     *, nb, tt, n_t):
    t_idx = pl.program_id(1)
    hist = CONV_W - 1
    lo = CONV_PAD - hist

    @pl.when(t_idx == 0)
    def _():
        xp_ref[:, lo:CONV_PAD, :] = conv0_ref[...]
        h_ref[...] = h0_ref[...]

    xp_ref[:, CONV_PAD:CONV_PAD + tt, :] = xa_ref[...]
    rows = nb * tt
    n_blk = tt // SUBLANES
    blk_shape = (nb * n_blk, SUBLANES, HEAD_A)
    t_in_blk = lax.broadcasted_iota(jnp.int32, blk_shape, 1)

    for hd in range(N_HEADS_A):
        sl = slice(hd * HEAD_A, (hd + 1) * HEAD_A)
        xc = cb_ref[:, sl][None]
        for k in range(CONV_W):
            xc = xc + cw_ref[k:k + 1, sl][None] * xp_ref[:, lo + k:lo + k + tt, sl]
        xc = xc.reshape(rows, HEAD_A)
        xcb = xc.astype(BF16)
        r = _sigmoid(jnp.dot(xcb, wr_ref[hd].astype(BF16), preferred_element_type=F32) + br_ref[:, sl])
        i = _sigmoid(jnp.dot(xcb, wi_ref[hd].astype(BF16), preferred_element_type=F32) + bi_ref[:, sl])
        z = -lam_ref[:, sl]
        softplus = jnp.maximum(z, 0.0) + jnp.log1p(jnp.exp(-jnp.abs(z)))
        log_a = (-C_GATE * r) * softplus
        a = jnp.exp(log_a)
        th = jnp.tanh(log_a)
        one_m_a2 = -2.0 * th / (1.0 - th)
        b = jnp.where(one_m_a2 > 0.0, one_m_a2 * lax.rsqrt(one_m_a2), 0.0) * (i * xc)
        a = a.reshape(blk_shape)
        b = b.reshape(blk_shape)
        d = 1
        while d < SUBLANES:
            a_sh = pltpu.roll(a, d, axis=1)
            b_sh = pltpu.roll(b, d, axis=1)
            keep = t_in_blk >= d
            b = jnp.where(keep, a * b_sh + b, b)
            a = jnp.where(keep, a * a_sh, a)
            d *= 2
        a3 = a.reshape(nb, tt, HEAD_A)
        b3 = b.reshape(nb, tt, HEAD_A)
        carry = h_ref[:, :, sl]
        h_blocks = []
        for k in range(n_blk):
            ks = slice(k * SUBLANES, (k + 1) * SUBLANES)
            h_k = a3[:, ks, :] * carry + b3[:, ks, :]
            carry = h_k[:, SUBLANES - 1:SUBLANES, :]
            h_blocks.append(h_k)
        h_ref[:, :, sl] = carry
        h = jnp.concatenate(h_blocks, axis=1) if n_blk > 1 else h_blocks[0]
        ya_ref[:, sl] = (ga_ref[:, sl].astype(F32) * h.reshape(rows, HEAD_A)).astype(BF16)

    xp_ref[:, lo:CONV_PAD, :] = xp_ref[:, lo + tt:CONV_PAD + tt, :]

    @pl.when(t_idx == n_t - 1)
    def _():
        hlast_ref[...] = h_ref[...]
        convnew_ref[...] = xp_ref[:, lo:CONV_PAD, :]


def _branch_a(grp, nb, tt, xa, gel, conv0, h0, cw, cb, wr, wi, br, bi, lam):
    b, t = grp.b, grp.t
    n_t = t // tt
    rows = nb * tt
    hist = CONV_W - 1
    vec = lambda: pl.BlockSpec((1, D_MODEL), lambda bi_, ti: (0, 0))
    head_w = lambda: pl.BlockSpec((N_HEADS_A, HEAD_A, HEAD_A), lambda bi_, ti: (0, 0, 0))
    return pl.pallas_call(
        functools.partial(_branch_a_kernel, nb=nb, tt=tt, n_t=n_t),
        grid=(b // nb, n_t),
        in_specs=[
            pl.BlockSpec((nb, tt, D_MODEL), lambda bi_, ti: (bi_, ti, 0)),
            pl.BlockSpec((rows, D_MODEL), lambda bi_, ti: (bi_ * n_t + ti, 0)),
            pl.BlockSpec((nb, hist, D_MODEL), lambda bi_, ti: (bi_, 0, 0)),
            pl.BlockSpec((nb, 1, D_MODEL), lambda bi_, ti: (bi_, 0, 0)),
            pl.BlockSpec((CONV_W, D_MODEL), lambda bi_, ti: (0, 0)),
            vec(), head_w(), head_w(), vec(), vec(), vec(),
        ],
        out_specs=[
            pl.BlockSpec((rows, D_MODEL), lambda bi_, ti: (bi_ * n_t + ti, 0)),
            pl.BlockSpec((nb, 1, D_MODEL), lambda bi_, ti: (bi_, 0, 0)),
            pl.BlockSpec((nb, hist, D_MODEL), lambda bi_, ti: (bi_, 0, 0)),
        ],
        out_shape=[
            jax.ShapeDtypeStruct((grp.m, D_MODEL), BF16),
            jax.ShapeDtypeStruct((b, 1, D_MODEL), F32),
            jax.ShapeDtypeStruct((b, hist, D_MODEL), F32),
        ],
        scratch_shapes=[
            pltpu.VMEM((nb, CONV_PAD + tt, D_MODEL), F32),
            pltpu.VMEM((nb, 1, D_MODEL), F32),
        ],
        compiler_params=_params(("arbitrary", "arbitrary"),
                                [((rows, D_MODEL), F32), ((rows, D_MODEL), BF16), ((rows, D_MODEL), BF16),
                                 ((nb, 8, D_MODEL), F32), ((nb, 8, D_MODEL), F32),
                                 ((2 * N_HEADS_A, HEAD_A, HEAD_A), F32)],
                                [((nb, CONV_PAD + tt, D_MODEL), F32), ((nb, 8, D_MODEL), F32)]),
        name="branch_a",
    )(xa.reshape(b, t, D_MODEL), gel, conv0, h0.reshape(b, 1, D_MODEL), cw, cb.reshape(1, D_MODEL),
      wr, wi, br.reshape(1, D_MODEL), bi.reshape(1, D_MODEL), lam.reshape(1, D_MODEL))


def _branch_b_kernel(u_ref, v_ref, gv_ref, ws_ref, bias_ref, yb_ref, wm_ref, *, n_chunks):
    @pl.when(pl.program_id(0) == 0)
    def _():
        tri = (lax.broadcasted_iota(jnp.int32, (CHUNK, CHUNK), 0)
               >= lax.broadcasted_iota(jnp.int32, (CHUNK, CHUNK), 1))
        for g in range(N_GROUPS_B):
            wm_ref[g] = jnp.where(tri, ws_ref[g], 0.0).astype(BF16)

    vn = _rms(v_ref[...].astype(F32), gv_ref[...]).astype(BF16)
    for g in range(N_GROUPS_B):
        cs = slice(g * GROUP_B, (g + 1) * GROUP_B)
        rhs = jnp.concatenate([vn[c * CHUNK:(c + 1) * CHUNK, cs] for c in range(n_chunks)], axis=1)
        mixed = jnp.dot(wm_ref[g], rhs, preferred_element_type=F32)
        for c in range(n_chunks):
            rs = slice(c * CHUNK, (c + 1) * CHUNK)
            mixed_c = mixed[:, c * GROUP_B:(c + 1) * GROUP_B] + bias_ref[:, cs]
            yb_ref[rs, cs] = (u_ref[rs, cs].astype(F32) * mixed_c).astype(BF16)


def _branch_b(grp, gel, gv, ws, bias_full):
    tm = 512
    return pl.pallas_call(
        functools.partial(_branch_b_kernel, n_chunks=tm // CHUNK),
        grid=(grp.m // tm,),
        in_specs=[
            pl.BlockSpec((tm, D_MODEL), lambda i: (i, 1)),
            pl.BlockSpec((tm, D_MODEL), lambda i: (i, 2)),
            pl.BlockSpec((1, D_MODEL), lambda i: (0, 0)),
            pl.BlockSpec((N_GROUPS_B, CHUNK, CHUNK), lambda i: (0, 0, 0)),
            pl.BlockSpec((CHUNK, D_MODEL), lambda i: (0, 0)),
        ],
        out_specs=pl.BlockSpec((tm, D_MODEL), lambda i: (i, 0)),
        out_shape=jax.ShapeDtypeStruct((grp.m, D_MODEL), BF16),
        scratch_shapes=[pltpu.VMEM((N_GROUPS_B, CHUNK, CHUNK), BF16)],
        compiler_params=_params(("arbitrary",),
                                [((tm, D_MODEL), BF16)] * 3 + [((N_GROUPS_B, CHUNK, CHUNK), F32),
                                                                ((CHUNK, D_MODEL), F32)],
                                [((tm, D_MODEL), F32)] * 2),
        name="branch_b",
    )(gel, gel, gv.reshape(1, D_MODEL), ws, bias_full)


def _branch_b_short_kernel(u_ref, v_ref, gv_ref, wc_ref, bias_ref, yb_ref, vn_ref, *, nb, t):
    vn = _rms(v_ref[...].astype(F32), gv_ref[...]).reshape(nb, t, D_MODEL)
    vn_ref[...] = vn
    mixed = bias_ref[...][None]
    for s in range(t):
        mixed = mixed + wc_ref[s][None] * vn[:, s:s + 1, :]
    yb = u_ref[...].astype(F32).reshape(nb, t, D_MODEL) * mixed
    yb_ref[...] = yb.reshape(nb * t, D_MODEL).astype(BF16)


def _branch_b_short(grp, gel, gv, ws, bs):
    b, t = grp.b, grp.t
    nb = 16
    rows = nb * t
    tri = jnp.tril(jnp.ones((t, t), F32))
    wc = jnp.repeat(jnp.transpose(ws[:, :t, :t] * tri, (2, 1, 0)), GROUP_B, axis=-1)
    bias = jnp.repeat(bs[:, :t].T, GROUP_B, axis=-1)
    return pl.pallas_call(
        functools.partial(_branch_b_short_kernel, nb=nb, t=t),
        grid=(b // nb,),
        in_specs=[
            pl.BlockSpec((rows, D_MODEL), lambda i: (i, 1)),
            pl.BlockSpec((rows, D_MODEL), lambda i: (i, 2)),
            pl.BlockSpec((1, D_MODEL), lambda i: (0, 0)),
            pl.BlockSpec((t, t, D_MODEL), lambda i: (0, 0, 0)),
            pl.BlockSpec((t, D_MODEL), lambda i: (0, 0)),
        ],
        out_specs=[
            pl.BlockSpec((rows, D_MODEL), lambda i: (i, 0)),
            pl.BlockSpec((nb, t, D_MODEL), lambda i: (i, 0, 0)),
        ],
        out_shape=[
            jax.ShapeDtypeStruct((grp.m, D_MODEL), BF16),
            jax.ShapeDtypeStruct((b, t, D_MODEL), F32),
        ],
        compiler_params=_params(("arbitrary",),
                                [((rows, D_MODEL), BF16)] * 3 + [((rows, D_MODEL), F32),
                                                                  ((t, t, D_MODEL), F32)],
                                [((rows, D_MODEL), F32)] * 3),
        name="branch_b_short",
    )(gel, gel, gv.reshape(1, D_MODEL), wc, bias)


def _merge_kernel(ya_ref, yb_ref, sa_ref, sb_ref, wpa_ref, wpb_ref, o_ref, wa_ref, wb_ref):
    @pl.when(pl.program_id(1) == 0)
    def _():
        wa_ref[...] = wpa_ref[...].astype(BF16)
        wb_ref[...] = wpb_ref[...].astype(BF16)

    pa = jnp.dot(ya_ref[...], wa_ref[...], preferred_element_type=F32)
    pb = jnp.dot(yb_ref[...], wb_ref[...], preferred_element_type=F32)
    o_ref[...] = (sa_ref[...].astype(F32) * pa + sb_ref[...].astype(F32) * pb).astype(BF16)


def _merge(grp, ya, yb, sig, wpa, wpb, layer):
    tm, tn = 1024, 512
    nj = D_MODEL // tn
    return pl.pallas_call(
        _merge_kernel,
        grid=(nj, grp.m // tm),
        in_specs=[
            pl.BlockSpec((tm, D_MODEL), lambda j, i: (i, 0)),
            pl.BlockSpec((tm, D_MODEL), lambda j, i: (i, 0)),
            pl.BlockSpec((tm, tn), lambda j, i: (i, j)),
            pl.BlockSpec((tm, tn), lambda j, i: (i, nj + j)),
            pl.BlockSpec((None, D_MODEL, tn), lambda j, i: (layer, 0, j)),
            pl.BlockSpec((None, D_MODEL, tn), lambda j, i: (layer, 0, j)),
        ],
        out_specs=pl.BlockSpec((tm, tn), lambda j, i: (i, j)),
        out_shape=jax.ShapeDtypeStruct((grp.m, D_MODEL), BF16),
        scratch_shapes=[pltpu.VMEM((D_MODEL, tn), BF16), pltpu.VMEM((D_MODEL, tn), BF16)],
        compiler_params=_params(("arbitrary", "arbitrary"),
                                [((tm, D_MODEL), BF16)] * 2 + [((tm, tn), BF16)] * 3
                                + [((D_MODEL, tn), F32)] * 2,
                                [((D_MODEL, tn), BF16)] * 2 + [((tm, tn), F32)] * 2),
        name="merge",
    )(ya, yb, sig, sig, wpa, wpb)


def _post_math(y, x, gpost, gt, nxt):
    x1 = x + gt * _rms(y, gpost)
    if nxt is None:
        return x1, None
    gpre, one_plus_sc, sh = nxt
    return x1, _rms(x1, gpre) * one_plus_sc + sh


def _next_mods(gpre_ref, sc_ref, sh_ref, rows):
    return gpre_ref[...], _mod_rows(sc_ref, rows, 1.0), _mod_rows(sh_ref, rows)


def _oproj_post_kernel(m_ref, wo_ref, x_ref, gpost_ref, gt_ref, gpre_ref, sc_ref, sh_ref,
                       xo_ref, xn_ref):
    y = jnp.dot(m_ref[...], wo_ref[...], preferred_element_type=F32)
    rows = x_ref.shape[0]
    x1, xn = _post_math(y, x_ref[...], gpost_ref[...], _mod_rows(gt_ref, rows),
                        _next_mods(gpre_ref, sc_ref, sh_ref, rows))
    xo_ref[...] = x1
    xn_ref[...] = xn.astype(xn_ref.dtype)


def _oproj_post(grp, m, wo, layer, x, gpost, gt, gpre, sc, sh, xn_dtype):
    tm = 512
    vec = lambda: pl.BlockSpec((1, D_MODEL), lambda i: (0, 0))
    row = lambda: pl.BlockSpec((tm, D_MODEL), lambda i: (i, 0))
    return pl.pallas_call(
        _oproj_post_kernel,
        grid=(grp.m // tm,),
        in_specs=[
            row(),
            pl.BlockSpec((None, D_MODEL, D_MODEL), lambda i: (layer, 0, 0),
                         pipeline_mode=pl.Buffered(1)),
            row(), vec(), grp.mod_spec(tm), vec(), grp.mod_spec(tm), grp.mod_spec(tm),
        ],
        out_specs=[row(), row()],
        out_shape=[
            jax.ShapeDtypeStruct((grp.m, D_MODEL), F32),
            jax.ShapeDtypeStruct((grp.m, D_MODEL), xn_dtype),
        ],
        compiler_params=_params(("arbitrary",),
                                [((tm, D_MODEL), F32)] * 5,
                                [((D_MODEL, D_MODEL), BF16), ((tm, D_MODEL), F32)]),
        name="oproj_post",
    )(m, wo, x, gpost.reshape(1, D_MODEL), gt, gpre.reshape(1, D_MODEL), sc, sh)


def _post_kernel(y_ref, x_ref, gpost_ref, gt_ref, *rest, with_next):
    rows = x_ref.shape[0]
    if with_next:
        gpre_ref, sc_ref, sh_ref, xo_ref, xn_ref = rest
        nxt = _next_mods(gpre_ref, sc_ref, sh_ref, rows)
    else:
        (xo_ref,) = rest
        nxt = None
    x1, xn = _post_math(y_ref[...], x_ref[...], gpost_ref[...], _mod_rows(gt_ref, rows), nxt)
    xo_ref[...] = x1
    if with_next:
        xn_ref[...] = xn.astype(xn_ref.dtype)


def _post_specs(grp, tm, x, gpost, gt, nxt, imap):
    vec = lambda: pl.BlockSpec((1, D_MODEL), lambda *ids: (0, 0))
    row = lambda: pl.BlockSpec((tm, D_MODEL), imap)
    in_specs = [row(), vec(), grp.mod_spec(tm)]
    args = [x, gpost.reshape(1, D_MODEL), gt]
    out_specs = [row()]
    out_shape = [jax.ShapeDtypeStruct((grp.m, D_MODEL), F32)]
    if nxt is not None:
        gpre, sc, sh = nxt
        in_specs += [vec(), grp.mod_spec(tm), grp.mod_spec(tm)]
        args += [gpre.reshape(1, D_MODEL), sc, sh]
        out_specs.append(row())
        out_shape.append(jax.ShapeDtypeStruct((grp.m, D_MODEL), BF16))
    return in_specs, args, out_specs, out_shape


def _post(grp, y, x, gpost, gt, nxt):
    tm = 256
    imap = lambda i: (i, 0)
    in_specs, args, out_specs, out_shape = _post_specs(grp, tm, x, gpost, gt, nxt, imap)
    out = pl.pallas_call(
        functools.partial(_post_kernel, with_next=nxt is not None),
        grid=(grp.m // tm,),
        in_specs=[pl.BlockSpec((tm, D_MODEL), imap)] + in_specs,
        out_specs=out_specs,
        out_shape=out_shape,
        compiler_params=_params(("arbitrary",), [((tm, D_MODEL), F32)] * 8),
        name="post",
    )(y, *args)
    return (out[0], out[1]) if nxt is not None else (out[0], None)


def _swiglu_tile(x, wg_ref, wu_ref, wd_ref):
    g = jnp.dot(x, wg_ref[...], preferred_element_type=F32)
    u = jnp.dot(x, wu_ref[...], preferred_element_type=F32)
    h = _silu(g) * u
    return jnp.dot(h.astype(BF16), wd_ref[...], preferred_element_type=F32)


def _swiglu_kernel(x_ref, wg_ref, wu_ref, wd_ref, o_ref):
    @pl.when(pl.program_id(1) == 0)
    def _():
        o_ref[...] = jnp.zeros_like(o_ref)

    o_ref[...] += _swiglu_tile(x_ref[...], wg_ref, wu_ref, wd_ref)


def _swiglu(grp, x, wg, wu, wd):
    f = wg.shape[-1]
    tm, tf = 1024, 512
    return pl.pallas_call(
        _swiglu_kernel,
        grid=(grp.m // tm, f // tf),
        in_specs=[
            pl.BlockSpec((tm, D_MODEL), lambda i, j: (i, 0)),
            pl.BlockSpec((D_MODEL, tf), lambda i, j: (0, j)),
            pl.BlockSpec((D_MODEL, tf), lambda i, j: (0, j)),
            pl.BlockSpec((tf, D_MODEL), lambda i, j: (j, 0)),
        ],
        out_specs=pl.BlockSpec((tm, D_MODEL), lambda i, j: (i, 0)),
        out_shape=jax.ShapeDtypeStruct((grp.m, D_MODEL), F32),
        compiler_params=_params(("arbitrary", "arbitrary"),
                                [((tm, D_MODEL), BF16), ((tm, D_MODEL), F32)]
                                + [((D_MODEL, tf), BF16)] * 3,
                                [((tm, tf), F32)] * 4),
        name="swiglu_dense",
    )(x, wg, wu, wd)


META_E1, META_E2, META_R1, META_R2, META_W1, META_W2 = range(6)
DMA_UNROLL = 8


def _split_bf16(v):
    hi = v.astype(BF16)
    return hi, (v - hi.astype(F32)).astype(BF16)


def _router_kernel(x_ref, rw_ref, rb_ref, meta_ref, cnt_ref, tri_ref):
    tm = x_ref.shape[0]

    @pl.when(pl.program_id(0) == 0)
    def _():
        cnt_ref[...] = jnp.zeros_like(cnt_ref)
        tri_ref[...] = (lax.broadcasted_iota(jnp.int32, (tm, tm), 0)
                        >= lax.broadcasted_iota(jnp.int32, (tm, tm), 1)).astype(BF16)

    xh, xl = _split_bf16(x_ref[...])
    wh, wl = _split_bf16(rw_ref[...])
    logits = (jnp.dot(xh, wh, preferred_element_type=F32) + jnp.dot(xl, wh, preferred_element_type=F32)
              + jnp.dot(xh, wl, preferred_element_type=F32)) + rb_ref[...]
    lane = lax.broadcasted_iota(jnp.int32, logits.shape, 1)
    logits = jnp.where(lane < N_EXPERTS, logits, -jnp.inf)
    ex = jnp.exp(logits - jnp.max(logits, axis=-1, keepdims=True))
    p = ex / jnp.sum(ex, axis=-1, keepdims=True)
    p1 = jnp.max(p, axis=-1, keepdims=True)
    i1 = jnp.min(jnp.where(p == p1, lane, LANES), axis=-1, keepdims=True)
    rest = jnp.where(lane == i1, -1.0, p)
    p2 = jnp.max(rest, axis=-1, keepdims=True)
    i2 = jnp.min(jnp.where(rest == p2, lane, LANES), axis=-1, keepdims=True)
    denom = p1 + p2
    onehot = jnp.where(lane == i1, 1.0, 0.0) + jnp.where(lane == i2, 1.0, 0.0)
    cum = jnp.dot(tri_ref[...], onehot.astype(BF16), preferred_element_type=F32) + cnt_ref[...]
    cnt_ref[...] = cum[tm - 1:tm, :]
    r1 = jnp.sum(jnp.where(lane == i1, cum, 0.0), axis=-1, keepdims=True) - 1.0
    r2 = jnp.sum(jnp.where(lane == i2, cum, 0.0), axis=-1, keepdims=True) - 1.0
    meta = jnp.zeros_like(logits)
    for k, v in ((META_E1, i1.astype(F32)), (META_E2, i2.astype(F32)), (META_R1, r1), (META_R2, r2),
                 (META_W1, p1 / denom), (META_W2, p2 / denom)):
        meta = jnp.where(lane == k, v, meta)
    meta_ref[...] = meta


def _router(grp, x, rw, rb):
    tm = 512
    rw_pad = jnp.pad(rw, ((0, 0), (0, LANES - N_EXPERTS)))
    rb_pad = jnp.pad(rb, (0, LANES - N_EXPERTS)).reshape(1, LANES)
    return pl.pallas_call(
        _router_kernel,
        grid=(grp.m // tm,),
        in_specs=[
            pl.BlockSpec((tm, D_MODEL), lambda i: (i, 0)),
            pl.BlockSpec((D_MODEL, LANES), lambda i: (0, 0)),
            pl.BlockSpec((1, LANES), lambda i: (0, 0)),
        ],
        out_specs=[
            pl.BlockSpec((tm, LANES), lambda i: (i, 0)),
            pl.BlockSpec((1, LANES), lambda i: (0, 0)),
        ],
        out_shape=[
            jax.ShapeDtypeStruct((grp.m, LANES), F32),
            jax.ShapeDtypeStruct((1, LANES), F32),
        ],
        scratch_shapes=[pltpu.VMEM((tm, tm), BF16)],
        compiler_params=_params(("arbitrary",),
                                [((tm, D_MODEL), F32), ((D_MODEL, LANES), F32), ((tm, LANES), F32)],
                                [((tm, tm), BF16), ((tm, D_MODEL), F32)]),
        name="router",
    )(x, rw_pad, rb_pad)


def _route_plan(metas, counts, tg, n_tiles):
    cnts = [c[0, :N_EXPERTS].astype(jnp.int32) for c in counts]
    total = sum(cnts)
    padded = (total + tg - 1) // tg * tg
    ends = jnp.cumsum(padded)
    offs = ends - padded
    slots = []
    for meta, before in zip(metas, [sum(cnts[:g], jnp.zeros_like(total)) for g in range(len(cnts))]):
        idx = meta[:, :META_W1].astype(jnp.int32)
        start = offs + before
        slots.append((start[idx[:, META_E1]] + idx[:, META_R1], start[idx[:, META_E2]] + idx[:, META_R2]))
    n_used = ends[-1] // tg
    tile = jnp.arange(n_tiles, dtype=jnp.int32)
    tile_c = jnp.minimum(tile, n_used - 1)
    tile_expert = jnp.sum(tile_c[:, None] * tg >= ends[None, :], axis=1).astype(jnp.int32)
    return slots, tile_expert, n_used.reshape(1).astype(jnp.int32)


def _dispatch_kernel(pos1_ref, pos2_ref, x_ref, xs_in_ref, xs_ref, sem):
    del xs_in_ref
    tm = x_ref.shape[0]
    base = pl.program_id(0) * tm

    def row_copy(t, slot):
        return pltpu.make_async_copy(x_ref.at[pl.ds(t, 1)], xs_ref.at[pl.ds(slot, 1)], sem)

    def start(t, c):
        row_copy(t, pos1_ref[base + t]).start()
        row_copy(t, pos2_ref[base + t]).start()
        return c

    def wait(t, c):
        row_copy(t, 0).wait()
        row_copy(t, 0).wait()
        return c

    lax.fori_loop(0, tm, start, 0, unroll=DMA_UNROLL)
    lax.fori_loop(0, tm, wait, 0, unroll=DMA_UNROLL)


def _dispatch(grp, x, pos1, pos2, xs):
    tm = 256
    return pl.pallas_call(
        _dispatch_kernel,
        grid_spec=pltpu.PrefetchScalarGridSpec(
            num_scalar_prefetch=2,
            grid=(grp.m // tm,),
            in_specs=[
                pl.BlockSpec((tm, D_MODEL), lambda i, p1, p2: (i, 0)),
                pl.BlockSpec(memory_space=pl.ANY),
            ],
            out_specs=pl.BlockSpec(memory_space=pl.ANY),
            scratch_shapes=[pltpu.SemaphoreType.DMA(())],
        ),
        out_shape=jax.ShapeDtypeStruct(xs.shape, F32),
        input_output_aliases={3: 0},
        compiler_params=_params(("arbitrary",), [((tm, D_MODEL), F32)]),
        name="moe_dispatch",
    )(pos1, pos2, x, xs)


def _experts_kernel(te_ref, nu_ref, xs_ref, wg_ref, wu_ref, wd_ref, o_ref):
    del te_ref
    r, j = pl.program_id(0), pl.program_id(1)

    @pl.when(j == 0)
    def _():
        o_ref[...] = jnp.zeros_like(o_ref)

    @pl.when(r < nu_ref[0])
    def _():
        o_ref[...] += _swiglu_tile(xs_ref[...].astype(BF16), wg_ref, wu_ref, wd_ref)


def _experts(xs, tile_expert, n_used, wg, wu, wd, tg):
    n_rows = xs.shape[0]
    f = wg.shape[-1]
    tf = 512
    n_j = f // tf

    def used(r, nu):
        return r < nu[0]

    def x_map(r, j, te, nu):
        return (jnp.minimum(r, nu[0] - 1), 0)

    def up_map(r, j, te, nu):
        return (te[r], 0, jnp.where(used(r, nu), j, n_j - 1))

    def down_map(r, j, te, nu):
        return (te[r], jnp.where(used(r, nu), j, n_j - 1), 0)

    return pl.pallas_call(
        _experts_kernel,
        grid_spec=pltpu.PrefetchScalarGridSpec(
            num_scalar_prefetch=2,
            grid=(n_rows // tg, n_j),
            in_specs=[
                pl.BlockSpec((tg, D_MODEL), x_map),
                pl.BlockSpec((None, D_MODEL, tf), up_map),
                pl.BlockSpec((None, D_MODEL, tf), up_map),
                pl.BlockSpec((None, tf, D_MODEL), down_map),
            ],
            out_specs=pl.BlockSpec((tg, D_MODEL), lambda r, j, te, nu: (r, 0)),
        ),
        out_shape=jax.ShapeDtypeStruct((n_rows, D_MODEL), F32),
        compiler_params=_params(("arbitrary", "arbitrary"),
                                [((tg, D_MODEL), F32)] * 2 + [((D_MODEL, tf), BF16)] * 3,
                                [((tg, tf), F32)] * 4 + [((tg, D_MODEL), BF16)]),
        name="moe_experts",
    )(tile_expert, n_used, xs, wg, wu, wd)


def _combine_post_kernel(pos1_ref, pos2_ref, ys_ref, meta_ref, x_ref, gpost_ref, gt_ref, *rest,
                         with_next):
    *rest, buf1_ref, buf2_ref, sem = rest
    tm = x_ref.shape[0]
    step, n_steps = pl.program_id(0), pl.num_programs(0)
    cur = step % 2

    def row_copy(row, buf_ref, half, t):
        return pltpu.make_async_copy(ys_ref.at[pl.ds(row, 1)], buf_ref.at[half, pl.ds(t, 1)],
                                     sem.at[half])

    def gather(tile, half):
        def start(t, c):
            row_copy(pos1_ref[tile * tm + t], buf1_ref, half, t).start()
            row_copy(pos2_ref[tile * tm + t], buf2_ref, half, t).start()
            return c
        lax.fori_loop(0, tm, start, 0, unroll=DMA_UNROLL)

    @pl.when(step == 0)
    def _():
        gather(0, 0)

    @pl.when(step + 1 < n_steps)
    def _():
        gather(step + 1, 1 - cur)

    def wait(t, c):
        row_copy(0, buf1_ref, cur, t).wait()
        row_copy(0, buf2_ref, cur, t).wait()
        return c

    lax.fori_loop(0, tm, wait, 0, unroll=DMA_UNROLL)
    meta = meta_ref[...]
    y = meta[:, META_W1:META_W1 + 1] * buf1_ref[cur] + meta[:, META_W2:META_W2 + 1] * buf2_ref[cur]
    if with_next:
        gpre_ref, sc_ref, sh_ref, xo_ref, xn_ref = rest
        nxt = _next_mods(gpre_ref, sc_ref, sh_ref, tm)
    else:
        (xo_ref,) = rest
        nxt = None
    x1, xn = _post_math(y, x_ref[...], gpost_ref[...], _mod_rows(gt_ref, tm), nxt)
    xo_ref[...] = x1
    if with_next:
        xn_ref[...] = xn.astype(xn_ref.dtype)


def _combine_post(grp, ys, pos1, pos2, meta, x, gpost, gt, nxt):
    tm = 256
    imap = lambda i, p1, p2: (i, 0)
    in_specs, args, out_specs, out_shape = _post_specs(grp, tm, x, gpost, gt, nxt, imap)
    out = pl.pallas_call(
        functools.partial(_combine_post_kernel, with_next=nxt is not None),
        grid_spec=pltpu.PrefetchScalarGridSpec(
            num_scalar_prefetch=2,
            grid=(grp.m // tm,),
            in_specs=[pl.BlockSpec(memory_space=pl.ANY), pl.BlockSpec((tm, LANES), imap)] + in_specs,
            out_specs=out_specs,
            scratch_shapes=[pltpu.VMEM((2, tm, D_MODEL), F32), pltpu.VMEM((2, tm, D_MODEL), F32),
                            pltpu.SemaphoreType.DMA((2,))],
        ),
        out_shape=out_shape,
        compiler_params=_params(("arbitrary",), [((tm, D_MODEL), F32)] * 7,
                                [((tm, D_MODEL), F32)] * 4),
        name="moe_combine_post",
    )(pos1, pos2, ys, meta, *args)
    return (out[0], out[1]) if nxt is not None else (out[0], None)


def _moe_post(grps, xns, p, j, xs_res, gpost, gts, nxts):
    tg = 512
    n_tiles = 2 * sum(g.m for g in grps) // tg + N_EXPERTS
    routed = [_router(g, xn, p['router_w'][j], p['router_b'][j]) for g, xn in zip(grps, xns)]
    metas, counts = zip(*routed)
    slots, tile_expert, n_used = _route_plan(metas, counts, tg, n_tiles)
    xs = jnp.zeros((n_tiles * tg, D_MODEL), F32)
    for g, xn, (pos1, pos2) in zip(grps, xns, slots):
        xs = _dispatch(g, xn, pos1, pos2, xs)
    ys = _experts(xs, tile_expert, n_used, p['moe_wg_bf'][j], p['moe_wu_bf'][j], p['moe_wd_bf'][j], tg)
    return [_combine_post(g, ys, pos1, pos2, meta, x, gpost, gt, nxt)
            for g, (pos1, pos2), meta, x, gt, nxt in zip(grps, slots, metas, xs_res, gts, nxts)]


def _mixer(grp, l, x, xn, mods, h0, conv0, p, nb, tt, xn_dtype):
    _, _, gt1, sh2, sc2, _ = mods
    w_in = p['w_in']
    xa = _mm_act(grp, xn, w_in, l, 0, D_MODEL, lambda y: y, F32, "in_proj_xa")
    gel = _mm_act(grp, xn, w_in, l, D_MODEL, 3 * D_MODEL, _gelu, BF16, "in_proj_gelu")
    sig = _mm_act(grp, xn, w_in, l, 4 * D_MODEL, 2 * D_MODEL, _sigmoid, BF16, "in_proj_sigmoid")
    ya, h_last, conv_new = _branch_a(
        grp, nb, tt, xa, gel, conv0, h0, p['conv_w'][l], p['conv_b'][l],
        p['w_r'][l], p['w_i'][l], p['b_r'][l], p['b_i'][l], p['lru_lambda'][l])
    if grp.t >= CHUNK:
        bias_full = jnp.repeat(p['b_s'][l].T, GROUP_B, axis=-1)
        yb = _branch_b(grp, gel, p['g_v'][l], p['w_s'][l], bias_full)
        vn = None
    else:
        yb, vn = _branch_b_short(grp, gel, p['g_v'][l], p['w_s'][l], p['b_s'][l])
    m = _merge(grp, ya, yb, sig, p['w_pa'], p['w_pb'], l)
    x, xn2 = _oproj_post(grp, m, p['w_o_bf'], l, x, p['g_post_mix'][l], gt1,
                         p['g_pre_ffn'][l], sc2, sh2, xn_dtype)
    return x, xn2, h_last.reshape(grp.b, D_MODEL), conv_new, vn


def _trunk(grps, xs, mods, h0s, conv0s, p, tiles):
    depth = p['w_in'].shape[0]
    n_g = len(grps)
    xns = [_prenorm(g, x, p['g_pre_mix'][0], mods[i][0][1], mods[i][0][0])
           for i, (g, x) in enumerate(zip(grps, xs))]
    states = [([], [], []) for _ in grps]
    for l in range(depth):
        dense = l % 2 == 0
        j = l // 2
        xn2s = []
        for i, g in enumerate(grps):
            xs[i], xn2, h_last, conv_new, vn = _mixer(
                g, l, xs[i], xns[i], mods[i][l], h0s[i][l], conv0s[i][l], p, *tiles[i],
                BF16 if dense else F32)
            xn2s.append(xn2)
            for acc, v in zip(states[i], (h_last, conv_new, vn)):
                acc.append(v)
        if l + 1 < depth:
            nxts = [(p['g_pre_mix'][l + 1], mods[i][l + 1][1], mods[i][l + 1][0]) for i in range(n_g)]
        else:
            nxts = [None] * n_g
        gt2s = [mods[i][l][5] for i in range(n_g)]
        gpost = p['g_post_ffn'][l]
        if dense:
            outs = []
            for i, g in enumerate(grps):
                y = _swiglu(g, xn2s[i], p['ffn_wg_bf'][j], p['ffn_wu_bf'][j], p['ffn_wd_bf'][j])
                outs.append(_post(g, y, xs[i], gpost, gt2s[i], nxts[i]))
        else:
            outs = _moe_post(grps, xn2s, p, j, xs, gpost, gt2s, nxts)
        xs = [o[0] for o in outs]
        xns = [o[1] for o in outs]
    return xs, states


def kernel(x_prompt, x_sample, c_prompt, c_sample, state_lru_h, state_lru_conv, w_ada, b_ada, g_pre_mix, g_post_mix, g_pre_ffn, g_post_ffn, w_in, conv_w, conv_b, w_r, b_r, w_i, b_i, lru_lambda, g_v, w_s, b_s, w_pa, w_pb, w_o, ffn_wg, ffn_wu, ffn_wd, router_w, router_b, moe_wg, moe_wu, moe_wd):
    p = dict(g_pre_mix=g_pre_mix, g_post_mix=g_post_mix, g_pre_ffn=g_pre_ffn, g_post_ffn=g_post_ffn,
             w_in=w_in, conv_w=conv_w, conv_b=conv_b, w_r=w_r, b_r=b_r, w_i=w_i, b_i=b_i,
             lru_lambda=lru_lambda, g_v=g_v, w_s=w_s, b_s=b_s, w_pa=w_pa, w_pb=w_pb,
             router_w=router_w, router_b=router_b, w_o_bf=w_o.astype(BF16),
             ffn_wg_bf=ffn_wg.astype(BF16), ffn_wu_bf=ffn_wu.astype(BF16), ffn_wd_bf=ffn_wd.astype(BF16),
             moe_wg_bf=moe_wg.astype(BF16), moe_wu_bf=moe_wu.astype(BF16), moe_wd_bf=moe_wd.astype(BF16))
    depth = w_in.shape[0]
    bp, tp, _ = x_prompt.shape
    bs, ts, _ = x_sample.shape
    assert tp % CHUNK == 0 and ts < CHUNK and ts % SUBLANES == 0

    n_c = bp + bs
    r_pad = -n_c % 16
    c_all = jnp.concatenate([c_prompt, c_sample, jnp.zeros((r_pad, D_MODEL), F32)], axis=0)
    mod = _ada(c_all, w_ada, b_ada).reshape(depth, n_c + r_pad, 6, D_MODEL)
    mods_p = [[mod[l, :bp, k].reshape(bp, 1, D_MODEL) for k in range(6)] for l in range(depth)]
    mods_s = [[mod[l, bp:n_c, k] for k in range(6)] for l in range(depth)]

    grp_p = _Group(bp, tp, short=False)
    grp_s = _Group(bs, ts, short=True)
    zeros_h = jnp.zeros((depth, bp, D_MODEL), F32)
    zeros_conv = jnp.zeros((depth, bp, CONV_W - 1, D_MODEL), F32)

    (y_p, y_s), (st_p, st_s) = _trunk(
        [grp_p, grp_s],
        [x_prompt.reshape(bp * tp, D_MODEL), x_sample.reshape(bs * ts, D_MODEL)],
        [mods_p, mods_s], [zeros_h, state_lru_h], [zeros_conv, state_lru_conv], p,
        tiles=[(1, 256), (16, ts)])
    return (y_p.reshape(bp, tp, D_MODEL), y_s.reshape(bs, ts, D_MODEL),
            jnp.stack(st_p[0]), jnp.stack(st_p[1]), jnp.stack(st_s[0]), jnp.stack(st_s[1]),
            jnp.stack(st_s[2]))
```

```python
import functools

import jax
import jax.numpy as jnp
from jax import lax
from jax.experimental import pallas as pl
from jax.experimental.pallas import tpu as pltpu

F32 = jnp.float32
BF16 = jnp.bfloat16

D_MODEL = 2048
N_HEADS_A = 16
HEAD_A = D_MODEL // N_HEADS_A
CONV_W = 4
C_GATE = 8.0
N_GROUPS_B = 16
GROUP_B = D_MODEL // N_GROUPS_B
CHUNK = 128
N_EXPERTS = 8
EPS = 1e-6

LANES = 128
SUBLANES = 8
VMEM_CAP = 58 * 2**20
CONV_PAD = SUBLANES


def _nbytes(shape, dtype):
    n = 1
    for s in shape:
        n *= s
    return n * jnp.dtype(dtype).itemsize


def _params(sem, blocks, scratch=()):
    need = 2 * sum(_nbytes(s, d) for s, d in blocks) + sum(_nbytes(s, d) for s, d in scratch)
    limit = min(VMEM_CAP, need + 16 * 2**20)
    return pltpu.CompilerParams(dimension_semantics=sem, vmem_limit_bytes=limit)


def _rms(x, g):
    return x * lax.rsqrt(jnp.mean(x * x, axis=-1, keepdims=True) + EPS) * g


def _sigmoid(x):
    return 1.0 / (1.0 + jnp.exp(-x))


def _silu(x):
    return x * _sigmoid(x)


def _gelu(x):
    return jax.nn.gelu(x)


class _Group:
    def __init__(self, b, t, short):
        self.b, self.t, self.m = b, t, b * t
        self.short = short

    def mod_spec(self, tm):
        if self.short:
            return pl.BlockSpec((tm // self.t, D_MODEL), lambda *ids: (ids[0], 0))
        t = self.t
        return pl.BlockSpec((None, 1, D_MODEL), lambda *ids: ((ids[0] * tm) // t, 0, 0))


def _mod_rows(ref, rows, offset=None):
    n = ref.shape[0]
    vals = ref[...] if offset is None else ref[...] + offset
    if n == 1:
        return vals
    rep = rows // n
    return jnp.concatenate([jnp.broadcast_to(vals[r:r + 1, :], (rep, D_MODEL)) for r in range(n)], axis=0)


def _ada_kernel(c_ref, w_ref, b_ref, o_ref):
    s = _silu(c_ref[...]).astype(BF16)
    o_ref[...] = jnp.dot(s, w_ref[...].astype(BF16), preferred_element_type=F32) + b_ref[...]


def _ada(c, w_ada, b_ada):
    depth, _, n = w_ada.shape
    r = c.shape[0]
    tn = 1024
    return pl.pallas_call(
        _ada_kernel,
        grid=(depth, n // tn),
        in_specs=[
            pl.BlockSpec((r, D_MODEL), lambda l, j: (0, 0)),
            pl.BlockSpec((None, D_MODEL, tn), lambda l, j: (l, 0, j)),
            pl.BlockSpec((None, 1, tn), lambda l, j: (l, 0, j)),
        ],
        out_specs=pl.BlockSpec((None, r, tn), lambda l, j: (l, 0, j)),
        out_shape=jax.ShapeDtypeStruct((depth, r, n), F32),
        compiler_params=_params(("arbitrary", "arbitrary"),
                                [((r, D_MODEL), F32), ((D_MODEL, tn), F32), ((r, tn), F32)],
                                [((D_MODEL, tn), BF16)]),
        name="ada_mod",
    )(c, w_ada, b_ada.reshape(depth, 1, n))


def _prenorm_kernel(x_ref, g_ref, sc_ref, sh_ref, o_ref):
    rows = x_ref.shape[0]
    o_ref[...] = (_rms(x_ref[...], g_ref[...]) * _mod_rows(sc_ref, rows, 1.0)
                  + _mod_rows(sh_ref, rows)).astype(BF16)


def _prenorm(grp, x, g, sc, sh):
    tm = 256 if grp.short else 512
    return pl.pallas_call(
        _prenorm_kernel,
        grid=(grp.m // tm,),
        in_specs=[
            pl.BlockSpec((tm, D_MODEL), lambda i: (i, 0)),
            pl.BlockSpec((1, D_MODEL), lambda i: (0, 0)),
            grp.mod_spec(tm), grp.mod_spec(tm),
        ],
        out_specs=pl.BlockSpec((tm, D_MODEL), lambda i: (i, 0)),
        out_shape=jax.ShapeDtypeStruct((grp.m, D_MODEL), BF16),
        compiler_params=_params(("arbitrary",), [((tm, D_MODEL), F32)] * 4),
        name="prenorm",
    )(x, g.reshape(1, D_MODEL), sc, sh)


def _mm_act_kernel(x_ref, w_ref, o_ref, wbf_ref, *, act):
    @pl.when(pl.program_id(1) == 0)
    def _():
        wbf_ref[...] = w_ref[...].astype(BF16)

    y = jnp.dot(x_ref[...], wbf_ref[...], preferred_element_type=F32)
    o_ref[...] = act(y).astype(o_ref.dtype)


def _mm_act(grp, x, w, layer, col0, ncols, act, out_dtype, name):
    tm, tn = 1024, 1024
    c0 = col0 // tn
    return pl.pallas_call(
        functools.partial(_mm_act_kernel, act=act),
        grid=(ncols // tn, grp.m // tm),
        in_specs=[
            pl.BlockSpec((tm, D_MODEL), lambda j, i: (i, 0)),
            pl.BlockSpec((None, D_MODEL, tn), lambda j, i: (layer, 0, c0 + j)),
        ],
        out_specs=pl.BlockSpec((tm, tn), lambda j, i: (i, j)),
        out_shape=jax.ShapeDtypeStruct((grp.m, ncols), out_dtype),
        scratch_shapes=[pltpu.VMEM((D_MODEL, tn), BF16)],
        compiler_params=_params(("arbitrary", "arbitrary"),
                                [((tm, D_MODEL), BF16), ((D_MODEL, tn), F32), ((tm, tn), F32)],
                                [((D_MODEL, tn), BF16), ((tm, tn), F32)]),
        name=name,
    )(x, w)


HEADS_PER_DOT = 2


def _lru_unit(xp_ref, seqs, t0, xsl, ga, carry, cw_ref, cb_ref, wr, wi, br, bi, softplus, sl, nb, tt):
    lo = CONV_PAD - (CONV_W - 1) + t0
    rows = nb * tt
    n_blk = tt // SUBLANES
    blk_shape = (nb * n_blk, SUBLANES, HEAD_A)
    t_in_blk = lax.broadcasted_iota(jnp.int32, blk_shape, 1)
    xc = cb_ref[:, sl][None]
    for k in range(CONV_W):
        xc = xc + cw_ref[k:k + 1, sl][None] * xp_ref[seqs, lo + k:lo + k + tt, xsl]
    xc = xc.reshape(rows, HEAD_A)
    xcb = xc.astype(BF16)
    r = _sigmoid(jnp.dot(xcb, wr, preferred_element_type=F32) + br)
    i = _sigmoid(jnp.dot(xcb, wi, preferred_element_type=F32) + bi)
    log_a = (-C_GATE * r) * softplus
    a = jnp.exp(log_a)
    th = jnp.tanh(log_a)
    one_m_a2 = -2.0 * th / (1.0 - th)
    b = jnp.where(one_m_a2 > 0.0, one_m_a2 * lax.rsqrt(one_m_a2), 0.0) * (i * xc)
    a = a.reshape(blk_shape)
    b = b.reshape(blk_shape)
    d = 1
    while d < SUBLANES:
        a_sh = pltpu.roll(a, d, axis=1)
        b_sh = pltpu.roll(b, d, axis=1)
        keep = t_in_blk >= d
        b = jnp.where(keep, a * b_sh + b, b)
        a = jnp.where(keep, a * a_sh, a)
        d *= 2
    a3 = a.reshape(nb, tt, HEAD_A)
    b3 = b.reshape(nb, tt, HEAD_A)
    h_blocks = []
    for k in range(n_blk):
        ks = slice(k * SUBLANES, (k + 1) * SUBLANES)
        h_k = a3[:, ks, :] * carry + b3[:, ks, :]
        carry = h_k[:, SUBLANES - 1:SUBLANES, :]
        h_blocks.append(h_k)
    h = jnp.concatenate(h_blocks, axis=1) if n_blk > 1 else h_blocks[0]
    return ga * h.reshape(rows, HEAD_A), carry


def _branch_a_kernel(x_ref, wxa_ref, wga_ref, conv0_ref, h0_ref, cw_ref, cb_ref, wr_ref, wi_ref,
                     br_ref, bi_ref, lam_ref, ya_ref, hlast_ref, convnew_ref, h_ref, *xp_refs,
                     n_seq, t_tile, nb, tt, tiles_per_seq):
    i = pl.program_id(1)
    hist = CONV_W - 1
    lo = CONV_PAD - hist
    n_col = HEADS_PER_DOT * HEAD_A
    pair_cols = [slice(pr * n_col, (pr + 1) * n_col) for pr in range(len(xp_refs))]

    @pl.when(i % tiles_per_seq == 0)
    def _():
        for xp_ref, ps in zip(xp_refs, pair_cols):
            xp_ref[:, lo:CONV_PAD, :] = conv0_ref[:, :, ps]
        h_ref[...] = h0_ref[...]

    z = -lam_ref[...]
    softplus = jnp.maximum(z, 0.0) + jnp.log1p(jnp.exp(-jnp.abs(z)))
    x = x_ref[...]

    def project(pr):
        ps = pair_cols[pr]
        xa = jnp.dot(x, wxa_ref[:, ps], preferred_element_type=F32)
        xp_refs[pr][:, CONV_PAD:CONV_PAD + t_tile, :] = xa.reshape(n_seq, t_tile, n_col)
        return _gelu(jnp.dot(x, wga_ref[:, ps], preferred_element_type=F32))

    ga_next = project(0)
    for pr, xp_ref in enumerate(xp_refs):
        ga = ga_next
        if pr + 1 < len(xp_refs):
            ga_next = project(pr + 1)
        for hh in range(HEADS_PER_DOT):
            hd = pr * HEADS_PER_DOT + hh
            xsl = slice(hh * HEAD_A, (hh + 1) * HEAD_A)
            sl = slice(hd * HEAD_A, (hd + 1) * HEAD_A)
            wr = wr_ref[hd].astype(BF16)
            wi = wi_ref[hd].astype(BF16)
            for us in range(n_seq // nb):
                seqs = slice(us * nb, (us + 1) * nb)
                carry = h_ref[seqs, :, sl]
                for ut in range(t_tile // tt):
                    r0 = us * nb * t_tile + ut * tt
                    y, carry = _lru_unit(
                        xp_ref, seqs, ut * tt, xsl, ga[r0:r0 + nb * tt, xsl], carry,
                        cw_ref, cb_ref, wr, wi, br_ref[:, sl], bi_ref[:, sl], softplus[:, sl], sl, nb, tt)
                    ya_ref[r0:r0 + nb * tt, sl] = y.astype(BF16)
                h_ref[seqs, :, sl] = carry

    for xp_ref in xp_refs:
        xp_ref[:, lo:CONV_PAD, :] = xp_ref[:, lo + t_tile:CONV_PAD + t_tile, :]

    @pl.when(i % tiles_per_seq == tiles_per_seq - 1)
    def _():
        hlast_ref[...] = h_ref[...]
        for xp_ref, ps in zip(xp_refs, pair_cols):
            convnew_ref[:, :, ps] = xp_ref[:, lo:CONV_PAD, :]


def _branch_a(grp, tm, tc, nb, tt, xn, w_ag, layer, conv0, h0, cw, cb, wr, wi, br, bi, lam):
    b, t = grp.b, grp.t
    t_tile = min(t, tm)
    n_seq = tm // t_tile
    tiles_per_seq = t // t_tile
    assert (nb == 1 or tt == t_tile) and n_seq % nb == 0 and t_tile % tt == 0
    hist = CONV_W - 1
    n_j = D_MODEL // tc
    seq_blk = lambda j, i: (i // tiles_per_seq, 0, j)
    vec = lambda: pl.BlockSpec((1, tc), lambda j, i: (0, j))
    head_w = lambda: pl.BlockSpec((tc // HEAD_A, HEAD_A, HEAD_A), lambda j, i: (j, 0, 0))
    return pl.pallas_call(
        functools.partial(_branch_a_kernel, n_seq=n_seq, t_tile=t_tile, nb=nb, tt=tt,
                          tiles_per_seq=tiles_per_seq),
        grid=(n_j, grp.m // tm),
        in_specs=[
            pl.BlockSpec((tm, D_MODEL), lambda j, i: (i, 0)),
            pl.BlockSpec((None, D_MODEL, tc), lambda j, i: (layer, 0, j)),
            pl.BlockSpec((None, D_MODEL, tc), lambda j, i: (layer, 0, n_j + j)),
            pl.BlockSpec((n_seq, hist, tc), seq_blk),
            pl.BlockSpec((n_seq, 1, tc), seq_blk),
            pl.BlockSpec((CONV_W, tc), lambda j, i: (0, j)),
            vec(), head_w(), head_w(), vec(), vec(), vec(),
        ],
        out_specs=[
            pl.BlockSpec((tm, tc), lambda j, i: (i, j)),
            pl.BlockSpec((n_seq, 1, tc), seq_blk),
            pl.BlockSpec((n_seq, hist, tc), seq_blk),
        ],
        out_shape=[
            jax.ShapeDtypeStruct((grp.m, D_MODEL), BF16),
            jax.ShapeDtypeStruct((b, 1, D_MODEL), F32),
            jax.ShapeDtypeStruct((b, hist, D_MODEL), F32),
        ],
        scratch_shapes=[pltpu.VMEM((n_seq, 1, tc), F32)]
        + [pltpu.VMEM((n_seq, CONV_PAD + t_tile, HEADS_PER_DOT * HEAD_A), F32)]
        * (tc // (HEADS_PER_DOT * HEAD_A)),
        compiler_params=_params(("arbitrary", "arbitrary"),
                                [((tm, D_MODEL), BF16), ((D_MODEL, tc), BF16), ((D_MODEL, tc), BF16),
                                 ((tm, tc), BF16)] + [((n_seq, SUBLANES, tc), F32)] * 4,
                                [((n_seq, CONV_PAD + t_tile, tc), F32), ((n_seq, SUBLANES, tc), F32),
                                 ((tm, 2 * HEADS_PER_DOT * HEAD_A), F32)]),
        name="branch_a",
    )(xn, w_ag, w_ag, conv0, h0.reshape(b, 1, D_MODEL), cw, cb.reshape(1, D_MODEL),
      wr, wi, br.reshape(1, D_MODEL), bi.reshape(1, D_MODEL), lam.reshape(1, D_MODEL))


def _branch_b_kernel(u_ref, v_ref, gv_ref, ws_ref, bias_ref, yb_ref, wm_ref, *, n_chunks):
    @pl.when(pl.program_id(0) == 0)
    def _():
        tri = (lax.broadcasted_iota(jnp.int32, (CHUNK, CHUNK), 0)
               >= lax.broadcasted_iota(jnp.int32, (CHUNK, CHUNK), 1))
        for g in range(N_GROUPS_B):
            wm_ref[g] = jnp.where(tri, ws_ref[g], 0.0).astype(BF16)

    vn = _rms(v_ref[...].astype(F32), gv_ref[...]).astype(BF16)
    for g in range(N_GROUPS_B):
        cs = slice(g * GROUP_B, (g + 1) * GROUP_B)
        rhs = jnp.concatenate([vn[c * CHUNK:(c + 1) * CHUNK, cs] for c in range(n_chunks)], axis=1)
        mixed = jnp.dot(wm_ref[g], rhs, preferred_element_type=F32)
        for c in range(n_chunks):
            rs = slice(c * CHUNK, (c + 1) * CHUNK)
            mixed_c = mixed[:, c * GROUP_B:(c + 1) * GROUP_B] + bias_ref[:, cs]
            yb_ref[rs, cs] = (u_ref[rs, cs].astype(F32) * mixed_c).astype(BF16)


def _branch_b(grp, gel, gv, ws, bias_full):
    tm = 512
    return pl.pallas_call(
        functools.partial(_branch_b_kernel, n_chunks=tm // CHUNK),
        grid=(grp.m // tm,),
        in_specs=[
            pl.BlockSpec((tm, D_MODEL), lambda i: (i, 0)),
            pl.BlockSpec((tm, D_MODEL), lambda i: (i, 1)),
            pl.BlockSpec((1, D_MODEL), lambda i: (0, 0)),
            pl.BlockSpec((N_GROUPS_B, CHUNK, CHUNK), lambda i: (0, 0, 0)),
            pl.BlockSpec((CHUNK, D_MODEL), lambda i: (0, 0)),
        ],
        out_specs=pl.BlockSpec((tm, D_MODEL), lambda i: (i, 0)),
        out_shape=jax.ShapeDtypeStruct((grp.m, D_MODEL), BF16),
        scratch_shapes=[pltpu.VMEM((N_GROUPS_B, CHUNK, CHUNK), BF16)],
        compiler_params=_params(("arbitrary",),
                                [((tm, D_MODEL), BF16)] * 3 + [((N_GROUPS_B, CHUNK, CHUNK), F32),
                                                                ((CHUNK, D_MODEL), F32)],
                                [((tm, D_MODEL), F32)] * 2),
        name="branch_b",
    )(gel, gel, gv.reshape(1, D_MODEL), ws, bias_full)


def _branch_b_short_kernel(u_ref, v_ref, gv_ref, wc_ref, bias_ref, yb_ref, vn_ref, *, nb, t):
    vn = _rms(v_ref[...].astype(F32), gv_ref[...]).reshape(nb, t, D_MODEL)
    vn_ref[...] = vn
    mixed = bias_ref[...][None]
    for s in range(t):
        mixed = mixed + wc_ref[s][None] * vn[:, s:s + 1, :]
    yb = u_ref[...].astype(F32).reshape(nb, t, D_MODEL) * mixed
    yb_ref[...] = yb.reshape(nb * t, D_MODEL).astype(BF16)


def _branch_b_short(grp, gel, gv, ws, bs):
    b, t = grp.b, grp.t
    nb = 16
    rows = nb * t
    tri = jnp.tril(jnp.ones((t, t), F32))
    wc = jnp.repeat(jnp.transpose(ws[:, :t, :t] * tri, (2, 1, 0)), GROUP_B, axis=-1)
    bias = jnp.repeat(bs[:, :t].T, GROUP_B, axis=-1)
    return pl.pallas_call(
        functools.partial(_branch_b_short_kernel, nb=nb, t=t),
        grid=(b // nb,),
        in_specs=[
            pl.BlockSpec((rows, D_MODEL), lambda i: (i, 0)),
            pl.BlockSpec((rows, D_MODEL), lambda i: (i, 1)),
            pl.BlockSpec((1, D_MODEL), lambda i: (0, 0)),
            pl.BlockSpec((t, t, D_MODEL), lambda i: (0, 0, 0)),
            pl.BlockSpec((t, D_MODEL), lambda i: (0, 0)),
        ],
        out_specs=[
            pl.BlockSpec((rows, D_MODEL), lambda i: (i, 0)),
            pl.BlockSpec((nb, t, D_MODEL), lambda i: (i, 0, 0)),
        ],
        out_shape=[
            jax.ShapeDtypeStruct((grp.m, D_MODEL), BF16),
            jax.ShapeDtypeStruct((b, t, D_MODEL), F32),
        ],
        compiler_params=_params(("arbitrary",),
                                [((rows, D_MODEL), BF16)] * 3 + [((rows, D_MODEL), F32),
                                                                  ((t, t, D_MODEL), F32)],
                                [((rows, D_MODEL), F32)] * 3),
        name="branch_b_short",
    )(gel, gel, gv.reshape(1, D_MODEL), wc, bias)


def _merge_kernel(ya_ref, yb_ref, sa_ref, sb_ref, wpa_ref, wpb_ref, o_ref, wa_ref, wb_ref):
    @pl.when(pl.program_id(1) == 0)
    def _():
        wa_ref[...] = wpa_ref[...].astype(BF16)
        wb_ref[...] = wpb_ref[...].astype(BF16)

    pa = jnp.dot(ya_ref[...], wa_ref[...], preferred_element_type=F32)
    pb = jnp.dot(yb_ref[...], wb_ref[...], preferred_element_type=F32)
    o_ref[...] = (sa_ref[...].astype(F32) * pa + sb_ref[...].astype(F32) * pb).astype(BF16)


def _merge(grp, ya, yb, sig, wpa, wpb, layer):
    tm, tn = 1024, 512
    nj = D_MODEL // tn
    return pl.pallas_call(
        _merge_kernel,
        grid=(nj, grp.m // tm),
        in_specs=[
            pl.BlockSpec((tm, D_MODEL), lambda j, i: (i, 0)),
            pl.BlockSpec((tm, D_MODEL), lambda j, i: (i, 0)),
            pl.BlockSpec((tm, tn), lambda j, i: (i, j)),
            pl.BlockSpec((tm, tn), lambda j, i: (i, nj + j)),
            pl.BlockSpec((None, D_MODEL, tn), lambda j, i: (layer, 0, j)),
            pl.BlockSpec((None, D_MODEL, tn), lambda j, i: (layer, 0, j)),
        ],
        out_specs=pl.BlockSpec((tm, tn), lambda j, i: (i, j)),
        out_shape=jax.ShapeDtypeStruct((grp.m, D_MODEL), BF16),
        scratch_shapes=[pltpu.VMEM((D_MODEL, tn), BF16), pltpu.VMEM((D_MODEL, tn), BF16)],
        compiler_params=_params(("arbitrary", "arbitrary"),
                                [((tm, D_MODEL), BF16)] * 2 + [((tm, tn), BF16)] * 3
                                + [((D_MODEL, tn), F32)] * 2,
                                [((D_MODEL, tn), BF16)] * 2 + [((tm, tn), F32)] * 2),
        name="merge",
    )(ya, yb, sig, sig, wpa, wpb)


def _post_math(y, x, gpost, gt, nxt):
    x1 = x + gt * _rms(y, gpost)
    if nxt is None:
        return x1, None
    gpre, one_plus_sc, sh = nxt
    return x1, _rms(x1, gpre) * one_plus_sc + sh


def _next_mods(gpre_ref, sc_ref, sh_ref, rows):
    return gpre_ref[...], _mod_rows(sc_ref, rows, 1.0), _mod_rows(sh_ref, rows)


def _oproj_post_kernel(m_ref, wo_ref, x_ref, gpost_ref, gt_ref, gpre_ref, sc_ref, sh_ref,
                       xo_ref, xn_ref):
    y = jnp.dot(m_ref[...], wo_ref[...], preferred_element_type=F32)
    rows = x_ref.shape[0]
    x1, xn = _post_math(y, x_ref[...], gpost_ref[...], _mod_rows(gt_ref, rows),
                        _next_mods(gpre_ref, sc_ref, sh_ref, rows))
    xo_ref[...] = x1
    xn_ref[...] = xn.astype(xn_ref.dtype)


def _oproj_post(grp, m, wo, layer, x, gpost, gt, gpre, sc, sh, xn_dtype):
    tm = 512
    vec = lambda: pl.BlockSpec((1, D_MODEL), lambda i: (0, 0))
    row = lambda: pl.BlockSpec((tm, D_MODEL), lambda i: (i, 0))
    return pl.pallas_call(
        _oproj_post_kernel,
        grid=(grp.m // tm,),
        in_specs=[
            row(),
            pl.BlockSpec((None, D_MODEL, D_MODEL), lambda i: (layer, 0, 0),
                         pipeline_mode=pl.Buffered(1)),
            row(), vec(), grp.mod_spec(tm), vec(), grp.mod_spec(tm), grp.mod_spec(tm),
        ],
        out_specs=[row(), row()],
        out_shape=[
            jax.ShapeDtypeStruct((grp.m, D_MODEL), F32),
            jax.ShapeDtypeStruct((grp.m, D_MODEL), xn_dtype),
        ],
        compiler_params=_params(("arbitrary",),
                                [((tm, D_MODEL), F32)] * 5,
                                [((D_MODEL, D_MODEL), BF16), ((tm, D_MODEL), F32)]),
        name="oproj_post",
    )(m, wo, x, gpost.reshape(1, D_MODEL), gt, gpre.reshape(1, D_MODEL), sc, sh)


def _post_kernel(y_ref, x_ref, gpost_ref, gt_ref, *rest, with_next):
    rows = x_ref.shape[0]
    if with_next:
        gpre_ref, sc_ref, sh_ref, xo_ref, xn_ref = rest
        nxt = _next_mods(gpre_ref, sc_ref, sh_ref, rows)
    else:
        (xo_ref,) = rest
        nxt = None
    x1, xn = _post_math(y_ref[...], x_ref[...], gpost_ref[...], _mod_rows(gt_ref, rows), nxt)
    xo_ref[...] = x1
    if with_next:
        xn_ref[...] = xn.astype(xn_ref.dtype)


def _post_specs(grp, tm, x, gpost, gt, nxt, imap):
    vec = lambda: pl.BlockSpec((1, D_MODEL), lambda *ids: (0, 0))
    row = lambda: pl.BlockSpec((tm, D_MODEL), imap)
    in_specs = [row(), vec(), grp.mod_spec(tm)]
    args = [x, gpost.reshape(1, D_MODEL), gt]
    out_specs = [row()]
    out_shape = [jax.ShapeDtypeStruct((grp.m, D_MODEL), F32)]
    if nxt is not None:
        gpre, sc, sh = nxt
        in_specs += [vec(), grp.mod_spec(tm), grp.mod_spec(tm)]
        args += [gpre.reshape(1, D_MODEL), sc, sh]
        out_specs.append(row())
        out_shape.append(jax.ShapeDtypeStruct((grp.m, D_MODEL), BF16))
    return in_specs, args, out_specs, out_shape


def _post(grp, y, x, gpost, gt, nxt):
    tm = 256
    imap = lambda i: (i, 0)
    in_specs, args, out_specs, out_shape = _post_specs(grp, tm, x, gpost, gt, nxt, imap)
    out = pl.pallas_call(
        functools.partial(_post_kernel, with_next=nxt is not None),
        grid=(grp.m // tm,),
        in_specs=[pl.BlockSpec((tm, D_MODEL), imap)] + in_specs,
        out_specs=out_specs,
        out_shape=out_shape,
        compiler_params=_params(("arbitrary",), [((tm, D_MODEL), F32)] * 8),
        name="post",
    )(y, *args)
    return (out[0], out[1]) if nxt is not None else (out[0], None)


def _swiglu_tile(x, wg_ref, wu_ref, wd_ref):
    g = jnp.dot(x, wg_ref[...], preferred_element_type=F32)
    u = jnp.dot(x, wu_ref[...], preferred_element_type=F32)
    h = _silu(g) * u
    return jnp.dot(h.astype(BF16), wd_ref[...], preferred_element_type=F32)


def _swiglu_kernel(x_ref, wg_ref, wu_ref, wd_ref, o_ref):
    @pl.when(pl.program_id(1) == 0)
    def _():
        o_ref[...] = jnp.zeros_like(o_ref)

    o_ref[...] += _swiglu_tile(x_ref[...], wg_ref, wu_ref, wd_ref)


def _swiglu(grp, x, wg, wu, wd):
    f = wg.shape[-1]
    tm, tf = 1024, 512
    return pl.pallas_call(
        _swiglu_kernel,
        grid=(grp.m // tm, f // tf),
        in_specs=[
            pl.BlockSpec((tm, D_MODEL), lambda i, j: (i, 0)),
            pl.BlockSpec((D_MODEL, tf), lambda i, j: (0, j)),
            pl.BlockSpec((D_MODEL, tf), lambda i, j: (0, j)),
            pl.BlockSpec((tf, D_MODEL), lambda i, j: (j, 0)),
        ],
        out_specs=pl.BlockSpec((tm, D_MODEL), lambda i, j: (i, 0)),
        out_shape=jax.ShapeDtypeStruct((grp.m, D_MODEL), F32),
        compiler_params=_params(("arbitrary", "arbitrary"),
                                [((tm, D_MODEL), BF16), ((tm, D_MODEL), F32)]
                                + [((D_MODEL, tf), BF16)] * 3,
                                [((tm, tf), F32)] * 4),
        name="swiglu_dense",
    )(x, wg, wu, wd)


META_E1, META_E2, META_R1, META_R2, META_W1, META_W2 = range(6)
DMA_UNROLL = 8


def _split_bf16(v):
    hi = v.astype(BF16)
    return hi, (v - hi.astype(F32)).astype(BF16)


def _router_kernel(x_ref, rw_ref, rb_ref, meta_ref, cnt_ref, tri_ref):
    tm = x_ref.shape[0]

    @pl.when(pl.program_id(0) == 0)
    def _():
        cnt_ref[...] = jnp.zeros_like(cnt_ref)
        tri_ref[...] = (lax.broadcasted_iota(jnp.int32, (tm, tm), 0)
                        >= lax.broadcasted_iota(jnp.int32, (tm, tm), 1)).astype(BF16)

    xh, xl = _split_bf16(x_ref[...])
    wh, wl = _split_bf16(rw_ref[...])
    logits = (jnp.dot(xh, wh, preferred_element_type=F32) + jnp.dot(xl, wh, preferred_element_type=F32)
              + jnp.dot(xh, wl, preferred_element_type=F32)) + rb_ref[...]
    lane = lax.broadcasted_iota(jnp.int32, logits.shape, 1)
    logits = jnp.where(lane < N_EXPERTS, logits, -jnp.inf)
    ex = jnp.exp(logits - jnp.max(logits, axis=-1, keepdims=True))
    p = ex / jnp.sum(ex, axis=-1, keepdims=True)
    p1 = jnp.max(p, axis=-1, keepdims=True)
    i1 = jnp.min(jnp.where(p == p1, lane, LANES), axis=-1, keepdims=True)
    rest = jnp.where(lane == i1, -1.0, p)
    p2 = jnp.max(rest, axis=-1, keepdims=True)
    i2 = jnp.min(jnp.where(rest == p2, lane, LANES), axis=-1, keepdims=True)
    denom = p1 + p2
    onehot = jnp.where(lane == i1, 1.0, 0.0) + jnp.where(lane == i2, 1.0, 0.0)
    cum = jnp.dot(tri_ref[...], onehot.astype(BF16), preferred_element_type=F32) + cnt_ref[...]
    cnt_ref[...] = cum[tm - 1:tm, :]
    r1 = jnp.sum(jnp.where(lane == i1, cum, 0.0), axis=-1, keepdims=True) - 1.0
    r2 = jnp.sum(jnp.where(lane == i2, cum, 0.0), axis=-1, keepdims=True) - 1.0
    meta = jnp.zeros_like(logits)
    for k, v in ((META_E1, i1.astype(F32)), (META_E2, i2.astype(F32)), (META_R1, r1), (META_R2, r2),
                 (META_W1, p1 / denom), (META_W2, p2 / denom)):
        meta = jnp.where(lane == k, v, meta)
    meta_ref[...] = meta


def _router(grp, x, rw, rb):
    tm = 512
    rw_pad = jnp.pad(rw, ((0, 0), (0, LANES - N_EXPERTS)))
    rb_pad = jnp.pad(rb, (0, LANES - N_EXPERTS)).reshape(1, LANES)
    return pl.pallas_call(
        _router_kernel,
        grid=(grp.m // tm,),
        in_specs=[
            pl.BlockSpec((tm, D_MODEL), lambda i: (i, 0)),
            pl.BlockSpec((D_MODEL, LANES), lambda i: (0, 0)),
            pl.BlockSpec((1, LANES), lambda i: (0, 0)),
        ],
        out_specs=[
            pl.BlockSpec((tm, LANES), lambda i: (i, 0)),
            pl.BlockSpec((1, LANES), lambda i: (0, 0)),
        ],
        out_shape=[
            jax.ShapeDtypeStruct((grp.m, LANES), F32),
            jax.ShapeDtypeStruct((1, LANES), F32),
        ],
        scratch_shapes=[pltpu.VMEM((tm, tm), BF16)],
        compiler_params=_params(("arbitrary",),
                                [((tm, D_MODEL), F32), ((D_MODEL, LANES), F32), ((tm, LANES), F32)],
                                [((tm, tm), BF16), ((tm, D_MODEL), F32)]),
        name="router",
    )(x, rw_pad, rb_pad)


def _route_plan(metas, counts, tg, n_tiles):
    cnts = [c[0, :N_EXPERTS].astype(jnp.int32) for c in counts]
    total = sum(cnts)
    padded = (total + tg - 1) // tg * tg
    ends = jnp.cumsum(padded)
    offs = ends - padded
    slots = []
    for meta, before in zip(metas, [sum(cnts[:g], jnp.zeros_like(total)) for g in range(len(cnts))]):
        idx = meta[:, :META_W1].astype(jnp.int32)
        start = offs + before
        slots.append((start[idx[:, META_E1]] + idx[:, META_R1], start[idx[:, META_E2]] + idx[:, META_R2]))
    n_used = ends[-1] // tg
    tile = jnp.arange(n_tiles, dtype=jnp.int32)
    tile_c = jnp.minimum(tile, n_used - 1)
    tile_expert = jnp.sum(tile_c[:, None] * tg >= ends[None, :], axis=1).astype(jnp.int32)
    return slots, tile_expert, n_used.reshape(1).astype(jnp.int32)


def _dispatch_kernel(pos1_ref, pos2_ref, x_ref, xs_in_ref, xs_ref, sem):
    del xs_in_ref
    tm = x_ref.shape[0]
    base = pl.program_id(0) * tm

    def row_copy(t, slot):
        return pltpu.make_async_copy(x_ref.at[pl.ds(t, 1)], xs_ref.at[pl.ds(slot, 1)], sem)

    def start(t, c):
        row_copy(t, pos1_ref[base + t]).start()
        row_copy(t, pos2_ref[base + t]).start()
        return c

    def wait(t, c):
        row_copy(t, 0).wait()
        row_copy(t, 0).wait()
        return c

    lax.fori_loop(0, tm, start, 0, unroll=DMA_UNROLL)
    lax.fori_loop(0, tm, wait, 0, unroll=DMA_UNROLL)


def _dispatch(grp, x, pos1, pos2, xs):
    tm = 256
    return pl.pallas_call(
        _dispatch_kernel,
        grid_spec=pltpu.PrefetchScalarGridSpec(
            num_scalar_prefetch=2,
            grid=(grp.m // tm,),
            in_specs=[
                pl.BlockSpec((tm, D_MODEL), lambda i, p1, p2: (i, 0)),
                pl.BlockSpec(memory_space=pl.ANY),
            ],
            out_specs=pl.BlockSpec(memory_space=pl.ANY),
            scratch_shapes=[pltpu.SemaphoreType.DMA(())],
        ),
        out_shape=jax.ShapeDtypeStruct(xs.shape, F32),
        input_output_aliases={3: 0},
        compiler_params=_params(("arbitrary",), [((tm, D_MODEL), F32)]),
        name="moe_dispatch",
    )(pos1, pos2, x, xs)


def _experts_kernel(te_ref, nu_ref, xs_ref, wg_ref, wu_ref, wd_ref, o_ref):
    del te_ref
    r, j = pl.program_id(0), pl.program_id(1)

    @pl.when(j == 0)
    def _():
        o_ref[...] = jnp.zeros_like(o_ref)

    @pl.when(r < nu_ref[0])
    def _():
        o_ref[...] += _swiglu_tile(xs_ref[...].astype(BF16), wg_ref, wu_ref, wd_ref)


def _experts(xs, tile_expert, n_used, wg, wu, wd, tg):
    n_rows = xs.shape[0]
    f = wg.shape[-1]
    tf = 512
    n_j = f // tf

    def used(r, nu):
        return r < nu[0]

    def x_map(r, j, te, nu):
        return (jnp.minimum(r, nu[0] - 1), 0)

    def up_map(r, j, te, nu):
        return (te[r], 0, jnp.where(used(r, nu), j, n_j - 1))

    def down_map(r, j, te, nu):
        return (te[r], jnp.where(used(r, nu), j, n_j - 1), 0)

    return pl.pallas_call(
        _experts_kernel,
        grid_spec=pltpu.PrefetchScalarGridSpec(
            num_scalar_prefetch=2,
            grid=(n_rows // tg, n_j),
            in_specs=[
                pl.BlockSpec((tg, D_MODEL), x_map),
                pl.BlockSpec((None, D_MODEL, tf), up_map),
                pl.BlockSpec((None, D_MODEL, tf), up_map),
                pl.BlockSpec((None, tf, D_MODEL), down_map),
            ],
            out_specs=pl.BlockSpec((tg, D_MODEL), lambda r, j, te, nu: (r, 0)),
        ),
        out_shape=jax.ShapeDtypeStruct((n_rows, D_MODEL), F32),
        compiler_params=_params(("arbitrary", "arbitrary"),
                                [((tg, D_MODEL), F32)] * 2 + [((D_MODEL, tf), BF16)] * 3,
                                [((tg, tf), F32)] * 4 + [((tg, D_MODEL), BF16)]),
        name="moe_experts",
    )(tile_expert, n_used, xs, wg, wu, wd)


def _combine_post_kernel(pos1_ref, pos2_ref, ys_ref, meta_ref, x_ref, gpost_ref, gt_ref, *rest,
                         with_next):
    *rest, buf1_ref, buf2_ref, sem = rest
    tm = x_ref.shape[0]
    step, n_steps = pl.program_id(0), pl.num_programs(0)
    cur = step % 2

    def row_copy(row, buf_ref, half, t):
        return pltpu.make_async_copy(ys_ref.at[pl.ds(row, 1)], buf_ref.at[half, pl.ds(t, 1)],
                                     sem.at[half])

    def gather(tile, half):
        def start(t, c):
            row_copy(pos1_ref[tile * tm + t], buf1_ref, half, t).start()
            row_copy(pos2_ref[tile * tm + t], buf2_ref, half, t).start()
            return c
        lax.fori_loop(0, tm, start, 0, unroll=DMA_UNROLL)

    @pl.when(step == 0)
    def _():
        gather(0, 0)

    @pl.when(step + 1 < n_steps)
    def _():
        gather(step + 1, 1 - cur)

    def wait(t, c):
        row_copy(0, buf1_ref, cur, t).wait()
        row_copy(0, buf2_ref, cur, t).wait()
        return c

    lax.fori_loop(0, tm, wait, 0, unroll=DMA_UNROLL)
    meta = meta_ref[...]
    y = meta[:, META_W1:META_W1 + 1] * buf1_ref[cur] + meta[:, META_W2:META_W2 + 1] * buf2_ref[cur]
    if with_next:
        gpre_ref, sc_ref, sh_ref, xo_ref, xn_ref = rest
        nxt = _next_mods(gpre_ref, sc_ref, sh_ref, tm)
    else:
        (xo_ref,) = rest
        nxt = None
    x1, xn = _post_math(y, x_ref[...], gpost_ref[...], _mod_rows(gt_ref, tm), nxt)
    xo_ref[...] = x1
    if with_next:
        xn_ref[...] = xn.astype(xn_ref.dtype)


def _combine_post(grp, ys, pos1, pos2, meta, x, gpost, gt, nxt):
    tm = 256
    imap = lambda i, p1, p2: (i, 0)
    in_specs, args, out_specs, out_shape = _post_specs(grp, tm, x, gpost, gt, nxt, imap)
    out = pl.pallas_call(
        functools.partial(_combine_post_kernel, with_next=nxt is not None),
        grid_spec=pltpu.PrefetchScalarGridSpec(
            num_scalar_prefetch=2,
            grid=(grp.m // tm,),
            in_specs=[pl.BlockSpec(memory_space=pl.ANY), pl.BlockSpec((tm, LANES), imap)] + in_specs,
            out_specs=out_specs,
            scratch_shapes=[pltpu.VMEM((2, tm, D_MODEL), F32), pltpu.VMEM((2, tm, D_MODEL), F32),
                            pltpu.SemaphoreType.DMA((2,))],
        ),
        out_shape=out_shape,
        compiler_params=_params(("arbitrary",), [((tm, D_MODEL), F32)] * 7,
                                [((tm, D_MODEL), F32)] * 4),
        name="moe_combine_post",
    )(pos1, pos2, ys, meta, *args)
    return (out[0], out[1]) if nxt is not None else (out[0], None)


def _moe_post(grps, xns, p, j, xs_res, gpost, gts, nxts):
    tg = 512
    n_tiles = 2 * sum(g.m for g in grps) // tg + N_EXPERTS
    routed = [_router(g, xn, p['router_w'][j], p['router_b'][j]) for g, xn in zip(grps, xns)]
    metas, counts = zip(*routed)
    slots, tile_expert, n_used = _route_plan(metas, counts, tg, n_tiles)
    xs = jnp.zeros((n_tiles * tg, D_MODEL), F32)
    for g, xn, (pos1, pos2) in zip(grps, xns, slots):
        xs = _dispatch(g, xn, pos1, pos2, xs)
    ys = _experts(xs, tile_expert, n_used, p['moe_wg_bf'][j], p['moe_wu_bf'][j], p['moe_wd_bf'][j], tg)
    return [_combine_post(g, ys, pos1, pos2, meta, x, gpost, gt, nxt)
            for g, (pos1, pos2), meta, x, gt, nxt in zip(grps, slots, metas, xs_res, gts, nxts)]


def _mixer(grp, l, x, xn, mods, h0, conv0, p, a_tiles, xn_dtype):
    _, _, gt1, sh2, sc2, _ = mods
    w_in = p['w_in']
    ya, h_last, conv_new = _branch_a(
        grp, *a_tiles, xn, p['w_ag_bf'], l, conv0, h0, p['conv_w'][l], p['conv_b'][l],
        p['w_r'][l], p['w_i'][l], p['b_r'][l], p['b_i'][l], p['lru_lambda'][l])
    gel = _mm_act(grp, xn, w_in, l, 2 * D_MODEL, 2 * D_MODEL, _gelu, BF16, "in_proj_gelu")
    sig = _mm_act(grp, xn, w_in, l, 4 * D_MODEL, 2 * D_MODEL, _sigmoid, BF16, "in_proj_sigmoid")
    if grp.t >= CHUNK:
        bias_full = jnp.repeat(p['b_s'][l].T, GROUP_B, axis=-1)
        yb = _branch_b(grp, gel, p['g_v'][l], p['w_s'][l], bias_full)
        vn = None
    else:
        yb, vn = _branch_b_short(grp, gel, p['g_v'][l], p['w_s'][l], p['b_s'][l])
    m = _merge(grp, ya, yb, sig, p['w_pa'], p['w_pb'], l)
    x, xn2 = _oproj_post(grp, m, p['w_o_bf'], l, x, p['g_post_mix'][l], gt1,
                         p['g_pre_ffn'][l], sc2, sh2, xn_dtype)
    return x, xn2, h_last.reshape(grp.b, D_MODEL), conv_new, vn


def _trunk(grps, xs, mods, h0s, conv0s, p, tiles):
    depth = p['w_in'].shape[0]
    n_g = len(grps)
    xns = [_prenorm(g, x, p['g_pre_mix'][0], mods[i][0][1], mods[i][0][0])
           for i, (g, x) in enumerate(zip(grps, xs))]
    states = [([], [], []) for _ in grps]
    for l in range(depth):
        dense = l % 2 == 0
        j = l // 2
        xn2s = []
        for i, g in enumerate(grps):
            xs[i], xn2, h_last, conv_new, vn = _mixer(
                g, l, xs[i], xns[i], mods[i][l], h0s[i][l], conv0s[i][l], p, tiles[i],
                BF16 if dense else F32)
            xn2s.append(xn2)
            for acc, v in zip(states[i], (h_last, conv_new, vn)):
                acc.append(v)
        if l + 1 < depth:
            nxts = [(p['g_pre_mix'][l + 1], mods[i][l + 1][1], mods[i][l + 1][0]) for i in range(n_g)]
        else:
            nxts = [None] * n_g
        gt2s = [mods[i][l][5] for i in range(n_g)]
        gpost = p['g_post_ffn'][l]
        if dense:
            outs = []
            for i, g in enumerate(grps):
                y = _swiglu(g, xn2s[i], p['ffn_wg_bf'][j], p['ffn_wu_bf'][j], p['ffn_wd_bf'][j])
                outs.append(_post(g, y, xs[i], gpost, gt2s[i], nxts[i]))
        else:
            outs = _moe_post(grps, xn2s, p, j, xs, gpost, gt2s, nxts)
        xs = [o[0] for o in outs]
        xns = [o[1] for o in outs]
    return xs, states


def kernel(x_prompt, x_sample, c_prompt, c_sample, state_lru_h, state_lru_conv, w_ada, b_ada, g_pre_mix, g_post_mix, g_pre_ffn, g_post_ffn, w_in, conv_w, conv_b, w_r, b_r, w_i, b_i, lru_lambda, g_v, w_s, b_s, w_pa, w_pb, w_o, ffn_wg, ffn_wu, ffn_wd, router_w, router_b, moe_wg, moe_wu, moe_wd):
    p = dict(g_pre_mix=g_pre_mix, g_post_mix=g_post_mix, g_pre_ffn=g_pre_ffn, g_post_ffn=g_post_ffn,
             w_in=w_in, conv_w=conv_w, conv_b=conv_b, w_r=w_r, b_r=b_r, w_i=w_i, b_i=b_i,
             lru_lambda=lru_lambda, g_v=g_v, w_s=w_s, b_s=b_s, w_pa=w_pa, w_pb=w_pb,
             router_w=router_w, router_b=router_b, w_o_bf=w_o.astype(BF16),
             w_ag_bf=w_in[:, :, :2 * D_MODEL].astype(BF16),
             ffn_wg_bf=ffn_wg.astype(BF16), ffn_wu_bf=ffn_wu.astype(BF16), ffn_wd_bf=ffn_wd.astype(BF16),
             moe_wg_bf=moe_wg.astype(BF16), moe_wu_bf=moe_wu.astype(BF16), moe_wd_bf=moe_wd.astype(BF16))
    depth = w_in.shape[0]
    bp, tp, _ = x_prompt.shape
    bs, ts, _ = x_sample.shape
    assert tp % CHUNK == 0 and ts < CHUNK and ts % SUBLANES == 0

    n_c = bp + bs
    r_pad = -n_c % 16
    c_all = jnp.concatenate([c_prompt, c_sample, jnp.zeros((r_pad, D_MODEL), F32)], axis=0)
    mod = _ada(c_all, w_ada, b_ada).reshape(depth, n_c + r_pad, 6, D_MODEL)
    mods_p = [[mod[l, :bp, k].reshape(bp, 1, D_MODEL) for k in range(6)] for l in range(depth)]
    mods_s = [[mod[l, bp:n_c, k] for k in range(6)] for l in range(depth)]

    grp_p = _Group(bp, tp, short=False)
    grp_s = _Group(bs, ts, short=True)
    zeros_h = jnp.zeros((depth, bp, D_MODEL), F32)
    zeros_conv = jnp.zeros((depth, bp, CONV_W - 1, D_MODEL), F32)

    (y_p, y_s), (st_p, st_s) = _trunk(
        [grp_p, grp_s],
        [x_prompt.reshape(bp * tp, D_MODEL), x_sample.reshape(bs * ts, D_MODEL)],
        [mods_p, mods_s], [zeros_h, state_lru_h], [zeros_conv, state_lru_conv], p,
        tiles=[(1024, 1024, 1, 256), (512, 1024, 16, ts)])
    return (y_p.reshape(bp, tp, D_MODEL), y_s.reshape(bs, ts, D_MODEL),
            jnp.stack(st_p[0]), jnp.stack(st_p[1]), jnp.stack(st_s[0]), jnp.stack(st_s[1]),
            jnp.stack(st_s[2]))
```

```python
import functools

import jax
import jax.numpy as jnp
from jax import lax
from jax.experimental import pallas as pl
from jax.experimental.pallas import tpu as pltpu

F32 = jnp.float32
BF16 = jnp.bfloat16

D_MODEL = 2048
N_HEADS_A = 16
HEAD_A = D_MODEL // N_HEADS_A
CONV_W = 4
C_GATE = 8.0
N_GROUPS_B = 16
GROUP_B = D_MODEL // N_GROUPS_B
CHUNK = 128
N_EXPERTS = 8
EPS = 1e-6

LANES = 128
SUBLANES = 8
VMEM_CAP = 58 * 2**20
CONV_PAD = SUBLANES


def _nbytes(shape, dtype):
    n = 1
    for s in shape:
        n *= s
    return n * jnp.dtype(dtype).itemsize


def _params(sem, blocks, scratch=()):
    need = 2 * sum(_nbytes(s, d) for s, d in blocks) + sum(_nbytes(s, d) for s, d in scratch)
    limit = min(VMEM_CAP, need + 16 * 2**20)
    return pltpu.CompilerParams(dimension_semantics=sem, vmem_limit_bytes=limit)


def _rms(x, g):
    return x * lax.rsqrt(jnp.mean(x * x, axis=-1, keepdims=True) + EPS) * g


def _sigmoid(x):
    return 1.0 / (1.0 + jnp.exp(-x))


def _sigmoid_tanh(x):
    return 0.5 * jnp.tanh(0.5 * x) + 0.5


def _silu(x):
    return x * _sigmoid(x)


def _gelu(x):
    return jax.nn.gelu(x)


class _Group:
    def __init__(self, b, t, short):
        self.b, self.t, self.m = b, t, b * t
        self.short = short

    def mod_spec(self, tm):
        if self.short:
            return pl.BlockSpec((tm // self.t, D_MODEL), lambda *ids: (ids[0], 0))
        t = self.t
        return pl.BlockSpec((None, 1, D_MODEL), lambda *ids: ((ids[0] * tm) // t, 0, 0))


def _mod_rows(ref, rows, offset=None):
    n = ref.shape[0]
    vals = ref[...] if offset is None else ref[...] + offset
    if n == 1:
        return vals
    rep = rows // n
    return jnp.concatenate([jnp.broadcast_to(vals[r:r + 1, :], (rep, D_MODEL)) for r in range(n)], axis=0)


def _ada_kernel(c_ref, w_ref, b_ref, o_ref):
    s = _silu(c_ref[...]).astype(BF16)
    o_ref[...] = jnp.dot(s, w_ref[...].astype(BF16), preferred_element_type=F32) + b_ref[...]


def _ada(c, w_ada, b_ada):
    depth, _, n = w_ada.shape
    r = c.shape[0]
    tn = 1024
    per_item = D_MODEL // tn
    return pl.pallas_call(
        _ada_kernel,
        grid=(depth, n // tn),
        in_specs=[
            pl.BlockSpec((r, D_MODEL), lambda l, j: (0, 0)),
            pl.BlockSpec((None, D_MODEL, tn), lambda l, j: (l, 0, j)),
            pl.BlockSpec((None, 1, tn), lambda l, j: (l, 0, j)),
        ],
        out_specs=pl.BlockSpec((None, None, r, tn), lambda l, j: (l, j // per_item, 0, j % per_item)),
        out_shape=jax.ShapeDtypeStruct((depth, n // D_MODEL, r, D_MODEL), F32),
        compiler_params=_params(("arbitrary", "arbitrary"),
                                [((r, D_MODEL), F32), ((D_MODEL, tn), F32), ((r, tn), F32)],
                                [((D_MODEL, tn), BF16)]),
        name="ada_mod",
    )(c, w_ada, b_ada.reshape(depth, 1, n))


def _prenorm_kernel(x_ref, g_ref, sc_ref, sh_ref, o_ref):
    rows = x_ref.shape[0]
    o_ref[...] = (_rms(x_ref[...], g_ref[...]) * _mod_rows(sc_ref, rows, 1.0)
                  + _mod_rows(sh_ref, rows)).astype(BF16)


def _prenorm(grp, x, g, sc, sh):
    tm = 256 if grp.short else 512
    return pl.pallas_call(
        _prenorm_kernel,
        grid=(grp.m // tm,),
        in_specs=[
            pl.BlockSpec((tm, D_MODEL), lambda i: (i, 0)),
            pl.BlockSpec((1, D_MODEL), lambda i: (0, 0)),
            grp.mod_spec(tm), grp.mod_spec(tm),
        ],
        out_specs=pl.BlockSpec((tm, D_MODEL), lambda i: (i, 0)),
        out_shape=jax.ShapeDtypeStruct((grp.m, D_MODEL), BF16),
        compiler_params=_params(("arbitrary",), [((tm, D_MODEL), F32)] * 4),
        name="prenorm",
    )(x, g.reshape(1, D_MODEL), sc, sh)


def _mm_act_kernel(x_ref, w_ref, o_ref, wbf_ref, *, act):
    @pl.when(pl.program_id(1) == 0)
    def _():
        wbf_ref[...] = w_ref[...].astype(BF16)

    y = jnp.dot(x_ref[...], wbf_ref[...], preferred_element_type=F32)
    o_ref[...] = act(y).astype(o_ref.dtype)


def _mm_act(grp, x, w, layer, col0, ncols, act, out_dtype, name):
    tm, tn = 1024, 1024
    c0 = col0 // tn
    return pl.pallas_call(
        functools.partial(_mm_act_kernel, act=act),
        grid=(ncols // tn, grp.m // tm),
        in_specs=[
            pl.BlockSpec((tm, D_MODEL), lambda j, i: (i, 0)),
            pl.BlockSpec((None, D_MODEL, tn), lambda j, i: (layer, 0, c0 + j)),
        ],
        out_specs=pl.BlockSpec((tm, tn), lambda j, i: (i, j)),
        out_shape=jax.ShapeDtypeStruct((grp.m, ncols), out_dtype),
        scratch_shapes=[pltpu.VMEM((D_MODEL, tn), BF16)],
        compiler_params=_params(("arbitrary", "arbitrary"),
                                [((tm, D_MODEL), BF16), ((D_MODEL, tn), F32), ((tm, tn), F32)],
                                [((D_MODEL, tn), BF16), ((tm, tn), F32)]),
        name=name,
    )(x, w)


HEADS_PER_DOT = 2


def _lru_unit(xp_ref, seqs, t0, xsl, ga, carry, cw_ref, cb_ref, wr, wi, br, bi, softplus, sl, nb, tt):
    rows = nb * tt
    n_blk = tt // SUBLANES
    blk_shape = (nb * n_blk, SUBLANES, HEAD_A)
    t_in_blk = lax.broadcasted_iota(jnp.int32, blk_shape, 1)
    xb = xp_ref[seqs, t0:t0 + CONV_PAD + tt, xsl].reshape(nb * (n_blk + 1), SUBLANES, HEAD_A)
    cur = xb.reshape(nb, n_blk + 1, SUBLANES, HEAD_A)[:, 1:].reshape(blk_shape)
    xc = cb_ref[:, sl][None] + cw_ref[CONV_W - 1:CONV_W, sl][None] * cur
    for s in range(1, CONV_W):
        rolled = pltpu.roll(xb, s, axis=1).reshape(nb, n_blk + 1, SUBLANES, HEAD_A)
        shifted = jnp.where(t_in_blk >= s, rolled[:, 1:].reshape(blk_shape), rolled[:, :-1].reshape(blk_shape))
        xc = xc + cw_ref[CONV_W - 1 - s:CONV_W - s, sl][None] * shifted
    xc = xc.reshape(rows, HEAD_A)
    xcb = xc.astype(BF16)
    r = _sigmoid_tanh(jnp.dot(xcb, wr, preferred_element_type=F32) + br)
    i = _sigmoid_tanh(jnp.dot(xcb, wi, preferred_element_type=F32) + bi)
    log_a = (-C_GATE * r) * softplus
    a = jnp.exp(log_a)
    th = jnp.tanh(log_a)
    one_m_a2 = -2.0 * th / (1.0 - th)
    b = jnp.where(one_m_a2 > 0.0, one_m_a2 * lax.rsqrt(one_m_a2), 0.0) * (i * xc)
    a = a.reshape(blk_shape)
    b = b.reshape(blk_shape)
    d = 1
    while d < SUBLANES:
        a_sh = pltpu.roll(a, d, axis=1)
        b_sh = pltpu.roll(b, d, axis=1)
        keep = t_in_blk >= d
        b = jnp.where(keep, a * b_sh + b, b)
        a = jnp.where(keep, a * a_sh, a)
        d *= 2
    a3 = a.reshape(nb, tt, HEAD_A)
    b3 = b.reshape(nb, tt, HEAD_A)
    h_blocks = []
    for k in range(n_blk):
        ks = slice(k * SUBLANES, (k + 1) * SUBLANES)
        h_k = a3[:, ks, :] * carry + b3[:, ks, :]
        carry = h_k[:, SUBLANES - 1:SUBLANES, :]
        h_blocks.append(h_k)
    h = jnp.concatenate(h_blocks, axis=1) if n_blk > 1 else h_blocks[0]
    return ga * h.reshape(rows, HEAD_A), carry


def _branch_a_kernel(x_ref, wxa_ref, wga_ref, conv0_ref, h0_ref, cw_ref, cb_ref, wr_ref, wi_ref,
                     br_ref, bi_ref, lam_ref, ya_ref, hlast_ref, convnew_ref, h_ref, *xp_refs,
                     n_seq, t_tile, nb, tt, tiles_per_seq):
    i = pl.program_id(1)
    hist = CONV_W - 1
    lo = CONV_PAD - hist
    n_col = HEADS_PER_DOT * HEAD_A
    pair_cols = [slice(pr * n_col, (pr + 1) * n_col) for pr in range(len(xp_refs))]

    @pl.when(i % tiles_per_seq == 0)
    def _():
        for xp_ref, ps in zip(xp_refs, pair_cols):
            xp_ref[:, 0:lo, :] = jnp.zeros((n_seq, lo, n_col), F32)
            xp_ref[:, lo:CONV_PAD, :] = conv0_ref[:, :, ps]
        h_ref[...] = h0_ref[...]

    z = -lam_ref[...]
    softplus = jnp.maximum(z, 0.0) + jnp.log1p(jnp.exp(-jnp.abs(z)))
    x = x_ref[...]

    def project(pr):
        ps = pair_cols[pr]
        xa = jnp.dot(x, wxa_ref[:, ps], preferred_element_type=F32)
        xp_refs[pr][:, CONV_PAD:CONV_PAD + t_tile, :] = xa.reshape(n_seq, t_tile, n_col)
        return _gelu(jnp.dot(x, wga_ref[:, ps], preferred_element_type=F32))

    ga_next = project(0)
    for pr, xp_ref in enumerate(xp_refs):
        ga = ga_next
        if pr + 1 < len(xp_refs):
            ga_next = project(pr + 1)
        for hh in range(HEADS_PER_DOT):
            hd = pr * HEADS_PER_DOT + hh
            xsl = slice(hh * HEAD_A, (hh + 1) * HEAD_A)
            sl = slice(hd * HEAD_A, (hd + 1) * HEAD_A)
            wr = wr_ref[hd].astype(BF16)
            wi = wi_ref[hd].astype(BF16)
            for us in range(n_seq // nb):
                seqs = slice(us * nb, (us + 1) * nb)
                carry = h_ref[seqs, :, sl]
                for ut in range(t_tile // tt):
                    r0 = us * nb * t_tile + ut * tt
                    y, carry = _lru_unit(
                        xp_ref, seqs, ut * tt, xsl, ga[r0:r0 + nb * tt, xsl], carry,
                        cw_ref, cb_ref, wr, wi, br_ref[:, sl], bi_ref[:, sl], softplus[:, sl], sl, nb, tt)
                    ya_ref[r0:r0 + nb * tt, sl] = y.astype(BF16)
                h_ref[seqs, :, sl] = carry

    for xp_ref in xp_refs:
        xp_ref[:, lo:CONV_PAD, :] = xp_ref[:, lo + t_tile:CONV_PAD + t_tile, :]

    @pl.when(i % tiles_per_seq == tiles_per_seq - 1)
    def _():
        hlast_ref[...] = h_ref[...]
        for xp_ref, ps in zip(xp_refs, pair_cols):
            convnew_ref[:, :, ps] = xp_ref[:, lo:CONV_PAD, :]


def _branch_a(grp, tm, tc, nb, tt, xn, w_ag, layer, conv0, h0, cw, cb, wr, wi, br, bi, lam):
    b, t = grp.b, grp.t
    t_tile = min(t, tm)
    n_seq = tm // t_tile
    tiles_per_seq = t // t_tile
    assert (nb == 1 or tt == t_tile) and n_seq % nb == 0 and t_tile % tt == 0
    hist = CONV_W - 1
    n_j = D_MODEL // tc
    seq_blk = lambda j, i: (i // tiles_per_seq, 0, j)
    vec = lambda: pl.BlockSpec((1, tc), lambda j, i: (0, j))
    head_w = lambda: pl.BlockSpec((tc // HEAD_A, HEAD_A, HEAD_A), lambda j, i: (j, 0, 0))
    return pl.pallas_call(
        functools.partial(_branch_a_kernel, n_seq=n_seq, t_tile=t_tile, nb=nb, tt=tt,
                          tiles_per_seq=tiles_per_seq),
        grid=(n_j, grp.m // tm),
        in_specs=[
            pl.BlockSpec((tm, D_MODEL), lambda j, i: (i, 0)),
            pl.BlockSpec((None, D_MODEL, tc), lambda j, i: (layer, 0, j)),
            pl.BlockSpec((None, D_MODEL, tc), lambda j, i: (layer, 0, n_j + j)),
            pl.BlockSpec((n_seq, hist, tc), seq_blk),
            pl.BlockSpec((n_seq, 1, tc), seq_blk),
            pl.BlockSpec((CONV_W, tc), lambda j, i: (0, j)),
            vec(), head_w(), head_w(), vec(), vec(), vec(),
        ],
        out_specs=[
            pl.BlockSpec((tm, tc), lambda j, i: (i, j)),
            pl.BlockSpec((n_seq, 1, tc), seq_blk),
            pl.BlockSpec((n_seq, hist, tc), seq_blk),
        ],
        out_shape=[
            jax.ShapeDtypeStruct((grp.m, D_MODEL), BF16),
            jax.ShapeDtypeStruct((b, 1, D_MODEL), F32),
            jax.ShapeDtypeStruct((b, hist, D_MODEL), F32),
        ],
        scratch_shapes=[pltpu.VMEM((n_seq, 1, tc), F32)]
        + [pltpu.VMEM((n_seq, CONV_PAD + t_tile, HEADS_PER_DOT * HEAD_A), F32)]
        * (tc // (HEADS_PER_DOT * HEAD_A)),
        compiler_params=_params(("arbitrary", "arbitrary"),
                                [((tm, D_MODEL), BF16), ((D_MODEL, tc), BF16), ((D_MODEL, tc), BF16),
                                 ((tm, tc), BF16)] + [((n_seq, SUBLANES, tc), F32)] * 4,
                                [((n_seq, CONV_PAD + t_tile, tc), F32), ((n_seq, SUBLANES, tc), F32),
                                 ((tm, 2 * HEADS_PER_DOT * HEAD_A), F32)]),
        name="branch_a",
    )(xn, w_ag, w_ag, conv0, h0.reshape(b, 1, D_MODEL), cw, cb.reshape(1, D_MODEL),
      wr, wi, br.reshape(1, D_MODEL), bi.reshape(1, D_MODEL), lam.reshape(1, D_MODEL))


def _branch_b_kernel(u_ref, v_ref, gv_ref, ws_ref, bias_ref, yb_ref, wm_ref, *, n_chunks):
    @pl.when(pl.program_id(0) == 0)
    def _():
        tri = (lax.broadcasted_iota(jnp.int32, (CHUNK, CHUNK), 0)
               >= lax.broadcasted_iota(jnp.int32, (CHUNK, CHUNK), 1))
        for g in range(N_GROUPS_B):
            wm_ref[g] = jnp.where(tri, ws_ref[g], 0.0).astype(BF16)

    vn = _rms(v_ref[...].astype(F32), gv_ref[...]).astype(BF16)
    for g in range(N_GROUPS_B):
        cs = slice(g * GROUP_B, (g + 1) * GROUP_B)
        rhs = jnp.concatenate([vn[c * CHUNK:(c + 1) * CHUNK, cs] for c in range(n_chunks)], axis=1)
        mixed = jnp.dot(wm_ref[g], rhs, preferred_element_type=F32)
        for c in range(n_chunks):
            rs = slice(c * CHUNK, (c + 1) * CHUNK)
            mixed_c = mixed[:, c * GROUP_B:(c + 1) * GROUP_B] + bias_ref[:, cs]
            yb_ref[rs, cs] = (u_ref[rs, cs].astype(F32) * mixed_c).astype(BF16)


def _branch_b(grp, gel, gv, ws, bias_full):
    tm = 512
    return pl.pallas_call(
        functools.partial(_branch_b_kernel, n_chunks=tm // CHUNK),
        grid=(grp.m // tm,),
        in_specs=[
            pl.BlockSpec((tm, D_MODEL), lambda i: (i, 0)),
            pl.BlockSpec((tm, D_MODEL), lambda i: (i, 1)),
            pl.BlockSpec((1, D_MODEL), lambda i: (0, 0)),
            pl.BlockSpec((N_GROUPS_B, CHUNK, CHUNK), lambda i: (0, 0, 0)),
            pl.BlockSpec((CHUNK, D_MODEL), lambda i: (0, 0)),
        ],
        out_specs=pl.BlockSpec((tm, D_MODEL), lambda i: (i, 0)),
        out_shape=jax.ShapeDtypeStruct((grp.m, D_MODEL), BF16),
        scratch_shapes=[pltpu.VMEM((N_GROUPS_B, CHUNK, CHUNK), BF16)],
        compiler_params=_params(("arbitrary",),
                                [((tm, D_MODEL), BF16)] * 3 + [((N_GROUPS_B, CHUNK, CHUNK), F32),
                                                                ((CHUNK, D_MODEL), F32)],
                                [((tm, D_MODEL), F32)] * 2),
        name="branch_b",
    )(gel, gel, gv.reshape(1, D_MODEL), ws, bias_full)


def _branch_b_short_kernel(u_ref, v_ref, gv_ref, wc_ref, bias_ref, yb_ref, vn_ref, *, nb, t):
    vn = _rms(v_ref[...].astype(F32), gv_ref[...]).reshape(nb, t, D_MODEL)
    vn_ref[...] = vn
    mixed = bias_ref[...][None]
    for s in range(t):
        mixed = mixed + wc_ref[s][None] * vn[:, s:s + 1, :]
    yb = u_ref[...].astype(F32).reshape(nb, t, D_MODEL) * mixed
    yb_ref[...] = yb.reshape(nb * t, D_MODEL).astype(BF16)


def _branch_b_short(grp, gel, gv, ws, bs):
    b, t = grp.b, grp.t
    nb = 16
    rows = nb * t
    tri = jnp.tril(jnp.ones((t, t), F32))
    wc = jnp.repeat(jnp.transpose(ws[:, :t, :t] * tri, (2, 1, 0)), GROUP_B, axis=-1)
    bias = jnp.repeat(bs[:, :t].T, GROUP_B, axis=-1)
    return pl.pallas_call(
        functools.partial(_branch_b_short_kernel, nb=nb, t=t),
        grid=(b // nb,),
        in_specs=[
            pl.BlockSpec((rows, D_MODEL), lambda i: (i, 0)),
            pl.BlockSpec((rows, D_MODEL), lambda i: (i, 1)),
            pl.BlockSpec((1, D_MODEL), lambda i: (0, 0)),
            pl.BlockSpec((t, t, D_MODEL), lambda i: (0, 0, 0)),
            pl.BlockSpec((t, D_MODEL), lambda i: (0, 0)),
        ],
        out_specs=[
            pl.BlockSpec((rows, D_MODEL), lambda i: (i, 0)),
            pl.BlockSpec((nb, t, D_MODEL), lambda i: (i, 0, 0)),
        ],
        out_shape=[
            jax.ShapeDtypeStruct((grp.m, D_MODEL), BF16),
            jax.ShapeDtypeStruct((b, t, D_MODEL), F32),
        ],
        compiler_params=_params(("arbitrary",),
                                [((rows, D_MODEL), BF16)] * 3 + [((rows, D_MODEL), F32),
                                                                  ((t, t, D_MODEL), F32)],
                                [((rows, D_MODEL), F32)] * 3),
        name="branch_b_short",
    )(gel, gel, gv.reshape(1, D_MODEL), wc, bias)


def _merge_kernel(ya_ref, yb_ref, sa_ref, sb_ref, wpa_ref, wpb_ref, o_ref, wa_ref, wb_ref):
    @pl.when(pl.program_id(1) == 0)
    def _():
        wa_ref[...] = wpa_ref[...].astype(BF16)
        wb_ref[...] = wpb_ref[...].astype(BF16)

    pa = jnp.dot(ya_ref[...], wa_ref[...], preferred_element_type=F32)
    pb = jnp.dot(yb_ref[...], wb_ref[...], preferred_element_type=F32)
    o_ref[...] = (sa_ref[...].astype(F32) * pa + sb_ref[...].astype(F32) * pb).astype(BF16)


def _merge(grp, ya, yb, sig, wpa, wpb, layer):
    tm, tn = 1024, 512
    nj = D_MODEL // tn
    return pl.pallas_call(
        _merge_kernel,
        grid=(nj, grp.m // tm),
        in_specs=[
            pl.BlockSpec((tm, D_MODEL), lambda j, i: (i, 0)),
            pl.BlockSpec((tm, D_MODEL), lambda j, i: (i, 0)),
            pl.BlockSpec((tm, tn), lambda j, i: (i, j)),
            pl.BlockSpec((tm, tn), lambda j, i: (i, nj + j)),
            pl.BlockSpec((None, D_MODEL, tn), lambda j, i: (layer, 0, j)),
            pl.BlockSpec((None, D_MODEL, tn), lambda j, i: (layer, 0, j)),
        ],
        out_specs=pl.BlockSpec((tm, tn), lambda j, i: (i, j)),
        out_shape=jax.ShapeDtypeStruct((grp.m, D_MODEL), BF16),
        scratch_shapes=[pltpu.VMEM((D_MODEL, tn), BF16), pltpu.VMEM((D_MODEL, tn), BF16)],
        compiler_params=_params(("arbitrary", "arbitrary"),
                                [((tm, D_MODEL), BF16)] * 2 + [((tm, tn), BF16)] * 3
                                + [((D_MODEL, tn), F32)] * 2,
                                [((D_MODEL, tn), BF16)] * 2 + [((tm, tn), F32)] * 2),
        name="merge",
    )(ya, yb, sig, sig, wpa, wpb)


def _post_math(y, x, gpost, gt, nxt):
    x1 = x + gt * _rms(y, gpost)
    if nxt is None:
        return x1, None
    gpre, one_plus_sc, sh = nxt
    return x1, _rms(x1, gpre) * one_plus_sc + sh


def _next_mods(gpre_ref, sc_ref, sh_ref, rows):
    return gpre_ref[...], _mod_rows(sc_ref, rows, 1.0), _mod_rows(sh_ref, rows)


def _oproj_post_kernel(m_ref, wo_ref, x_ref, gpost_ref, gt_ref, gpre_ref, sc_ref, sh_ref,
                       xo_ref, xn_ref):
    y = jnp.dot(m_ref[...], wo_ref[...], preferred_element_type=F32)
    rows = x_ref.shape[0]
    x1, xn = _post_math(y, x_ref[...], gpost_ref[...], _mod_rows(gt_ref, rows),
                        _next_mods(gpre_ref, sc_ref, sh_ref, rows))
    xo_ref[...] = x1
    xn_ref[...] = xn.astype(xn_ref.dtype)


def _oproj_post(grp, m, wo, layer, x, gpost, gt, gpre, sc, sh, xn_dtype):
    tm = 512
    vec = lambda: pl.BlockSpec((1, D_MODEL), lambda i: (0, 0))
    row = lambda: pl.BlockSpec((tm, D_MODEL), lambda i: (i, 0))
    return pl.pallas_call(
        _oproj_post_kernel,
        grid=(grp.m // tm,),
        in_specs=[
            row(),
            pl.BlockSpec((None, D_MODEL, D_MODEL), lambda i: (layer, 0, 0),
                         pipeline_mode=pl.Buffered(1)),
            row(), vec(), grp.mod_spec(tm), vec(), grp.mod_spec(tm), grp.mod_spec(tm),
        ],
        out_specs=[row(), row()],
        out_shape=[
            jax.ShapeDtypeStruct((grp.m, D_MODEL), F32),
            jax.ShapeDtypeStruct((grp.m, D_MODEL), xn_dtype),
        ],
        compiler_params=_params(("arbitrary",),
                                [((tm, D_MODEL), F32)] * 5,
                                [((D_MODEL, D_MODEL), BF16), ((tm, D_MODEL), F32)]),
        name="oproj_post",
    )(m, wo, x, gpost.reshape(1, D_MODEL), gt, gpre.reshape(1, D_MODEL), sc, sh)


def _post_kernel(y_ref, x_ref, gpost_ref, gt_ref, *rest, with_next):
    rows = x_ref.shape[0]
    if with_next:
        gpre_ref, sc_ref, sh_ref, xo_ref, xn_ref = rest
        nxt = _next_mods(gpre_ref, sc_ref, sh_ref, rows)
    else:
        (xo_ref,) = rest
        nxt = None
    x1, xn = _post_math(y_ref[...], x_ref[...], gpost_ref[...], _mod_rows(gt_ref, rows), nxt)
    xo_ref[...] = x1
    if with_next:
        xn_ref[...] = xn.astype(xn_ref.dtype)


def _post_specs(grp, tm, x, gpost, gt, nxt, imap):
    vec = lambda: pl.BlockSpec((1, D_MODEL), lambda *ids: (0, 0))
    row = lambda: pl.BlockSpec((tm, D_MODEL), imap)
    in_specs = [row(), vec(), grp.mod_spec(tm)]
    args = [x, gpost.reshape(1, D_MODEL), gt]
    out_specs = [row()]
    out_shape = [jax.ShapeDtypeStruct((grp.m, D_MODEL), F32)]
    if nxt is not None:
        gpre, sc, sh = nxt
        in_specs += [vec(), grp.mod_spec(tm), grp.mod_spec(tm)]
        args += [gpre.reshape(1, D_MODEL), sc, sh]
        out_specs.append(row())
        out_shape.append(jax.ShapeDtypeStruct((grp.m, D_MODEL), BF16))
    return in_specs, args, out_specs, out_shape


def _post(grp, y, x, gpost, gt, nxt):
    tm = 256
    imap = lambda i: (i, 0)
    in_specs, args, out_specs, out_shape = _post_specs(grp, tm, x, gpost, gt, nxt, imap)
    out = pl.pallas_call(
        functools.partial(_post_kernel, with_next=nxt is not None),
        grid=(grp.m // tm,),
        in_specs=[pl.BlockSpec((tm, D_MODEL), imap)] + in_specs,
        out_specs=out_specs,
        out_shape=out_shape,
        compiler_params=_params(("arbitrary",), [((tm, D_MODEL), F32)] * 8),
        name="post",
    )(y, *args)
    return (out[0], out[1]) if nxt is not None else (out[0], None)


def _swiglu_tile(x, wg_ref, wu_ref, wd_ref):
    g = jnp.dot(x, wg_ref[...], preferred_element_type=F32)
    u = jnp.dot(x, wu_ref[...], preferred_element_type=F32)
    h = _silu(g) * u
    return jnp.dot(h.astype(BF16), wd_ref[...], preferred_element_type=F32)


def _swiglu_kernel(x_ref, wg_ref, wu_ref, wd_ref, o_ref):
    @pl.when(pl.program_id(1) == 0)
    def _():
        o_ref[...] = jnp.zeros_like(o_ref)

    o_ref[...] += _swiglu_tile(x_ref[...], wg_ref, wu_ref, wd_ref)


def _swiglu(grp, x, wg, wu, wd):
    f = wg.shape[-1]
    tm, tf = 1024, 512
    return pl.pallas_call(
        _swiglu_kernel,
        grid=(grp.m // tm, f // tf),
        in_specs=[
            pl.BlockSpec((tm, D_MODEL), lambda i, j: (i, 0)),
            pl.BlockSpec((D_MODEL, tf), lambda i, j: (0, j)),
            pl.BlockSpec((D_MODEL, tf), lambda i, j: (0, j)),
            pl.BlockSpec((tf, D_MODEL), lambda i, j: (j, 0)),
        ],
        out_specs=pl.BlockSpec((tm, D_MODEL), lambda i, j: (i, 0)),
        out_shape=jax.ShapeDtypeStruct((grp.m, D_MODEL), F32),
        compiler_params=_params(("arbitrary", "arbitrary"),
                                [((tm, D_MODEL), BF16), ((tm, D_MODEL), F32)]
                                + [((D_MODEL, tf), BF16)] * 3,
                                [((tm, tf), F32)] * 4),
        name="swiglu_dense",
    )(x, wg, wu, wd)


META_E1, META_E2, META_R1, META_R2, META_W1, META_W2 = range(6)
DMA_UNROLL = 8


def _split_bf16(v):
    hi = v.astype(BF16)
    return hi, (v - hi.astype(F32)).astype(BF16)


def _router_kernel(x_ref, rw_ref, rb_ref, meta_ref, cnt_ref, tri_ref):
    tm = x_ref.shape[0]

    @pl.when(pl.program_id(0) == 0)
    def _():
        cnt_ref[...] = jnp.zeros_like(cnt_ref)
        tri_ref[...] = (lax.broadcasted_iota(jnp.int32, (tm, tm), 0)
                        >= lax.broadcasted_iota(jnp.int32, (tm, tm), 1)).astype(BF16)

    xh, xl = _split_bf16(x_ref[...])
    wh, wl = _split_bf16(rw_ref[...])
    logits = (jnp.dot(xh, wh, preferred_element_type=F32) + jnp.dot(xl, wh, preferred_element_type=F32)
              + jnp.dot(xh, wl, preferred_element_type=F32)) + rb_ref[...]
    lane = lax.broadcasted_iota(jnp.int32, logits.shape, 1)
    logits = jnp.where(lane < N_EXPERTS, logits, -jnp.inf)
    ex = jnp.exp(logits - jnp.max(logits, axis=-1, keepdims=True))
    p = ex / jnp.sum(ex, axis=-1, keepdims=True)
    p1 = jnp.max(p, axis=-1, keepdims=True)
    i1 = jnp.min(jnp.where(p == p1, lane, LANES), axis=-1, keepdims=True)
    rest = jnp.where(lane == i1, -1.0, p)
    p2 = jnp.max(rest, axis=-1, keepdims=True)
    i2 = jnp.min(jnp.where(rest == p2, lane, LANES), axis=-1, keepdims=True)
    denom = p1 + p2
    onehot = jnp.where(lane == i1, 1.0, 0.0) + jnp.where(lane == i2, 1.0, 0.0)
    cum = jnp.dot(tri_ref[...], onehot.astype(BF16), preferred_element_type=F32) + cnt_ref[...]
    cnt_ref[...] = cum[tm - 1:tm, :]
    r1 = jnp.sum(jnp.where(lane == i1, cum, 0.0), axis=-1, keepdims=True) - 1.0
    r2 = jnp.sum(jnp.where(lane == i2, cum, 0.0), axis=-1, keepdims=True) - 1.0
    meta = jnp.zeros_like(logits)
    for k, v in ((META_E1, i1.astype(F32)), (META_E2, i2.astype(F32)), (META_R1, r1), (META_R2, r2),
                 (META_W1, p1 / denom), (META_W2, p2 / denom)):
        meta = jnp.where(lane == k, v, meta)
    meta_ref[...] = meta


def _router(grp, x, rw_pad, rb_pad):
    tm = 512
    return pl.pallas_call(
        _router_kernel,
        grid=(grp.m // tm,),
        in_specs=[
            pl.BlockSpec((tm, D_MODEL), lambda i: (i, 0)),
            pl.BlockSpec((D_MODEL, LANES), lambda i: (0, 0)),
            pl.BlockSpec((1, LANES), lambda i: (0, 0)),
        ],
        out_specs=[
            pl.BlockSpec((tm, LANES), lambda i: (i, 0)),
            pl.BlockSpec((1, LANES), lambda i: (0, 0)),
        ],
        out_shape=[
            jax.ShapeDtypeStruct((grp.m, LANES), F32),
            jax.ShapeDtypeStruct((1, LANES), F32),
        ],
        scratch_shapes=[pltpu.VMEM((tm, tm), BF16)],
        compiler_params=_params(("arbitrary",),
                                [((tm, D_MODEL), F32), ((D_MODEL, LANES), F32), ((tm, LANES), F32)],
                                [((tm, tm), BF16), ((tm, D_MODEL), F32)]),
        name="router",
    )(x, rw_pad, rb_pad)


def _route_plan(metas, counts, tg, n_tiles):
    cnts = [c[0, :N_EXPERTS].astype(jnp.int32) for c in counts]
    total = sum(cnts)
    padded = (total + tg - 1) // tg * tg
    ends = jnp.cumsum(padded)
    offs = ends - padded
    slots = []
    for meta, before in zip(metas, [sum(cnts[:g], jnp.zeros_like(total)) for g in range(len(cnts))]):
        idx = meta[:, :META_W1].astype(jnp.int32)
        start = offs + before
        slots.append((start[idx[:, META_E1]] + idx[:, META_R1], start[idx[:, META_E2]] + idx[:, META_R2]))
    n_used = ends[-1] // tg
    tile = jnp.arange(n_tiles, dtype=jnp.int32)
    tile_c = jnp.minimum(tile, n_used - 1)
    tile_expert = jnp.sum(tile_c[:, None] * tg >= ends[None, :], axis=1).astype(jnp.int32)
    return slots, tile_expert, n_used.reshape(1).astype(jnp.int32)


def _dispatch_kernel(pos1_ref, pos2_ref, x_ref, xs_in_ref, xs_ref, sem):
    del xs_in_ref
    tm = x_ref.shape[0]
    base = pl.program_id(0) * tm

    def row_copy(t, slot):
        return pltpu.make_async_copy(x_ref.at[pl.ds(t, 1)], xs_ref.at[pl.ds(slot, 1)], sem)

    def start(t, c):
        row_copy(t, pos1_ref[base + t]).start()
        row_copy(t, pos2_ref[base + t]).start()
        return c

    def wait(t, c):
        row_copy(t, 0).wait()
        row_copy(t, 0).wait()
        return c

    lax.fori_loop(0, tm, start, 0, unroll=DMA_UNROLL)
    lax.fori_loop(0, tm, wait, 0, unroll=DMA_UNROLL)


def _dispatch(grp, x, pos1, pos2, xs):
    tm = 1024
    return pl.pallas_call(
        _dispatch_kernel,
        grid_spec=pltpu.PrefetchScalarGridSpec(
            num_scalar_prefetch=2,
            grid=(grp.m // tm,),
            in_specs=[
                pl.BlockSpec((tm, D_MODEL), lambda i, p1, p2: (i, 0)),
                pl.BlockSpec(memory_space=pl.ANY),
            ],
            out_specs=pl.BlockSpec(memory_space=pl.ANY),
            scratch_shapes=[pltpu.SemaphoreType.DMA(())],
        ),
        out_shape=jax.ShapeDtypeStruct(xs.shape, F32),
        input_output_aliases={3: 0},
        compiler_params=_params(("arbitrary",), [((tm, D_MODEL), F32)]),
        name="moe_dispatch",
    )(pos1, pos2, x, xs)


def _experts_kernel(te_ref, nu_ref, xs_ref, wg_ref, wu_ref, wd_ref, o_ref):
    del te_ref
    r, j = pl.program_id(0), pl.program_id(1)

    @pl.when(j == 0)
    def _():
        o_ref[...] = jnp.zeros_like(o_ref)

    @pl.when(r < nu_ref[0])
    def _():
        o_ref[...] += _swiglu_tile(xs_ref[...].astype(BF16), wg_ref, wu_ref, wd_ref)


def _experts(xs, tile_expert, n_used, wg, wu, wd, tg):
    n_rows = xs.shape[0]
    f = wg.shape[-1]
    tf = 512
    n_j = f // tf

    def used(r, nu):
        return r < nu[0]

    def x_map(r, j, te, nu):
        return (jnp.minimum(r, nu[0] - 1), 0)

    def up_map(r, j, te, nu):
        return (te[r], 0, jnp.where(used(r, nu), j, n_j - 1))

    def down_map(r, j, te, nu):
        return (te[r], jnp.where(used(r, nu), j, n_j - 1), 0)

    return pl.pallas_call(
        _experts_kernel,
        grid_spec=pltpu.PrefetchScalarGridSpec(
            num_scalar_prefetch=2,
            grid=(n_rows // tg, n_j),
            in_specs=[
                pl.BlockSpec((tg, D_MODEL), x_map),
                pl.BlockSpec((None, D_MODEL, tf), up_map),
                pl.BlockSpec((None, D_MODEL, tf), up_map),
                pl.BlockSpec((None, tf, D_MODEL), down_map),
            ],
            out_specs=pl.BlockSpec((tg, D_MODEL), lambda r, j, te, nu: (r, 0)),
        ),
        out_shape=jax.ShapeDtypeStruct((n_rows, D_MODEL), F32),
        compiler_params=_params(("arbitrary", "arbitrary"),
                                [((tg, D_MODEL), F32)] * 2 + [((D_MODEL, tf), BF16)] * 3,
                                [((tg, tf), F32)] * 4 + [((tg, D_MODEL), BF16)]),
        name="moe_experts",
    )(tile_expert, n_used, xs, wg, wu, wd)


def _combine_post_kernel(pos1_ref, pos2_ref, ys_ref, meta_ref, x_ref, gpost_ref, gt_ref, *rest,
                         with_next):
    *rest, buf1_ref, buf2_ref, sem = rest
    tm = x_ref.shape[0]
    step, n_steps = pl.program_id(0), pl.num_programs(0)
    cur = step % 2

    def row_copy(row, buf_ref, half, t):
        return pltpu.make_async_copy(ys_ref.at[pl.ds(row, 1)], buf_ref.at[half, pl.ds(t, 1)],
                                     sem.at[half])

    def gather(tile, half):
        def start(t, c):
            row_copy(pos1_ref[tile * tm + t], buf1_ref, half, t).start()
            row_copy(pos2_ref[tile * tm + t], buf2_ref, half, t).start()
            return c
        lax.fori_loop(0, tm, start, 0, unroll=DMA_UNROLL)

    @pl.when(step == 0)
    def _():
        gather(0, 0)

    @pl.when(step + 1 < n_steps)
    def _():
        gather(step + 1, 1 - cur)

    def wait(t, c):
        row_copy(0, buf1_ref, cur, t).wait()
        row_copy(0, buf2_ref, cur, t).wait()
        return c

    lax.fori_loop(0, tm, wait, 0, unroll=DMA_UNROLL)
    meta = meta_ref[...]
    y = meta[:, META_W1:META_W1 + 1] * buf1_ref[cur] + meta[:, META_W2:META_W2 + 1] * buf2_ref[cur]
    if with_next:
        gpre_ref, sc_ref, sh_ref, xo_ref, xn_ref = rest
        nxt = _next_mods(gpre_ref, sc_ref, sh_ref, tm)
    else:
        (xo_ref,) = rest
        nxt = None
    x1, xn = _post_math(y, x_ref[...], gpost_ref[...], _mod_rows(gt_ref, tm), nxt)
    xo_ref[...] = x1
    if with_next:
        xn_ref[...] = xn.astype(xn_ref.dtype)


def _combine_post(grp, ys, pos1, pos2, meta, x, gpost, gt, nxt):
    tm = 256
    imap = lambda i, p1, p2: (i, 0)
    in_specs, args, out_specs, out_shape = _post_specs(grp, tm, x, gpost, gt, nxt, imap)
    out = pl.pallas_call(
        functools.partial(_combine_post_kernel, with_next=nxt is not None),
        grid_spec=pltpu.PrefetchScalarGridSpec(
            num_scalar_prefetch=2,
            grid=(grp.m // tm,),
            in_specs=[pl.BlockSpec(memory_space=pl.ANY), pl.BlockSpec((tm, LANES), imap)] + in_specs,
            out_specs=out_specs,
            scratch_shapes=[pltpu.VMEM((2, tm, D_MODEL), F32), pltpu.VMEM((2, tm, D_MODEL), F32),
                            pltpu.SemaphoreType.DMA((2,))],
        ),
        out_shape=out_shape,
        compiler_params=_params(("arbitrary",), [((tm, D_MODEL), F32)] * 7,
                                [((tm, D_MODEL), F32)] * 4),
        name="moe_combine_post",
    )(pos1, pos2, ys, meta, *args)
    return (out[0], out[1]) if nxt is not None else (out[0], None)


def _moe_post(grps, xns, p, j, xs_res, gpost, gts, nxts):
    tg = 512
    n_tiles = 2 * sum(g.m for g in grps) // tg + N_EXPERTS
    rw_pad = jnp.pad(p['router_w'][j], ((0, 0), (0, LANES - N_EXPERTS)))
    rb_pad = jnp.pad(p['router_b'][j], (0, LANES - N_EXPERTS)).reshape(1, LANES)
    routed = [_router(g, xn, rw_pad, rb_pad) for g, xn in zip(grps, xns)]
    metas, counts = zip(*routed)
    slots, tile_expert, n_used = _route_plan(metas, counts, tg, n_tiles)
    xs = jnp.zeros((n_tiles * tg, D_MODEL), F32)
    for g, xn, (pos1, pos2) in zip(grps, xns, slots):
        xs = _dispatch(g, xn, pos1, pos2, xs)
    ys = _experts(xs, tile_expert, n_used, p['moe_wg_bf'][j], p['moe_wu_bf'][j], p['moe_wd_bf'][j], tg)
    return [_combine_post(g, ys, pos1, pos2, meta, x, gpost, gt, nxt)
            for g, (pos1, pos2), meta, x, gt, nxt in zip(grps, slots, metas, xs_res, gts, nxts)]


def _mixer(grp, l, x, xn, mods, h0, conv0, p, a_tiles, xn_dtype):
    _, _, gt1, sh2, sc2, _ = mods
    w_in = p['w_in']
    ya, h_last, conv_new = _branch_a(
        grp, *a_tiles, xn, p['w_ag_bf'], l, conv0, h0, p['conv_w'][l], p['conv_b'][l],
        p['w_r'][l], p['w_i'][l], p['b_r'][l], p['b_i'][l], p['lru_lambda'][l])
    gel = _mm_act(grp, xn, w_in, l, 2 * D_MODEL, 2 * D_MODEL, _gelu, BF16, "in_proj_gelu")
    sig = _mm_act(grp, xn, w_in, l, 4 * D_MODEL, 2 * D_MODEL, _sigmoid, BF16, "in_proj_sigmoid")
    if grp.t >= CHUNK:
        bias_full = jnp.repeat(p['b_s'][l].T, GROUP_B, axis=-1)
        yb = _branch_b(grp, gel, p['g_v'][l], p['w_s'][l], bias_full)
        vn = None
    else:
        yb, vn = _branch_b_short(grp, gel, p['g_v'][l], p['w_s'][l], p['b_s'][l])
    m = _merge(grp, ya, yb, sig, p['w_pa'], p['w_pb'], l)
    x, xn2 = _oproj_post(grp, m, p['w_o_bf'], l, x, p['g_post_mix'][l], gt1,
                         p['g_pre_ffn'][l], sc2, sh2, xn_dtype)
    return x, xn2, h_last.reshape(grp.b, D_MODEL), conv_new, vn


def _trunk(grps, xs, mods, h0s, conv0s, p, tiles):
    depth = p['w_in'].shape[0]
    n_g = len(grps)
    xns = [_prenorm(g, x, p['g_pre_mix'][0], mods[i][0][1], mods[i][0][0])
           for i, (g, x) in enumerate(zip(grps, xs))]
    states = [([], [], []) for _ in grps]
    for l in range(depth):
        dense = l % 2 == 0
        j = l // 2
        xn2s = []
        for i, g in enumerate(grps):
            xs[i], xn2, h_last, conv_new, vn = _mixer(
                g, l, xs[i], xns[i], mods[i][l], h0s[i][l], conv0s[i][l], p, tiles[i],
                BF16 if dense else F32)
            xn2s.append(xn2)
            for acc, v in zip(states[i], (h_last, conv_new, vn)):
                acc.append(v)
        if l + 1 < depth:
            nxts = [(p['g_pre_mix'][l + 1], mods[i][l + 1][1], mods[i][l + 1][0]) for i in range(n_g)]
        else:
            nxts = [None] * n_g
        gt2s = [mods[i][l][5] for i in range(n_g)]
        gpost = p['g_post_ffn'][l]
        if dense:
            outs = []
            for i, g in enumerate(grps):
                y = _swiglu(g, xn2s[i], p['ffn_wg_bf'][j], p['ffn_wu_bf'][j], p['ffn_wd_bf'][j])
                outs.append(_post(g, y, xs[i], gpost, gt2s[i], nxts[i]))
        else:
            outs = _moe_post(grps, xn2s, p, j, xs, gpost, gt2s, nxts)
        xs = [o[0] for o in outs]
        xns = [o[1] for o in outs]
    return xs, states


def kernel(x_prompt, x_sample, c_prompt, c_sample, state_lru_h, state_lru_conv, w_ada, b_ada, g_pre_mix, g_post_mix, g_pre_ffn, g_post_ffn, w_in, conv_w, conv_b, w_r, b_r, w_i, b_i, lru_lambda, g_v, w_s, b_s, w_pa, w_pb, w_o, ffn_wg, ffn_wu, ffn_wd, router_w, router_b, moe_wg, moe_wu, moe_wd):
    p = dict(g_pre_mix=g_pre_mix, g_post_mix=g_post_mix, g_pre_ffn=g_pre_ffn, g_post_ffn=g_post_ffn,
             w_in=w_in, conv_w=conv_w, conv_b=conv_b, w_r=w_r, b_r=b_r, w_i=w_i, b_i=b_i,
             lru_lambda=lru_lambda, g_v=g_v, w_s=w_s, b_s=b_s, w_pa=w_pa, w_pb=w_pb,
             router_w=router_w, router_b=router_b, w_o_bf=w_o.astype(BF16),
             w_ag_bf=w_in[:, :, :2 * D_MODEL].astype(BF16),
             ffn_wg_bf=ffn_wg.astype(BF16), ffn_wu_bf=ffn_wu.astype(BF16), ffn_wd_bf=ffn_wd.astype(BF16),
             moe_wg_bf=moe_wg.astype(BF16), moe_wu_bf=moe_wu.astype(BF16), moe_wd_bf=moe_wd.astype(BF16))
    depth = w_in.shape[0]
    bp, tp, _ = x_prompt.shape
    bs, ts, _ = x_sample.shape
    assert tp % CHUNK == 0 and ts < CHUNK and ts % SUBLANES == 0

    n_c = bp + bs
    r_pad = -n_c % 16
    c_all = jnp.concatenate([c_sample, c_prompt, jnp.zeros((r_pad, D_MODEL), F32)], axis=0)
    mod = _ada(c_all, w_ada, b_ada)
    mods_s = [[mod[l, k, :bs] for k in range(6)] for l in range(depth)]
    mods_p = [[mod[l, k, bs:n_c].reshape(bp, 1, D_MODEL) for k in range(6)] for l in range(depth)]

    grp_p = _Group(bp, tp, short=False)
    grp_s = _Group(bs, ts, short=True)
    zeros_h = jnp.zeros((depth, bp, D_MODEL), F32)
    zeros_conv = jnp.zeros((depth, bp, CONV_W - 1, D_MODEL), F32)

    (y_p, y_s), (st_p, st_s) = _trunk(
        [grp_p, grp_s],
        [x_prompt.reshape(bp * tp, D_MODEL), x_sample.reshape(bs * ts, D_MODEL)],
        [mods_p, mods_s], [zeros_h, state_lru_h], [zeros_conv, state_lru_conv], p,
        tiles=[(1024, 1024, 1, 256), (512, 1024, 16, ts)])
    return (y_p.reshape(bp, tp, D_MODEL), y_s.reshape(bs, ts, D_MODEL),
            jnp.stack(st_p[0]), jnp.stack(st_p[1]), jnp.stack(st_s[0]), jnp.stack(st_s[1]),
            jnp.stack(st_s[2]))
```

```python
import functools

import jax
import jax.numpy as jnp
from jax import lax
from jax.experimental import pallas as pl
from jax.experimental.pallas import tpu as pltpu

F32 = jnp.float32
BF16 = jnp.bfloat16

D_MODEL = 2048
N_HEADS_A = 16
HEAD_A = D_MODEL // N_HEADS_A
CONV_W = 4
C_GATE = 8.0
N_GROUPS_B = 16
GROUP_B = D_MODEL // N_GROUPS_B
CHUNK = 128
N_EXPERTS = 8
EPS = 1e-6

LANES = 128
SUBLANES = 8
VMEM_CAP = 58 * 2**20
CONV_PAD = SUBLANES
CAST_SLAB_BYTES = 4 * 2**20


def _nbytes(shape, dtype):
    n = 1
    for s in shape:
        n *= s
    return n * jnp.dtype(dtype).itemsize


def _params(sem, blocks, scratch=()):
    need = 2 * sum(_nbytes(s, d) for s, d in blocks) + sum(_nbytes(s, d) for s, d in scratch)
    limit = min(VMEM_CAP, need + 16 * 2**20)
    return pltpu.CompilerParams(dimension_semantics=sem, vmem_limit_bytes=limit)


def _rms(x, g):
    return x * lax.rsqrt(jnp.mean(x * x, axis=-1, keepdims=True) + EPS) * g


def _sigmoid(x):
    return 1.0 / (1.0 + jnp.exp(-x))


def _sigmoid_tanh(x):
    return 0.5 * jnp.tanh(0.5 * x) + 0.5


def _silu(x):
    return x * _sigmoid(x)


def _gelu(x):
    return jax.nn.gelu(x)


class _Group:
    def __init__(self, b, t, short):
        self.b, self.t, self.m = b, t, b * t
        self.short = short

    def mod_spec(self, tm):
        if self.short:
            return pl.BlockSpec((tm // self.t, D_MODEL), lambda *ids: (ids[0], 0))
        t = self.t
        return pl.BlockSpec((None, 1, D_MODEL), lambda *ids: ((ids[0] * tm) // t, 0, 0))


def _mod_rows(ref, rows, offset=None):
    n = ref.shape[0]
    vals = ref[...] if offset is None else ref[...] + offset
    if n == 1:
        return vals
    rep = rows // n
    return jnp.concatenate([jnp.broadcast_to(vals[r:r + 1, :], (rep, D_MODEL)) for r in range(n)], axis=0)


def _ada_kernel(c_ref, w_ref, b_ref, o_ref):
    s = _silu(c_ref[...]).astype(BF16)
    o_ref[...] = jnp.dot(s, w_ref[...].astype(BF16), preferred_element_type=F32) + b_ref[...]


def _ada(c, w_ada, b_ada):
    depth, _, n = w_ada.shape
    r = c.shape[0]
    tn = 1024
    per_item = D_MODEL // tn
    return pl.pallas_call(
        _ada_kernel,
        grid=(depth, n // tn),
        in_specs=[
            pl.BlockSpec((r, D_MODEL), lambda l, j: (0, 0)),
            pl.BlockSpec((None, D_MODEL, tn), lambda l, j: (l, 0, j)),
            pl.BlockSpec((None, 1, tn), lambda l, j: (l, 0, j)),
        ],
        out_specs=pl.BlockSpec((None, None, r, tn), lambda l, j: (l, j // per_item, 0, j % per_item)),
        out_shape=jax.ShapeDtypeStruct((depth, n // D_MODEL, r, D_MODEL), F32),
        compiler_params=_params(("arbitrary", "arbitrary"),
                                [((r, D_MODEL), F32), ((D_MODEL, tn), F32), ((r, tn), F32)],
                                [((D_MODEL, tn), BF16)]),
        name="ada_mod",
    )(c, w_ada, b_ada.reshape(depth, 1, n))


def _prenorm_kernel(x_ref, g_ref, sc_ref, sh_ref, o_ref):
    rows = x_ref.shape[0]
    o_ref[...] = (_rms(x_ref[...], g_ref[...]) * _mod_rows(sc_ref, rows, 1.0)
                  + _mod_rows(sh_ref, rows)).astype(BF16)


def _prenorm(grp, x, g, sc, sh):
    tm = 256 if grp.short else 512
    return pl.pallas_call(
        _prenorm_kernel,
        grid=(grp.m // tm,),
        in_specs=[
            pl.BlockSpec((tm, D_MODEL), lambda i: (i, 0)),
            pl.BlockSpec((1, D_MODEL), lambda i: (0, 0)),
            grp.mod_spec(tm), grp.mod_spec(tm),
        ],
        out_specs=pl.BlockSpec((tm, D_MODEL), lambda i: (i, 0)),
        out_shape=jax.ShapeDtypeStruct((grp.m, D_MODEL), BF16),
        compiler_params=_params(("arbitrary",), [((tm, D_MODEL), F32)] * 4),
        name="prenorm",
    )(x, g.reshape(1, D_MODEL), sc, sh)


class _CastJobs:
    def __init__(self, arrays):
        self.shapes = {k: v.shape for k, v in arrays.items()}
        self.pending = [(k, v.reshape(-1, v.shape[-1])) for k, v in arrays.items()]
        self.done = {}

    def take(self, steps):
        for n, (name, src) in enumerate(self.pending):
            rows = src.shape[0] // steps
            if src.shape[0] % (steps * 2 * SUBLANES) == 0 and _nbytes((rows, src.shape[1]), F32) <= CAST_SLAB_BYTES:
                return self.pending.pop(n)
        return None

    def put(self, name, out):
        self.done[name] = out.reshape(self.shapes[name])

    def result(self, name):
        for n, (k, src) in enumerate(self.pending):
            if k == name:
                self.pending.pop(n)
                self.put(name, src.astype(BF16))
                break
        return self.done[name]


def _mm_act_kernel(x_ref, w_ref, *rest, act, with_cast):
    if with_cast:
        src_ref, o_ref, dst_ref, wbf_ref = rest
        dst_ref[...] = src_ref[...].astype(BF16)
    else:
        o_ref, wbf_ref = rest

    @pl.when(pl.program_id(1) == 0)
    def _():
        wbf_ref[...] = w_ref[...].astype(BF16)

    y = jnp.dot(x_ref[...], wbf_ref[...], preferred_element_type=F32)
    o_ref[...] = act(y).astype(o_ref.dtype)


def _mm_act(grp, x, w, layer, col0, ncols, act, out_dtype, name, jobs=None):
    tm, tn = 1024, 1024
    c0 = col0 // tn
    n_j, n_i = ncols // tn, grp.m // tm
    in_specs = [
        pl.BlockSpec((tm, D_MODEL), lambda j, i: (i, 0)),
        pl.BlockSpec((None, D_MODEL, tn), lambda j, i: (layer, 0, c0 + j)),
    ]
    out_specs = [pl.BlockSpec((tm, tn), lambda j, i: (i, j))]
    out_shape = [jax.ShapeDtypeStruct((grp.m, ncols), out_dtype)]
    args = [x, w]
    blocks = [((tm, D_MODEL), BF16), ((D_MODEL, tn), F32), ((tm, tn), F32)]
    job = jobs.take(n_j * n_i) if jobs is not None else None
    if job is not None:
        src = job[1]
        slab = (src.shape[0] // (n_j * n_i), src.shape[1])
        slab_spec = pl.BlockSpec(slab, lambda j, i: (j * n_i + i, 0))
        in_specs.append(slab_spec)
        out_specs.append(slab_spec)
        out_shape.append(jax.ShapeDtypeStruct(src.shape, BF16))
        args.append(src)
        blocks += [(slab, F32), (slab, BF16)]
    out = pl.pallas_call(
        functools.partial(_mm_act_kernel, act=act, with_cast=job is not None),
        grid=(n_j, n_i),
        in_specs=in_specs,
        out_specs=out_specs,
        out_shape=out_shape,
        scratch_shapes=[pltpu.VMEM((D_MODEL, tn), BF16)],
        compiler_params=_params(("arbitrary", "arbitrary"), blocks,
                                [((D_MODEL, tn), BF16), ((tm, tn), F32)]),
        name=name,
    )(*args)
    if job is not None:
        jobs.put(job[0], out[1])
    return out[0]


HEADS_PER_DOT = 2


def _lru_unit(xp_ref, seqs, t0, xsl, ga, carry, cw_ref, cb_ref, wr, wi, br, bi, softplus, sl, nb, tt):
    rows = nb * tt
    n_blk = tt // SUBLANES
    blk_shape = (nb * n_blk, SUBLANES, HEAD_A)
    t_in_blk = lax.broadcasted_iota(jnp.int32, blk_shape, 1)
    xb = xp_ref[seqs, t0:t0 + CONV_PAD + tt, xsl].reshape(nb * (n_blk + 1), SUBLANES, HEAD_A)
    cur = xb.reshape(nb, n_blk + 1, SUBLANES, HEAD_A)[:, 1:].reshape(blk_shape)
    xc = cb_ref[:, sl][None] + cw_ref[CONV_W - 1:CONV_W, sl][None] * cur
    for s in range(1, CONV_W):
        rolled = pltpu.roll(xb, s, axis=1).reshape(nb, n_blk + 1, SUBLANES, HEAD_A)
        shifted = jnp.where(t_in_blk >= s, rolled[:, 1:].reshape(blk_shape), rolled[:, :-1].reshape(blk_shape))
        xc = xc + cw_ref[CONV_W - 1 - s:CONV_W - s, sl][None] * shifted
    xc = xc.reshape(rows, HEAD_A)
    xcb = xc.astype(BF16)
    r = _sigmoid_tanh(jnp.dot(xcb, wr, preferred_element_type=F32) + br)
    i = _sigmoid_tanh(jnp.dot(xcb, wi, preferred_element_type=F32) + bi)
    log_a = (-C_GATE * r) * softplus
    a = jnp.exp(log_a)
    th = jnp.tanh(log_a)
    one_m_a2 = -2.0 * th / (1.0 - th)
    b = jnp.where(one_m_a2 > 0.0, one_m_a2 * lax.rsqrt(one_m_a2), 0.0) * (i * xc)
    a = a.reshape(blk_shape)
    b = b.reshape(blk_shape)
    d = 1
    while d < SUBLANES:
        a_sh = pltpu.roll(a, d, axis=1)
        b_sh = pltpu.roll(b, d, axis=1)
        keep = t_in_blk >= d
        b = jnp.where(keep, a * b_sh + b, b)
        a = jnp.where(keep, a * a_sh, a)
        d *= 2
    a3 = a.reshape(nb, tt, HEAD_A)
    b3 = b.reshape(nb, tt, HEAD_A)
    h_blocks = []
    for k in range(n_blk):
        ks = slice(k * SUBLANES, (k + 1) * SUBLANES)
        h_k = a3[:, ks, :] * carry + b3[:, ks, :]
        carry = h_k[:, SUBLANES - 1:SUBLANES, :]
        h_blocks.append(h_k)
    h = jnp.concatenate(h_blocks, axis=1) if n_blk > 1 else h_blocks[0]
    return ga * h.reshape(rows, HEAD_A), carry


def _branch_a_kernel(x_ref, wxa_ref, wga_ref, conv0_ref, h0_ref, cw_ref, cb_ref, wr_ref, wi_ref,
                     br_ref, bi_ref, lam_ref, ya_ref, hlast_ref, convnew_ref, h_ref, *xp_refs,
                     n_seq, t_tile, nb, tt, tiles_per_seq):
    i = pl.program_id(1)
    hist = CONV_W - 1
    lo = CONV_PAD - hist
    n_col = HEADS_PER_DOT * HEAD_A
    pair_cols = [slice(pr * n_col, (pr + 1) * n_col) for pr in range(len(xp_refs))]

    @pl.when(i % tiles_per_seq == 0)
    def _():
        for xp_ref, ps in zip(xp_refs, pair_cols):
            xp_ref[:, 0:lo, :] = jnp.zeros((n_seq, lo, n_col), F32)
            xp_ref[:, lo:CONV_PAD, :] = conv0_ref[:, :, ps]
        h_ref[...] = h0_ref[...]

    z = -lam_ref[...]
    softplus = jnp.maximum(z, 0.0) + jnp.log1p(jnp.exp(-jnp.abs(z)))
    x = x_ref[...]

    def project(pr):
        ps = pair_cols[pr]
        xa = jnp.dot(x, wxa_ref[:, ps], preferred_element_type=F32)
        xp_refs[pr][:, CONV_PAD:CONV_PAD + t_tile, :] = xa.reshape(n_seq, t_tile, n_col)
        return _gelu(jnp.dot(x, wga_ref[:, ps], preferred_element_type=F32))

    ga_next = project(0)
    for pr, xp_ref in enumerate(xp_refs):
        ga = ga_next
        if pr + 1 < len(xp_refs):
            ga_next = project(pr + 1)
        for hh in range(HEADS_PER_DOT):
            hd = pr * HEADS_PER_DOT + hh
            xsl = slice(hh * HEAD_A, (hh + 1) * HEAD_A)
            sl = slice(hd * HEAD_A, (hd + 1) * HEAD_A)
            wr = wr_ref[hd].astype(BF16)
            wi = wi_ref[hd].astype(BF16)
            for us in range(n_seq // nb):
                seqs = slice(us * nb, (us + 1) * nb)
                carry = h_ref[seqs, :, sl]
                for ut in range(t_tile // tt):
                    r0 = us * nb * t_tile + ut * tt
                    y, carry = _lru_unit(
                        xp_ref, seqs, ut * tt, xsl, ga[r0:r0 + nb * tt, xsl], carry,
                        cw_ref, cb_ref, wr, wi, br_ref[:, sl], bi_ref[:, sl], softplus[:, sl], sl, nb, tt)
                    ya_ref[r0:r0 + nb * tt, sl] = y.astype(BF16)
                h_ref[seqs, :, sl] = carry

    for xp_ref in xp_refs:
        xp_ref[:, lo:CONV_PAD, :] = xp_ref[:, lo + t_tile:CONV_PAD + t_tile, :]

    @pl.when(i % tiles_per_seq == tiles_per_seq - 1)
    def _():
        hlast_ref[...] = h_ref[...]
        for xp_ref, ps in zip(xp_refs, pair_cols):
            convnew_ref[:, :, ps] = xp_ref[:, lo:CONV_PAD, :]


def _branch_a(grp, tm, tc, nb, tt, xn, w_ag, layer, conv0, h0, cw, cb, wr, wi, br, bi, lam):
    b, t = grp.b, grp.t
    t_tile = min(t, tm)
    n_seq = tm // t_tile
    tiles_per_seq = t // t_tile
    assert (nb == 1 or tt == t_tile) and n_seq % nb == 0 and t_tile % tt == 0
    hist = CONV_W - 1
    n_j = D_MODEL // tc
    seq_blk = lambda j, i: (i // tiles_per_seq, 0, j)
    vec = lambda: pl.BlockSpec((1, tc), lambda j, i: (0, j))
    head_w = lambda: pl.BlockSpec((tc // HEAD_A, HEAD_A, HEAD_A), lambda j, i: (j, 0, 0))
    return pl.pallas_call(
        functools.partial(_branch_a_kernel, n_seq=n_seq, t_tile=t_tile, nb=nb, tt=tt,
                          tiles_per_seq=tiles_per_seq),
        grid=(n_j, grp.m // tm),
        in_specs=[
            pl.BlockSpec((tm, D_MODEL), lambda j, i: (i, 0)),
            pl.BlockSpec((None, D_MODEL, tc), lambda j, i: (layer, 0, j)),
            pl.BlockSpec((None, D_MODEL, tc), lambda j, i: (layer, 0, n_j + j)),
            pl.BlockSpec((n_seq, hist, tc), seq_blk),
            pl.BlockSpec((n_seq, 1, tc), seq_blk),
            pl.BlockSpec((CONV_W, tc), lambda j, i: (0, j)),
            vec(), head_w(), head_w(), vec(), vec(), vec(),
        ],
        out_specs=[
            pl.BlockSpec((tm, tc), lambda j, i: (i, j)),
            pl.BlockSpec((n_seq, 1, tc), seq_blk),
            pl.BlockSpec((n_seq, hist, tc), seq_blk),
        ],
        out_shape=[
            jax.ShapeDtypeStruct((grp.m, D_MODEL), BF16),
            jax.ShapeDtypeStruct((b, 1, D_MODEL), F32),
            jax.ShapeDtypeStruct((b, hist, D_MODEL), F32),
        ],
        scratch_shapes=[pltpu.VMEM((n_seq, 1, tc), F32)]
        + [pltpu.VMEM((n_seq, CONV_PAD + t_tile, HEADS_PER_DOT * HEAD_A), F32)]
        * (tc // (HEADS_PER_DOT * HEAD_A)),
        compiler_params=_params(("arbitrary", "arbitrary"),
                                [((tm, D_MODEL), BF16), ((D_MODEL, tc), BF16), ((D_MODEL, tc), BF16),
                                 ((tm, tc), BF16)] + [((n_seq, SUBLANES, tc), F32)] * 4,
                                [((n_seq, CONV_PAD + t_tile, tc), F32), ((n_seq, SUBLANES, tc), F32),
                                 ((tm, 2 * HEADS_PER_DOT * HEAD_A), F32)]),
        name="branch_a",
    )(xn, w_ag, w_ag, conv0, h0.reshape(b, 1, D_MODEL), cw, cb.reshape(1, D_MODEL),
      wr, wi, br.reshape(1, D_MODEL), bi.reshape(1, D_MODEL), lam.reshape(1, D_MODEL))


def _branch_b_kernel(u_ref, v_ref, gv_ref, ws_ref, bias_ref, yb_ref, wm_ref, *, n_chunks):
    @pl.when(pl.program_id(0) == 0)
    def _():
        tri = (lax.broadcasted_iota(jnp.int32, (CHUNK, CHUNK), 0)
               >= lax.broadcasted_iota(jnp.int32, (CHUNK, CHUNK), 1))
        for g in range(N_GROUPS_B):
            wm_ref[g] = jnp.where(tri, ws_ref[g], 0.0).astype(BF16)

    vn = _rms(v_ref[...].astype(F32), gv_ref[...]).astype(BF16)
    for g in range(N_GROUPS_B):
        cs = slice(g * GROUP_B, (g + 1) * GROUP_B)
        rhs = jnp.concatenate([vn[c * CHUNK:(c + 1) * CHUNK, cs] for c in range(n_chunks)], axis=1)
        mixed = jnp.dot(wm_ref[g], rhs, preferred_element_type=F32)
        for c in range(n_chunks):
            rs = slice(c * CHUNK, (c + 1) * CHUNK)
            mixed_c = mixed[:, c * GROUP_B:(c + 1) * GROUP_B] + bias_ref[:, cs]
            yb_ref[rs, cs] = (u_ref[rs, cs].astype(F32) * mixed_c).astype(BF16)


def _branch_b(grp, gel, gv, ws, bias_full):
    tm = 512
    return pl.pallas_call(
        functools.partial(_branch_b_kernel, n_chunks=tm // CHUNK),
        grid=(grp.m // tm,),
        in_specs=[
            pl.BlockSpec((tm, D_MODEL), lambda i: (i, 0)),
            pl.BlockSpec((tm, D_MODEL), lambda i: (i, 1)),
            pl.BlockSpec((1, D_MODEL), lambda i: (0, 0)),
            pl.BlockSpec((N_GROUPS_B, CHUNK, CHUNK), lambda i: (0, 0, 0)),
            pl.BlockSpec((CHUNK, D_MODEL), lambda i: (0, 0)),
        ],
        out_specs=pl.BlockSpec((tm, D_MODEL), lambda i: (i, 0)),
        out_shape=jax.ShapeDtypeStruct((grp.m, D_MODEL), BF16),
        scratch_shapes=[pltpu.VMEM((N_GROUPS_B, CHUNK, CHUNK), BF16)],
        compiler_params=_params(("arbitrary",),
                                [((tm, D_MODEL), BF16)] * 3 + [((N_GROUPS_B, CHUNK, CHUNK), F32),
                                                                ((CHUNK, D_MODEL), F32)],
                                [((tm, D_MODEL), F32)] * 2),
        name="branch_b",
    )(gel, gel, gv.reshape(1, D_MODEL), ws, bias_full)


def _branch_b_short_kernel(u_ref, v_ref, gv_ref, wc_ref, bias_ref, yb_ref, vn_ref, *, nb, t):
    vn = _rms(v_ref[...].astype(F32), gv_ref[...]).reshape(nb, t, D_MODEL)
    vn_ref[...] = vn
    mixed = bias_ref[...][None]
    for s in range(t):
        mixed = mixed + wc_ref[s][None] * vn[:, s:s + 1, :]
    yb = u_ref[...].astype(F32).reshape(nb, t, D_MODEL) * mixed
    yb_ref[...] = yb.reshape(nb * t, D_MODEL).astype(BF16)


def _branch_b_short(grp, gel, gv, ws, bs):
    b, t = grp.b, grp.t
    nb = 16
    rows = nb * t
    tri = jnp.tril(jnp.ones((t, t), F32))
    wc = jnp.repeat(jnp.transpose(ws[:, :t, :t] * tri, (2, 1, 0)), GROUP_B, axis=-1)
    bias = jnp.repeat(bs[:, :t].T, GROUP_B, axis=-1)
    return pl.pallas_call(
        functools.partial(_branch_b_short_kernel, nb=nb, t=t),
        grid=(b // nb,),
        in_specs=[
            pl.BlockSpec((rows, D_MODEL), lambda i: (i, 0)),
            pl.BlockSpec((rows, D_MODEL), lambda i: (i, 1)),
            pl.BlockSpec((1, D_MODEL), lambda i: (0, 0)),
            pl.BlockSpec((t, t, D_MODEL), lambda i: (0, 0, 0)),
            pl.BlockSpec((t, D_MODEL), lambda i: (0, 0)),
        ],
        out_specs=[
            pl.BlockSpec((rows, D_MODEL), lambda i: (i, 0)),
            pl.BlockSpec((nb, t, D_MODEL), lambda i: (i, 0, 0)),
        ],
        out_shape=[
            jax.ShapeDtypeStruct((grp.m, D_MODEL), BF16),
            jax.ShapeDtypeStruct((b, t, D_MODEL), F32),
        ],
        compiler_params=_params(("arbitrary",),
                                [((rows, D_MODEL), BF16)] * 3 + [((rows, D_MODEL), F32),
                                                                  ((t, t, D_MODEL), F32)],
                                [((rows, D_MODEL), F32)] * 3),
        name="branch_b_short",
    )(gel, gel, gv.reshape(1, D_MODEL), wc, bias)


def _merge_kernel(ya_ref, yb_ref, sa_ref, sb_ref, wpa_ref, wpb_ref, o_ref, wa_ref, wb_ref):
    @pl.when(pl.program_id(1) == 0)
    def _():
        wa_ref[...] = wpa_ref[...].astype(BF16)
        wb_ref[...] = wpb_ref[...].astype(BF16)

    pa = jnp.dot(ya_ref[...], wa_ref[...], preferred_element_type=F32)
    pb = jnp.dot(yb_ref[...], wb_ref[...], preferred_element_type=F32)
    o_ref[...] = (sa_ref[...].astype(F32) * pa + sb_ref[...].astype(F32) * pb).astype(BF16)


def _merge(grp, ya, yb, sig, wpa, wpb, layer):
    tm, tn = 1024, 512
    nj = D_MODEL // tn
    return pl.pallas_call(
        _merge_kernel,
        grid=(nj, grp.m // tm),
        in_specs=[
            pl.BlockSpec((tm, D_MODEL), lambda j, i: (i, 0)),
            pl.BlockSpec((tm, D_MODEL), lambda j, i: (i, 0)),
            pl.BlockSpec((tm, tn), lambda j, i: (i, j)),
            pl.BlockSpec((tm, tn), lambda j, i: (i, nj + j)),
            pl.BlockSpec((None, D_MODEL, tn), lambda j, i: (layer, 0, j)),
            pl.BlockSpec((None, D_MODEL, tn), lambda j, i: (layer, 0, j)),
        ],
        out_specs=pl.BlockSpec((tm, tn), lambda j, i: (i, j)),
        out_shape=jax.ShapeDtypeStruct((grp.m, D_MODEL), BF16),
        scratch_shapes=[pltpu.VMEM((D_MODEL, tn), BF16), pltpu.VMEM((D_MODEL, tn), BF16)],
        compiler_params=_params(("arbitrary", "arbitrary"),
                                [((tm, D_MODEL), BF16)] * 2 + [((tm, tn), BF16)] * 3
                                + [((D_MODEL, tn), F32)] * 2,
                                [((D_MODEL, tn), BF16)] * 2 + [((tm, tn), F32)] * 2),
        name="merge",
    )(ya, yb, sig, sig, wpa, wpb)


def _post_math(y, x, gpost, gt, nxt):
    x1 = x + gt * _rms(y, gpost)
    if nxt is None:
        return x1, None
    gpre, one_plus_sc, sh = nxt
    return x1, _rms(x1, gpre) * one_plus_sc + sh


def _next_mods(gpre_ref, sc_ref, sh_ref, rows):
    return gpre_ref[...], _mod_rows(sc_ref, rows, 1.0), _mod_rows(sh_ref, rows)


def _oproj_post_kernel(m_ref, wo_ref, x_ref, gpost_ref, gt_ref, gpre_ref, sc_ref, sh_ref,
                       xo_ref, xn_ref):
    y = jnp.dot(m_ref[...], wo_ref[...], preferred_element_type=F32)
    rows = x_ref.shape[0]
    x1, xn = _post_math(y, x_ref[...], gpost_ref[...], _mod_rows(gt_ref, rows),
                        _next_mods(gpre_ref, sc_ref, sh_ref, rows))
    xo_ref[...] = x1
    xn_ref[...] = xn.astype(xn_ref.dtype)


def _oproj_post(grp, m, wo, layer, x, gpost, gt, gpre, sc, sh, xn_dtype):
    tm = 512
    vec = lambda: pl.BlockSpec((1, D_MODEL), lambda i: (0, 0))
    row = lambda: pl.BlockSpec((tm, D_MODEL), lambda i: (i, 0))
    return pl.pallas_call(
        _oproj_post_kernel,
        grid=(grp.m // tm,),
        in_specs=[
            row(),
            pl.BlockSpec((None, D_MODEL, D_MODEL), lambda i: (layer, 0, 0),
                         pipeline_mode=pl.Buffered(1)),
            row(), vec(), grp.mod_spec(tm), vec(), grp.mod_spec(tm), grp.mod_spec(tm),
        ],
        out_specs=[row(), row()],
        out_shape=[
            jax.ShapeDtypeStruct((grp.m, D_MODEL), F32),
            jax.ShapeDtypeStruct((grp.m, D_MODEL), xn_dtype),
        ],
        compiler_params=_params(("arbitrary",),
                                [((tm, D_MODEL), F32)] * 5,
                                [((D_MODEL, D_MODEL), BF16), ((tm, D_MODEL), F32)]),
        name="oproj_post",
    )(m, wo, x, gpost.reshape(1, D_MODEL), gt, gpre.reshape(1, D_MODEL), sc, sh)


def _post_kernel(y_ref, x_ref, gpost_ref, gt_ref, *rest, with_next):
    rows = x_ref.shape[0]
    if with_next:
        gpre_ref, sc_ref, sh_ref, xo_ref, xn_ref = rest
        nxt = _next_mods(gpre_ref, sc_ref, sh_ref, rows)
    else:
        (xo_ref,) = rest
        nxt = None
    x1, xn = _post_math(y_ref[...], x_ref[...], gpost_ref[...], _mod_rows(gt_ref, rows), nxt)
    xo_ref[...] = x1
    if with_next:
        xn_ref[...] = xn.astype(xn_ref.dtype)


def _post_specs(grp, tm, x, gpost, gt, nxt, imap):
    vec = lambda: pl.BlockSpec((1, D_MODEL), lambda *ids: (0, 0))
    row = lambda: pl.BlockSpec((tm, D_MODEL), imap)
    in_specs = [row(), vec(), grp.mod_spec(tm)]
    args = [x, gpost.reshape(1, D_MODEL), gt]
    out_specs = [row()]
    out_shape = [jax.ShapeDtypeStruct((grp.m, D_MODEL), F32)]
    if nxt is not None:
        gpre, sc, sh = nxt
        in_specs += [vec(), grp.mod_spec(tm), grp.mod_spec(tm)]
        args += [gpre.reshape(1, D_MODEL), sc, sh]
        out_specs.append(row())
        out_shape.append(jax.ShapeDtypeStruct((grp.m, D_MODEL), BF16))
    return in_specs, args, out_specs, out_shape


def _post(grp, y, x, gpost, gt, nxt):
    tm = 256
    imap = lambda i: (i, 0)
    in_specs, args, out_specs, out_shape = _post_specs(grp, tm, x, gpost, gt, nxt, imap)
    out = pl.pallas_call(
        functools.partial(_post_kernel, with_next=nxt is not None),
        grid=(grp.m // tm,),
        in_specs=[pl.BlockSpec((tm, D_MODEL), imap)] + in_specs,
        out_specs=out_specs,
        out_shape=out_shape,
        compiler_params=_params(("arbitrary",), [((tm, D_MODEL), F32)] * 8),
        name="post",
    )(y, *args)
    return (out[0], out[1]) if nxt is not None else (out[0], None)


def _swiglu_tile(x, wg_ref, wu_ref, wd_ref):
    g = jnp.dot(x, wg_ref[...], preferred_element_type=F32)
    u = jnp.dot(x, wu_ref[...], preferred_element_type=F32)
    h = _silu(g) * u
    return jnp.dot(h.astype(BF16), wd_ref[...], preferred_element_type=F32)


def _swiglu_kernel(x_ref, wg_ref, wu_ref, wd_ref, o_ref):
    @pl.when(pl.program_id(1) == 0)
    def _():
        o_ref[...] = jnp.zeros_like(o_ref)

    o_ref[...] += _swiglu_tile(x_ref[...], wg_ref, wu_ref, wd_ref)


def _swiglu(grp, x, wg, wu, wd):
    f = wg.shape[-1]
    tm, tf = 1024, 512
    return pl.pallas_call(
        _swiglu_kernel,
        grid=(grp.m // tm, f // tf),
        in_specs=[
            pl.BlockSpec((tm, D_MODEL), lambda i, j: (i, 0)),
            pl.BlockSpec((D_MODEL, tf), lambda i, j: (0, j)),
            pl.BlockSpec((D_MODEL, tf), lambda i, j: (0, j)),
            pl.BlockSpec((tf, D_MODEL), lambda i, j: (j, 0)),
        ],
        out_specs=pl.BlockSpec((tm, D_MODEL), lambda i, j: (i, 0)),
        out_shape=jax.ShapeDtypeStruct((grp.m, D_MODEL), F32),
        compiler_params=_params(("arbitrary", "arbitrary"),
                                [((tm, D_MODEL), BF16), ((tm, D_MODEL), F32)]
                                + [((D_MODEL, tf), BF16)] * 3,
                                [((tm, tf), F32)] * 4),
        name="swiglu_dense",
    )(x, wg, wu, wd)


META_E1, META_E2, META_R1, META_R2, META_W1, META_W2 = range(6)
DMA_UNROLL = 8


def _split_bf16(v):
    hi = v.astype(BF16)
    return hi, (v - hi.astype(F32)).astype(BF16)


def _router_kernel(x_ref, rw_ref, rb_ref, meta_ref, cnt_ref, tri_ref):
    tm = x_ref.shape[0]

    @pl.when(pl.program_id(0) == 0)
    def _():
        cnt_ref[...] = jnp.zeros_like(cnt_ref)
        tri_ref[...] = (lax.broadcasted_iota(jnp.int32, (tm, tm), 0)
                        >= lax.broadcasted_iota(jnp.int32, (tm, tm), 1)).astype(BF16)

    xh, xl = _split_bf16(x_ref[...])
    wh, wl = _split_bf16(rw_ref[...])
    logits = (jnp.dot(xh, wh, preferred_element_type=F32) + jnp.dot(xl, wh, preferred_element_type=F32)
              + jnp.dot(xh, wl, preferred_element_type=F32)) + rb_ref[...]
    lane = lax.broadcasted_iota(jnp.int32, logits.shape, 1)
    logits = jnp.where(lane < N_EXPERTS, logits, -jnp.inf)
    ex = jnp.exp(logits - jnp.max(logits, axis=-1, keepdims=True))
    p = ex / jnp.sum(ex, axis=-1, keepdims=True)
    p1 = jnp.max(p, axis=-1, keepdims=True)
    i1 = jnp.min(jnp.where(p == p1, lane, LANES), axis=-1, keepdims=True)
    rest = jnp.where(lane == i1, -1.0, p)
    p2 = jnp.max(rest, axis=-1, keepdims=True)
    i2 = jnp.min(jnp.where(rest == p2, lane, LANES), axis=-1, keepdims=True)
    denom = p1 + p2
    onehot = jnp.where(lane == i1, 1.0, 0.0) + jnp.where(lane == i2, 1.0, 0.0)
    cum = jnp.dot(tri_ref[...], onehot.astype(BF16), preferred_element_type=F32) + cnt_ref[...]
    cnt_ref[...] = cum[tm - 1:tm, :]
    r1 = jnp.sum(jnp.where(lane == i1, cum, 0.0), axis=-1, keepdims=True) - 1.0
    r2 = jnp.sum(jnp.where(lane == i2, cum, 0.0), axis=-1, keepdims=True) - 1.0
    meta = jnp.zeros_like(logits)
    for k, v in ((META_E1, i1.astype(F32)), (META_E2, i2.astype(F32)), (META_R1, r1), (META_R2, r2),
                 (META_W1, p1 / denom), (META_W2, p2 / denom)):
        meta = jnp.where(lane == k, v, meta)
    meta_ref[...] = meta


def _router(grp, x, rw_pad, rb_pad):
    tm = 512
    return pl.pallas_call(
        _router_kernel,
        grid=(grp.m // tm,),
        in_specs=[
            pl.BlockSpec((tm, D_MODEL), lambda i: (i, 0)),
            pl.BlockSpec((D_MODEL, LANES), lambda i: (0, 0)),
            pl.BlockSpec((1, LANES), lambda i: (0, 0)),
        ],
        out_specs=[
            pl.BlockSpec((tm, LANES), lambda i: (i, 0)),
            pl.BlockSpec((1, LANES), lambda i: (0, 0)),
        ],
        out_shape=[
            jax.ShapeDtypeStruct((grp.m, LANES), F32),
            jax.ShapeDtypeStruct((1, LANES), F32),
        ],
        scratch_shapes=[pltpu.VMEM((tm, tm), BF16)],
        compiler_params=_params(("arbitrary",),
                                [((tm, D_MODEL), F32), ((D_MODEL, LANES), F32), ((tm, LANES), F32)],
                                [((tm, tm), BF16), ((tm, D_MODEL), F32)]),
        name="router",
    )(x, rw_pad, rb_pad)


def _route_plan(metas, counts, tg, n_tiles):
    cnts = [c[0, :N_EXPERTS].astype(jnp.int32) for c in counts]
    total = sum(cnts)
    padded = (total + tg - 1) // tg * tg
    ends = jnp.cumsum(padded)
    offs = ends - padded
    slots = []
    for meta, before in zip(metas, [sum(cnts[:g], jnp.zeros_like(total)) for g in range(len(cnts))]):
        idx = meta[:, :META_W1].astype(jnp.int32)
        start = offs + before
        slots.append((start[idx[:, META_E1]] + idx[:, META_R1], start[idx[:, META_E2]] + idx[:, META_R2]))
    n_used = ends[-1] // tg
    tile = jnp.arange(n_tiles, dtype=jnp.int32)
    tile_c = jnp.minimum(tile, n_used - 1)
    tile_expert = jnp.sum(tile_c[:, None] * tg >= ends[None, :], axis=1).astype(jnp.int32)
    return slots, tile_expert, n_used.reshape(1).astype(jnp.int32)


def _dispatch_kernel(pos1_ref, pos2_ref, x_ref, xs_in_ref, xs_ref, sem):
    del xs_in_ref
    tm = x_ref.shape[0]
    base = pl.program_id(0) * tm

    def row_copy(t, slot):
        return pltpu.make_async_copy(x_ref.at[pl.ds(t, 1)], xs_ref.at[pl.ds(slot, 1)], sem)

    def start(t, c):
        row_copy(t, pos1_ref[base + t]).start()
        row_copy(t, pos2_ref[base + t]).start()
        return c

    def wait(t, c):
        row_copy(t, 0).wait()
        row_copy(t, 0).wait()
        return c

    lax.fori_loop(0, tm, start, 0, unroll=DMA_UNROLL)
    lax.fori_loop(0, tm, wait, 0, unroll=DMA_UNROLL)


def _dispatch(grp, x, pos1, pos2, xs):
    tm = 1024
    return pl.pallas_call(
        _dispatch_kernel,
        grid_spec=pltpu.PrefetchScalarGridSpec(
            num_scalar_prefetch=2,
            grid=(grp.m // tm,),
            in_specs=[
                pl.BlockSpec((tm, D_MODEL), lambda i, p1, p2: (i, 0)),
                pl.BlockSpec(memory_space=pl.ANY),
            ],
            out_specs=pl.BlockSpec(memory_space=pl.ANY),
            scratch_shapes=[pltpu.SemaphoreType.DMA(())],
        ),
        out_shape=jax.ShapeDtypeStruct(xs.shape, F32),
        input_output_aliases={3: 0},
        compiler_params=_params(("arbitrary",), [((tm, D_MODEL), F32)]),
        name="moe_dispatch",
    )(pos1, pos2, x, xs)


def _experts_kernel(te_ref, nu_ref, xs_ref, wg_ref, wu_ref, wd_ref, o_ref):
    del te_ref
    r, j = pl.program_id(0), pl.program_id(1)

    @pl.when(j == 0)
    def _():
        o_ref[...] = jnp.zeros_like(o_ref)

    @pl.when(r < nu_ref[0])
    def _():
        o_ref[...] += _swiglu_tile(xs_ref[...].astype(BF16), wg_ref, wu_ref, wd_ref)


def _experts(xs, tile_expert, n_used, wg, wu, wd, tg):
    n_rows = xs.shape[0]
    f = wg.shape[-1]
    tf = 512
    n_j = f // tf

    def used(r, nu):
        return r < nu[0]

    def x_map(r, j, te, nu):
        return (jnp.minimum(r, nu[0] - 1), 0)

    def up_map(r, j, te, nu):
        return (te[r], 0, jnp.where(used(r, nu), j, n_j - 1))

    def down_map(r, j, te, nu):
        return (te[r], jnp.where(used(r, nu), j, n_j - 1), 0)

    return pl.pallas_call(
        _experts_kernel,
        grid_spec=pltpu.PrefetchScalarGridSpec(
            num_scalar_prefetch=2,
            grid=(n_rows // tg, n_j),
            in_specs=[
                pl.BlockSpec((tg, D_MODEL), x_map),
                pl.BlockSpec((None, D_MODEL, tf), up_map),
                pl.BlockSpec((None, D_MODEL, tf), up_map),
                pl.BlockSpec((None, tf, D_MODEL), down_map),
            ],
            out_specs=pl.BlockSpec((tg, D_MODEL), lambda r, j, te, nu: (r, 0)),
        ),
        out_shape=jax.ShapeDtypeStruct((n_rows, D_MODEL), F32),
        compiler_params=_params(("arbitrary", "arbitrary"),
                                [((tg, D_MODEL), F32)] * 2 + [((D_MODEL, tf), BF16)] * 3,
                                [((tg, tf), F32)] * 4 + [((tg, D_MODEL), BF16)]),
        name="moe_experts",
    )(tile_expert, n_used, xs, wg, wu, wd)


def _combine_post_kernel(pos1_ref, pos2_ref, ys_ref, meta_ref, x_ref, gpost_ref, gt_ref, *rest,
                         with_next):
    *rest, buf1_ref, buf2_ref, sem = rest
    tm = x_ref.shape[0]
    step, n_steps = pl.program_id(0), pl.num_programs(0)
    cur = step % 2

    def row_copy(row, buf_ref, half, t):
        return pltpu.make_async_copy(ys_ref.at[pl.ds(row, 1)], buf_ref.at[half, pl.ds(t, 1)],
                                     sem.at[half])

    def gather(tile, half):
        def start(t, c):
            row_copy(pos1_ref[tile * tm + t], buf1_ref, half, t).start()
            row_copy(pos2_ref[tile * tm + t], buf2_ref, half, t).start()
            return c
        lax.fori_loop(0, tm, start, 0, unroll=DMA_UNROLL)

    @pl.when(step == 0)
    def _():
        gather(0, 0)

    @pl.when(step + 1 < n_steps)
    def _():
        gather(step + 1, 1 - cur)

    def wait(t, c):
        row_copy(0, buf1_ref, cur, t).wait()
        row_copy(0, buf2_ref, cur, t).wait()
        return c

    lax.fori_loop(0, tm, wait, 0, unroll=DMA_UNROLL)
    meta = meta_ref[...]
    y = meta[:, META_W1:META_W1 + 1] * buf1_ref[cur] + meta[:, META_W2:META_W2 + 1] * buf2_ref[cur]
    if with_next:
        gpre_ref, sc_ref, sh_ref, xo_ref, xn_ref = rest
        nxt = _next_mods(gpre_ref, sc_ref, sh_ref, tm)
    else:
        (xo_ref,) = rest
        nxt = None
    x1, xn = _post_math(y, x_ref[...], gpost_ref[...], _mod_rows(gt_ref, tm), nxt)
    xo_ref[...] = x1
    if with_next:
        xn_ref[...] = xn.astype(xn_ref.dtype)


def _combine_post(grp, ys, pos1, pos2, meta, x, gpost, gt, nxt):
    tm = 256
    imap = lambda i, p1, p2: (i, 0)
    in_specs, args, out_specs, out_shape = _post_specs(grp, tm, x, gpost, gt, nxt, imap)
    out = pl.pallas_call(
        functools.partial(_combine_post_kernel, with_next=nxt is not None),
        grid_spec=pltpu.PrefetchScalarGridSpec(
            num_scalar_prefetch=2,
            grid=(grp.m // tm,),
            in_specs=[pl.BlockSpec(memory_space=pl.ANY), pl.BlockSpec((tm, LANES), imap)] + in_specs,
            out_specs=out_specs,
            scratch_shapes=[pltpu.VMEM((2, tm, D_MODEL), F32), pltpu.VMEM((2, tm, D_MODEL), F32),
                            pltpu.SemaphoreType.DMA((2,))],
        ),
        out_shape=out_shape,
        compiler_params=_params(("arbitrary",), [((tm, D_MODEL), F32)] * 7,
                                [((tm, D_MODEL), F32)] * 4),
        name="moe_combine_post",
    )(pos1, pos2, ys, meta, *args)
    return (out[0], out[1]) if nxt is not None else (out[0], None)


def _moe_post(grps, xns, p, j, xs_res, gpost, gts, nxts):
    tg = 512
    n_tiles = 2 * sum(g.m for g in grps) // tg + N_EXPERTS
    rw_pad = jnp.pad(p['router_w'][j], ((0, 0), (0, LANES - N_EXPERTS)))
    rb_pad = jnp.pad(p['router_b'][j], (0, LANES - N_EXPERTS)).reshape(1, LANES)
    routed = [_router(g, xn, rw_pad, rb_pad) for g, xn in zip(grps, xns)]
    metas, counts = zip(*routed)
    slots, tile_expert, n_used = _route_plan(metas, counts, tg, n_tiles)
    xs = jnp.zeros((n_tiles * tg, D_MODEL), F32)
    for g, xn, (pos1, pos2) in zip(grps, xns, slots):
        xs = _dispatch(g, xn, pos1, pos2, xs)
    wg, wu, wd = (p['jobs'].result(k)[j] for k in ('moe_wg', 'moe_wu', 'moe_wd'))
    ys = _experts(xs, tile_expert, n_used, wg, wu, wd, tg)
    return [_combine_post(g, ys, pos1, pos2, meta, x, gpost, gt, nxt)
            for g, (pos1, pos2), meta, x, gt, nxt in zip(grps, slots, metas, xs_res, gts, nxts)]


def _mixer(grp, l, x, xn, mods, h0, conv0, p, a_tiles, xn_dtype):
    _, _, gt1, sh2, sc2, _ = mods
    w_in = p['w_in']
    ya, h_last, conv_new = _branch_a(
        grp, *a_tiles, xn, p['w_ag_bf'], l, conv0, h0, p['conv_w'][l], p['conv_b'][l],
        p['w_r'][l], p['w_i'][l], p['b_r'][l], p['b_i'][l], p['lru_lambda'][l])
    gel = _mm_act(grp, xn, w_in, l, 2 * D_MODEL, 2 * D_MODEL, _gelu, BF16, "in_proj_gelu", p['jobs'])
    sig = _mm_act(grp, xn, w_in, l, 4 * D_MODEL, 2 * D_MODEL, _sigmoid, BF16, "in_proj_sigmoid", p['jobs'])
    if grp.t >= CHUNK:
        bias_full = jnp.repeat(p['b_s'][l].T, GROUP_B, axis=-1)
        yb = _branch_b(grp, gel, p['g_v'][l], p['w_s'][l], bias_full)
        vn = None
    else:
        yb, vn = _branch_b_short(grp, gel, p['g_v'][l], p['w_s'][l], p['b_s'][l])
    m = _merge(grp, ya, yb, sig, p['w_pa'], p['w_pb'], l)
    x, xn2 = _oproj_post(grp, m, p['w_o_bf'], l, x, p['g_post_mix'][l], gt1,
                         p['g_pre_ffn'][l], sc2, sh2, xn_dtype)
    return x, xn2, h_last.reshape(grp.b, D_MODEL), conv_new, vn


def _trunk(grps, xs, mods, h0s, conv0s, p, tiles):
    depth = p['w_in'].shape[0]
    n_g = len(grps)
    xns = [_prenorm(g, x, p['g_pre_mix'][0], mods[i][0][1], mods[i][0][0])
           for i, (g, x) in enumerate(zip(grps, xs))]
    states = [([], [], []) for _ in grps]
    for l in range(depth):
        dense = l % 2 == 0
        j = l // 2
        xn2s = []
        for i, g in enumerate(grps):
            xs[i], xn2, h_last, conv_new, vn = _mixer(
                g, l, xs[i], xns[i], mods[i][l], h0s[i][l], conv0s[i][l], p, tiles[i],
                BF16 if dense else F32)
            xn2s.append(xn2)
            for acc, v in zip(states[i], (h_last, conv_new, vn)):
                acc.append(v)
        if l + 1 < depth:
            nxts = [(p['g_pre_mix'][l + 1], mods[i][l + 1][1], mods[i][l + 1][0]) for i in range(n_g)]
        else:
            nxts = [None] * n_g
        gt2s = [mods[i][l][5] for i in range(n_g)]
        gpost = p['g_post_ffn'][l]
        if dense:
            outs = []
            for i, g in enumerate(grps):
                y = _swiglu(g, xn2s[i], p['ffn_wg_bf'][j], p['ffn_wu_bf'][j], p['ffn_wd_bf'][j])
                outs.append(_post(g, y, xs[i], gpost, gt2s[i], nxts[i]))
        else:
            outs = _moe_post(grps, xn2s, p, j, xs, gpost, gt2s, nxts)
        xs = [o[0] for o in outs]
        xns = [o[1] for o in outs]
    return xs, states


def kernel(x_prompt, x_sample, c_prompt, c_sample, state_lru_h, state_lru_conv, w_ada, b_ada, g_pre_mix, g_post_mix, g_pre_ffn, g_post_ffn, w_in, conv_w, conv_b, w_r, b_r, w_i, b_i, lru_lambda, g_v, w_s, b_s, w_pa, w_pb, w_o, ffn_wg, ffn_wu, ffn_wd, router_w, router_b, moe_wg, moe_wu, moe_wd):
    p = dict(g_pre_mix=g_pre_mix, g_post_mix=g_post_mix, g_pre_ffn=g_pre_ffn, g_post_ffn=g_post_ffn,
             w_in=w_in, conv_w=conv_w, conv_b=conv_b, w_r=w_r, b_r=b_r, w_i=w_i, b_i=b_i,
             lru_lambda=lru_lambda, g_v=g_v, w_s=w_s, b_s=b_s, w_pa=w_pa, w_pb=w_pb,
             router_w=router_w, router_b=router_b, w_o_bf=w_o.astype(BF16),
             w_ag_bf=w_in[:, :, :2 * D_MODEL].astype(BF16),
             ffn_wg_bf=ffn_wg.astype(BF16), ffn_wu_bf=ffn_wu.astype(BF16), ffn_wd_bf=ffn_wd.astype(BF16),
             jobs=_CastJobs(dict(moe_wg=moe_wg, moe_wu=moe_wu, moe_wd=moe_wd)))
    depth = w_in.shape[0]
    bp, tp, _ = x_prompt.shape
    bs, ts, _ = x_sample.shape
    assert tp % CHUNK == 0 and ts < CHUNK and ts % SUBLANES == 0

    n_c = bp + bs
    r_pad = -n_c % 16
    c_all = jnp.concatenate([c_sample, c_prompt, jnp.zeros((r_pad, D_MODEL), F32)], axis=0)
    mod = _ada(c_all, w_ada, b_ada)
    mods_s = [[mod[l, k, :bs] for k in range(6)] for l in range(depth)]
    mods_p = [[mod[l, k, bs:n_c].reshape(bp, 1, D_MODEL) for k in range(6)] for l in range(depth)]

    grp_p = _Group(bp, tp, short=False)
    grp_s = _Group(bs, ts, short=True)
    zeros_h = jnp.zeros((depth, bp, D_MODEL), F32)
    zeros_conv = jnp.zeros((depth, bp, CONV_W - 1, D_MODEL), F32)

    (y_p, y_s), (st_p, st_s) = _trunk(
        [grp_p, grp_s],
        [x_prompt.reshape(bp * tp, D_MODEL), x_sample.reshape(bs * ts, D_MODEL)],
        [mods_p, mods_s], [zeros_h, state_lru_h], [zeros_conv, state_lru_conv], p,
        tiles=[(1024, 1024, 1, 256), (512, 1024, 16, ts)])
    return (y_p.reshape(bp, tp, D_MODEL), y_s.reshape(bs, ts, D_MODEL),
            jnp.stack(st_p[0]), jnp.stack(st_p[1]), jnp.stack(st_s[0]), jnp.stack(st_s[1]),
            jnp.stack(st_s[2]))
```

```python
import functools

import jax
import jax.numpy as jnp
from jax import lax
from jax.experimental import pallas as pl
from jax.experimental.pallas import tpu as pltpu

F32 = jnp.float32
BF16 = jnp.bfloat16

D_MODEL = 2048
N_HEADS_A = 16
HEAD_A = D_MODEL // N_HEADS_A
CONV_W = 4
C_GATE = 8.0
N_GROUPS_B = 16
GROUP_B = D_MODEL // N_GROUPS_B
CHUNK = 128
N_EXPERTS = 8
EPS = 1e-6

LANES = 128
SUBLANES = 8
VMEM_CAP = 58 * 2**20
CONV_PAD = SUBLANES
CAST_SLAB_BYTES = 4 * 2**20


def _nbytes(shape, dtype):
    n = 1
    for s in shape:
        n *= s
    return n * jnp.dtype(dtype).itemsize


def _params(sem, blocks, scratch=()):
    need = 2 * sum(_nbytes(s, d) for s, d in blocks) + sum(_nbytes(s, d) for s, d in scratch)
    limit = min(VMEM_CAP, need + 16 * 2**20)
    return pltpu.CompilerParams(dimension_semantics=sem, vmem_limit_bytes=limit)


def _rms(x, g):
    return x * lax.rsqrt(jnp.mean(x * x, axis=-1, keepdims=True) + EPS) * g


def _sigmoid(x):
    return 1.0 / (1.0 + jnp.exp(-x))


def _sigmoid_tanh(x):
    return 0.5 * jnp.tanh(0.5 * x) + 0.5


def _silu(x):
    return x * _sigmoid(x)


def _gelu(x):
    return jax.nn.gelu(x)


class _Group:
    def __init__(self, b, t, short):
        self.b, self.t, self.m = b, t, b * t
        self.short = short

    def mod_spec(self, tm):
        if self.short:
            return pl.BlockSpec((tm // self.t, D_MODEL), lambda *ids: (ids[0], 0))
        t = self.t
        return pl.BlockSpec((None, 1, D_MODEL), lambda *ids: ((ids[0] * tm) // t, 0, 0))


def _mod_rows(ref, rows, offset=None):
    n = ref.shape[0]
    vals = ref[...] if offset is None else ref[...] + offset
    if n == 1:
        return vals
    rep = rows // n
    return jnp.concatenate([jnp.broadcast_to(vals[r:r + 1, :], (rep, D_MODEL)) for r in range(n)], axis=0)


def _ada_kernel(c_ref, w_ref, b_ref, o_ref):
    s = _silu(c_ref[...]).astype(BF16)
    o_ref[...] = jnp.dot(s, w_ref[...].astype(BF16), preferred_element_type=F32) + b_ref[...]


def _ada(c, w_ada, b_ada):
    depth, _, n = w_ada.shape
    r = c.shape[0]
    tn = 1024
    per_item = D_MODEL // tn
    return pl.pallas_call(
        _ada_kernel,
        grid=(depth, n // tn),
        in_specs=[
            pl.BlockSpec((r, D_MODEL), lambda l, j: (0, 0)),
            pl.BlockSpec((None, D_MODEL, tn), lambda l, j: (l, 0, j)),
            pl.BlockSpec((None, 1, tn), lambda l, j: (l, 0, j)),
        ],
        out_specs=pl.BlockSpec((None, None, r, tn), lambda l, j: (l, j // per_item, 0, j % per_item)),
        out_shape=jax.ShapeDtypeStruct((depth, n // D_MODEL, r, D_MODEL), F32),
        compiler_params=_params(("arbitrary", "arbitrary"),
                                [((r, D_MODEL), F32), ((D_MODEL, tn), F32), ((r, tn), F32)],
                                [((D_MODEL, tn), BF16)]),
        name="ada_mod",
    )(c, w_ada, b_ada.reshape(depth, 1, n))


def _prenorm_kernel(x_ref, g_ref, sc_ref, sh_ref, o_ref):
    rows = x_ref.shape[0]
    o_ref[...] = (_rms(x_ref[...], g_ref[...]) * _mod_rows(sc_ref, rows, 1.0)
                  + _mod_rows(sh_ref, rows)).astype(BF16)


def _prenorm(grp, x, g, sc, sh):
    tm = 256 if grp.short else 512
    return pl.pallas_call(
        _prenorm_kernel,
        grid=(grp.m // tm,),
        in_specs=[
            pl.BlockSpec((tm, D_MODEL), lambda i: (i, 0)),
            pl.BlockSpec((1, D_MODEL), lambda i: (0, 0)),
            grp.mod_spec(tm), grp.mod_spec(tm),
        ],
        out_specs=pl.BlockSpec((tm, D_MODEL), lambda i: (i, 0)),
        out_shape=jax.ShapeDtypeStruct((grp.m, D_MODEL), BF16),
        compiler_params=_params(("arbitrary",), [((tm, D_MODEL), F32)] * 4),
        name="prenorm",
    )(x, g.reshape(1, D_MODEL), sc, sh)


class _CastJobs:
    def __init__(self, arrays):
        self.shapes = {k: v.shape for k, v in arrays.items()}
        self.pending = [(k, v.reshape(-1, v.shape[-1])) for k, v in arrays.items()]
        self.done = {}

    def take(self, steps, slab_limit):
        for n, (name, src) in enumerate(self.pending):
            rows = src.shape[0] // steps
            if src.shape[0] % (steps * 2 * SUBLANES) == 0 and _nbytes((rows, src.shape[1]), F32) <= slab_limit:
                return self.pending.pop(n)
        return None

    def put(self, name, out):
        self.done[name] = out.reshape(self.shapes[name])

    def result(self, name):
        for n, (k, src) in enumerate(self.pending):
            if k == name:
                self.pending.pop(n)
                self.put(name, src.astype(BF16))
                break
        return self.done[name]


def _hosted_call(kernel, args, jobs, slab_limit, *, grid, in_specs, out_specs, out_shape,
                 scratch_shapes, blocks, scratch, name):
    steps = 1
    for n in grid:
        steps *= n
    job = jobs.take(steps, slab_limit) if jobs is not None else None
    if job is not None:
        n_in, n_out = len(in_specs), len(out_specs)
        src = job[1]
        slab = (src.shape[0] // steps, src.shape[1])
        strides = [1] * len(grid)
        for k in range(len(grid) - 2, -1, -1):
            strides[k] = strides[k + 1] * grid[k + 1]
        spec = pl.BlockSpec(slab, lambda *ids: (sum(i * s for i, s in zip(ids, strides)), 0))
        in_specs, out_specs = [*in_specs, spec], [*out_specs, spec]
        out_shape = [*out_shape, jax.ShapeDtypeStruct(src.shape, BF16)]
        args, blocks = [*args, src], [*blocks, (slab, F32), (slab, BF16)]
        inner = kernel

        def kernel(*refs):
            refs[n_in + 1 + n_out][...] = refs[n_in][...].astype(BF16)
            inner(*refs[:n_in], *refs[n_in + 1:n_in + 1 + n_out], *refs[n_in + 2 + n_out:])

    out = pl.pallas_call(
        kernel, grid=grid, in_specs=in_specs, out_specs=out_specs, out_shape=out_shape,
        scratch_shapes=scratch_shapes,
        compiler_params=_params(("arbitrary",) * len(grid), blocks, scratch),
        name=name,
    )(*args)
    if job is not None:
        jobs.put(job[0], out[-1])
        out = out[:-1]
    return list(out)


def _mm_act_kernel(x_ref, w_ref, o_ref, wbf_ref, *, act):
    @pl.when(pl.program_id(1) == 0)
    def _():
        wbf_ref[...] = w_ref[...].astype(BF16)

    y = jnp.dot(x_ref[...], wbf_ref[...], preferred_element_type=F32)
    o_ref[...] = act(y).astype(o_ref.dtype)


def _mm_act(grp, x, w, layer, col0, ncols, act, out_dtype, name, jobs=None):
    tm, tn = 1024, 1024
    c0 = col0 // tn
    return _hosted_call(
        functools.partial(_mm_act_kernel, act=act), [x, w], jobs, CAST_SLAB_BYTES,
        grid=(ncols // tn, grp.m // tm),
        in_specs=[
            pl.BlockSpec((tm, D_MODEL), lambda j, i: (i, 0)),
            pl.BlockSpec((None, D_MODEL, tn), lambda j, i: (layer, 0, c0 + j)),
        ],
        out_specs=[pl.BlockSpec((tm, tn), lambda j, i: (i, j))],
        out_shape=[jax.ShapeDtypeStruct((grp.m, ncols), out_dtype)],
        scratch_shapes=[pltpu.VMEM((D_MODEL, tn), BF16)],
        blocks=[((tm, D_MODEL), BF16), ((D_MODEL, tn), F32), ((tm, tn), F32)],
        scratch=[((D_MODEL, tn), BF16), ((tm, tn), F32)],
        name=name,
    )[0]


HEADS_PER_DOT = 2


def _lru_unit(xp_ref, seqs, t0, xsl, ga, carry, cw_ref, cb_ref, wr, wi, br, bi, softplus, sl, nb, tt):
    rows = nb * tt
    n_blk = tt // SUBLANES
    blk_shape = (nb * n_blk, SUBLANES, HEAD_A)
    t_in_blk = lax.broadcasted_iota(jnp.int32, blk_shape, 1)
    xb = xp_ref[seqs, t0:t0 + CONV_PAD + tt, xsl].reshape(nb * (n_blk + 1), SUBLANES, HEAD_A)
    cur = xb.reshape(nb, n_blk + 1, SUBLANES, HEAD_A)[:, 1:].reshape(blk_shape)
    xc = cb_ref[:, sl][None] + cw_ref[CONV_W - 1:CONV_W, sl][None] * cur
    for s in range(1, CONV_W):
        rolled = pltpu.roll(xb, s, axis=1).reshape(nb, n_blk + 1, SUBLANES, HEAD_A)
        shifted = jnp.where(t_in_blk >= s, rolled[:, 1:].reshape(blk_shape), rolled[:, :-1].reshape(blk_shape))
        xc = xc + cw_ref[CONV_W - 1 - s:CONV_W - s, sl][None] * shifted
    xc = xc.reshape(rows, HEAD_A)
    xcb = xc.astype(BF16)
    r = _sigmoid_tanh(jnp.dot(xcb, wr, preferred_element_type=F32) + br)
    i = _sigmoid_tanh(jnp.dot(xcb, wi, preferred_element_type=F32) + bi)
    log_a = (-C_GATE * r) * softplus
    a = jnp.exp(log_a)
    th = jnp.tanh(log_a)
    one_m_a2 = -2.0 * th / (1.0 - th)
    b = jnp.where(one_m_a2 > 0.0, one_m_a2 * lax.rsqrt(one_m_a2), 0.0) * (i * xc)
    a = a.reshape(blk_shape)
    b = b.reshape(blk_shape)
    d = 1
    while d < SUBLANES:
        a_sh = pltpu.roll(a, d, axis=1)
        b_sh = pltpu.roll(b, d, axis=1)
        keep = t_in_blk >= d
        b = jnp.where(keep, a * b_sh + b, b)
        a = jnp.where(keep, a * a_sh, a)
        d *= 2
    a3 = a.reshape(nb, tt, HEAD_A)
    b3 = b.reshape(nb, tt, HEAD_A)
    h_blocks = []
    for k in range(n_blk):
        ks = slice(k * SUBLANES, (k + 1) * SUBLANES)
        h_k = a3[:, ks, :] * carry + b3[:, ks, :]
        carry = h_k[:, SUBLANES - 1:SUBLANES, :]
        h_blocks.append(h_k)
    h = jnp.concatenate(h_blocks, axis=1) if n_blk > 1 else h_blocks[0]
    return ga * h.reshape(rows, HEAD_A), carry


def _branch_a_kernel(x_ref, wxa_ref, wga_ref, conv0_ref, h0_ref, cw_ref, cb_ref, wr_ref, wi_ref,
                     br_ref, bi_ref, lam_ref, ya_ref, hlast_ref, convnew_ref, h_ref, *xp_refs,
                     n_seq, t_tile, nb, tt, tiles_per_seq):
    i = pl.program_id(1)
    hist = CONV_W - 1
    lo = CONV_PAD - hist
    n_col = HEADS_PER_DOT * HEAD_A
    pair_cols = [slice(pr * n_col, (pr + 1) * n_col) for pr in range(len(xp_refs))]

    @pl.when(i % tiles_per_seq == 0)
    def _():
        for xp_ref, ps in zip(xp_refs, pair_cols):
            xp_ref[:, 0:lo, :] = jnp.zeros((n_seq, lo, n_col), F32)
            xp_ref[:, lo:CONV_PAD, :] = conv0_ref[:, :, ps]
        h_ref[...] = h0_ref[...]

    z = -lam_ref[...]
    softplus = jnp.maximum(z, 0.0) + jnp.log1p(jnp.exp(-jnp.abs(z)))
    x = x_ref[...]

    def project(pr):
        ps = pair_cols[pr]
        xa = jnp.dot(x, wxa_ref[:, ps], preferred_element_type=F32)
        xp_refs[pr][:, CONV_PAD:CONV_PAD + t_tile, :] = xa.reshape(n_seq, t_tile, n_col)
        return _gelu(jnp.dot(x, wga_ref[:, ps], preferred_element_type=F32))

    ga_next = project(0)
    for pr, xp_ref in enumerate(xp_refs):
        ga = ga_next
        if pr + 1 < len(xp_refs):
            ga_next = project(pr + 1)
        for hh in range(HEADS_PER_DOT):
            hd = pr * HEADS_PER_DOT + hh
            xsl = slice(hh * HEAD_A, (hh + 1) * HEAD_A)
            sl = slice(hd * HEAD_A, (hd + 1) * HEAD_A)
            wr = wr_ref[hd].astype(BF16)
            wi = wi_ref[hd].astype(BF16)
            for us in range(n_seq // nb):
                seqs = slice(us * nb, (us + 1) * nb)
                carry = h_ref[seqs, :, sl]
                for ut in range(t_tile // tt):
                    r0 = us * nb * t_tile + ut * tt
                    y, carry = _lru_unit(
                        xp_ref, seqs, ut * tt, xsl, ga[r0:r0 + nb * tt, xsl], carry,
                        cw_ref, cb_ref, wr, wi, br_ref[:, sl], bi_ref[:, sl], softplus[:, sl], sl, nb, tt)
                    ya_ref[r0:r0 + nb * tt, sl] = y.astype(BF16)
                h_ref[seqs, :, sl] = carry

    for xp_ref in xp_refs:
        xp_ref[:, lo:CONV_PAD, :] = xp_ref[:, lo + t_tile:CONV_PAD + t_tile, :]

    @pl.when(i % tiles_per_seq == tiles_per_seq - 1)
    def _():
        hlast_ref[...] = h_ref[...]
        for xp_ref, ps in zip(xp_refs, pair_cols):
            convnew_ref[:, :, ps] = xp_ref[:, lo:CONV_PAD, :]


def _branch_a(grp, tm, tc, nb, tt, xn, w_ag, layer, conv0, h0, cw, cb, wr, wi, br, bi, lam, jobs=None):
    b, t = grp.b, grp.t
    t_tile = min(t, tm)
    n_seq = tm // t_tile
    tiles_per_seq = t // t_tile
    assert (nb == 1 or tt == t_tile) and n_seq % nb == 0 and t_tile % tt == 0
    hist = CONV_W - 1
    n_j = D_MODEL // tc
    seq_blk = lambda j, i: (i // tiles_per_seq, 0, j)
    vec = lambda: pl.BlockSpec((1, tc), lambda j, i: (0, j))
    head_w = lambda: pl.BlockSpec((tc // HEAD_A, HEAD_A, HEAD_A), lambda j, i: (j, 0, 0))
    return _hosted_call(
        functools.partial(_branch_a_kernel, n_seq=n_seq, t_tile=t_tile, nb=nb, tt=tt,
                          tiles_per_seq=tiles_per_seq),
        [xn, w_ag, w_ag, conv0, h0.reshape(b, 1, D_MODEL), cw, cb.reshape(1, D_MODEL),
         wr, wi, br.reshape(1, D_MODEL), bi.reshape(1, D_MODEL), lam.reshape(1, D_MODEL)],
        jobs, CAST_SLAB_BYTES,
        grid=(n_j, grp.m // tm),
        in_specs=[
            pl.BlockSpec((tm, D_MODEL), lambda j, i: (i, 0)),
            pl.BlockSpec((None, D_MODEL, tc), lambda j, i: (layer, 0, j)),
            pl.BlockSpec((None, D_MODEL, tc), lambda j, i: (layer, 0, n_j + j)),
            pl.BlockSpec((n_seq, hist, tc), seq_blk),
            pl.BlockSpec((n_seq, 1, tc), seq_blk),
            pl.BlockSpec((CONV_W, tc), lambda j, i: (0, j)),
            vec(), head_w(), head_w(), vec(), vec(), vec(),
        ],
        out_specs=[
            pl.BlockSpec((tm, tc), lambda j, i: (i, j)),
            pl.BlockSpec((n_seq, 1, tc), seq_blk),
            pl.BlockSpec((n_seq, hist, tc), seq_blk),
        ],
        out_shape=[
            jax.ShapeDtypeStruct((grp.m, D_MODEL), BF16),
            jax.ShapeDtypeStruct((b, 1, D_MODEL), F32),
            jax.ShapeDtypeStruct((b, hist, D_MODEL), F32),
        ],
        scratch_shapes=[pltpu.VMEM((n_seq, 1, tc), F32)]
        + [pltpu.VMEM((n_seq, CONV_PAD + t_tile, HEADS_PER_DOT * HEAD_A), F32)]
        * (tc // (HEADS_PER_DOT * HEAD_A)),
        blocks=[((tm, D_MODEL), BF16), ((D_MODEL, tc), BF16), ((D_MODEL, tc), BF16),
                ((tm, tc), BF16)] + [((n_seq, SUBLANES, tc), F32)] * 4,
        scratch=[((n_seq, CONV_PAD + t_tile, tc), F32), ((n_seq, SUBLANES, tc), F32),
                 ((tm, 2 * HEADS_PER_DOT * HEAD_A), F32)],
        name="branch_a",
    )


def _branch_b_kernel(u_ref, v_ref, gv_ref, ws_ref, bias_ref, yb_ref, wm_ref, *, n_chunks):
    @pl.when(pl.program_id(0) == 0)
    def _():
        tri = (lax.broadcasted_iota(jnp.int32, (CHUNK, CHUNK), 0)
               >= lax.broadcasted_iota(jnp.int32, (CHUNK, CHUNK), 1))
        for g in range(N_GROUPS_B):
            wm_ref[g] = jnp.where(tri, ws_ref[g], 0.0).astype(BF16)

    vn = _rms(v_ref[...].astype(F32), gv_ref[...]).astype(BF16)
    for g in range(N_GROUPS_B):
        cs = slice(g * GROUP_B, (g + 1) * GROUP_B)
        rhs = jnp.concatenate([vn[c * CHUNK:(c + 1) * CHUNK, cs] for c in range(n_chunks)], axis=1)
        mixed = jnp.dot(wm_ref[g], rhs, preferred_element_type=F32)
        for c in range(n_chunks):
            rs = slice(c * CHUNK, (c + 1) * CHUNK)
            mixed_c = mixed[:, c * GROUP_B:(c + 1) * GROUP_B] + bias_ref[:, cs]
            yb_ref[rs, cs] = (u_ref[rs, cs].astype(F32) * mixed_c).astype(BF16)


def _branch_b(grp, gel, gv, ws, bias_full):
    tm = 512
    return pl.pallas_call(
        functools.partial(_branch_b_kernel, n_chunks=tm // CHUNK),
        grid=(grp.m // tm,),
        in_specs=[
            pl.BlockSpec((tm, D_MODEL), lambda i: (i, 0)),
            pl.BlockSpec((tm, D_MODEL), lambda i: (i, 1)),
            pl.BlockSpec((1, D_MODEL), lambda i: (0, 0)),
            pl.BlockSpec((N_GROUPS_B, CHUNK, CHUNK), lambda i: (0, 0, 0)),
            pl.BlockSpec((CHUNK, D_MODEL), lambda i: (0, 0)),
        ],
        out_specs=pl.BlockSpec((tm, D_MODEL), lambda i: (i, 0)),
        out_shape=jax.ShapeDtypeStruct((grp.m, D_MODEL), BF16),
        scratch_shapes=[pltpu.VMEM((N_GROUPS_B, CHUNK, CHUNK), BF16)],
        compiler_params=_params(("arbitrary",),
                                [((tm, D_MODEL), BF16)] * 3 + [((N_GROUPS_B, CHUNK, CHUNK), F32),
                                                                ((CHUNK, D_MODEL), F32)],
                                [((tm, D_MODEL), F32)] * 2),
        name="branch_b",
    )(gel, gel, gv.reshape(1, D_MODEL), ws, bias_full)


def _branch_b_short_kernel(u_ref, v_ref, gv_ref, wc_ref, bias_ref, yb_ref, vn_ref, *, nb, t):
    vn = _rms(v_ref[...].astype(F32), gv_ref[...]).reshape(nb, t, D_MODEL)
    vn_ref[...] = vn
    mixed = bias_ref[...][None]
    for s in range(t):
        mixed = mixed + wc_ref[s][None] * vn[:, s:s + 1, :]
    yb = u_ref[...].astype(F32).reshape(nb, t, D_MODEL) * mixed
    yb_ref[...] = yb.reshape(nb * t, D_MODEL).astype(BF16)


def _branch_b_short(grp, gel, gv, ws, bs):
    b, t = grp.b, grp.t
    nb = 16
    rows = nb * t
    tri = jnp.tril(jnp.ones((t, t), F32))
    wc = jnp.repeat(jnp.transpose(ws[:, :t, :t] * tri, (2, 1, 0)), GROUP_B, axis=-1)
    bias = jnp.repeat(bs[:, :t].T, GROUP_B, axis=-1)
    return pl.pallas_call(
        functools.partial(_branch_b_short_kernel, nb=nb, t=t),
        grid=(b // nb,),
        in_specs=[
            pl.BlockSpec((rows, D_MODEL), lambda i: (i, 0)),
            pl.BlockSpec((rows, D_MODEL), lambda i: (i, 1)),
            pl.BlockSpec((1, D_MODEL), lambda i: (0, 0)),
            pl.BlockSpec((t, t, D_MODEL), lambda i: (0, 0, 0)),
            pl.BlockSpec((t, D_MODEL), lambda i: (0, 0)),
        ],
        out_specs=[
            pl.BlockSpec((rows, D_MODEL), lambda i: (i, 0)),
            pl.BlockSpec((nb, t, D_MODEL), lambda i: (i, 0, 0)),
        ],
        out_shape=[
            jax.ShapeDtypeStruct((grp.m, D_MODEL), BF16),
            jax.ShapeDtypeStruct((b, t, D_MODEL), F32),
        ],
        compiler_params=_params(("arbitrary",),
                                [((rows, D_MODEL), BF16)] * 3 + [((rows, D_MODEL), F32),
                                                                  ((t, t, D_MODEL), F32)],
                                [((rows, D_MODEL), F32)] * 3),
        name="branch_b_short",
    )(gel, gel, gv.reshape(1, D_MODEL), wc, bias)


def _merge_kernel(ya_ref, yb_ref, sa_ref, sb_ref, wpa_ref, wpb_ref, o_ref, wa_ref, wb_ref):
    @pl.when(pl.program_id(1) == 0)
    def _():
        wa_ref[...] = wpa_ref[...].astype(BF16)
        wb_ref[...] = wpb_ref[...].astype(BF16)

    pa = jnp.dot(ya_ref[...], wa_ref[...], preferred_element_type=F32)
    pb = jnp.dot(yb_ref[...], wb_ref[...], preferred_element_type=F32)
    o_ref[...] = (sa_ref[...].astype(F32) * pa + sb_ref[...].astype(F32) * pb).astype(BF16)


def _merge(grp, ya, yb, sig, wpa, wpb, layer, jobs=None):
    tm, tn = 1024, 512
    nj = D_MODEL // tn
    return _hosted_call(
        _merge_kernel, [ya, yb, sig, sig, wpa, wpb], jobs, CAST_SLAB_BYTES // 2,
        grid=(nj, grp.m // tm),
        in_specs=[
            pl.BlockSpec((tm, D_MODEL), lambda j, i: (i, 0)),
            pl.BlockSpec((tm, D_MODEL), lambda j, i: (i, 0)),
            pl.BlockSpec((tm, tn), lambda j, i: (i, j)),
            pl.BlockSpec((tm, tn), lambda j, i: (i, nj + j)),
            pl.BlockSpec((None, D_MODEL, tn), lambda j, i: (layer, 0, j)),
            pl.BlockSpec((None, D_MODEL, tn), lambda j, i: (layer, 0, j)),
        ],
        out_specs=[pl.BlockSpec((tm, tn), lambda j, i: (i, j))],
        out_shape=[jax.ShapeDtypeStruct((grp.m, D_MODEL), BF16)],
        scratch_shapes=[pltpu.VMEM((D_MODEL, tn), BF16), pltpu.VMEM((D_MODEL, tn), BF16)],
        blocks=[((tm, D_MODEL), BF16)] * 2 + [((tm, tn), BF16)] * 3 + [((D_MODEL, tn), F32)] * 2,
        scratch=[((D_MODEL, tn), BF16)] * 2 + [((tm, tn), F32)] * 2,
        name="merge",
    )[0]


def _post_math(y, x, gpost, gt, nxt):
    x1 = x + gt * _rms(y, gpost)
    if nxt is None:
        return x1, None
    gpre, one_plus_sc, sh = nxt
    return x1, _rms(x1, gpre) * one_plus_sc + sh


def _next_mods(gpre_ref, sc_ref, sh_ref, rows):
    return gpre_ref[...], _mod_rows(sc_ref, rows, 1.0), _mod_rows(sh_ref, rows)


def _oproj_post_kernel(m_ref, wo_ref, x_ref, gpost_ref, gt_ref, gpre_ref, sc_ref, sh_ref,
                       xo_ref, xn_ref):
    y = jnp.dot(m_ref[...], wo_ref[...], preferred_element_type=F32)
    rows = x_ref.shape[0]
    x1, xn = _post_math(y, x_ref[...], gpost_ref[...], _mod_rows(gt_ref, rows),
                        _next_mods(gpre_ref, sc_ref, sh_ref, rows))
    xo_ref[...] = x1
    xn_ref[...] = xn.astype(xn_ref.dtype)


def _oproj_post(grp, m, wo, layer, x, gpost, gt, gpre, sc, sh, xn_dtype, jobs=None):
    tm = 512
    vec = lambda: pl.BlockSpec((1, D_MODEL), lambda i: (0, 0))
    row = lambda: pl.BlockSpec((tm, D_MODEL), lambda i: (i, 0))
    return _hosted_call(
        _oproj_post_kernel,
        [m, wo, x, gpost.reshape(1, D_MODEL), gt, gpre.reshape(1, D_MODEL), sc, sh],
        jobs, CAST_SLAB_BYTES,
        grid=(grp.m // tm,),
        in_specs=[
            row(),
            pl.BlockSpec((None, D_MODEL, D_MODEL), lambda i: (layer, 0, 0),
                         pipeline_mode=pl.Buffered(1)),
            row(), vec(), grp.mod_spec(tm), vec(), grp.mod_spec(tm), grp.mod_spec(tm),
        ],
        out_specs=[row(), row()],
        out_shape=[
            jax.ShapeDtypeStruct((grp.m, D_MODEL), F32),
            jax.ShapeDtypeStruct((grp.m, D_MODEL), xn_dtype),
        ],
        scratch_shapes=[],
        blocks=[((tm, D_MODEL), F32)] * 5,
        scratch=[((D_MODEL, D_MODEL), BF16), ((tm, D_MODEL), F32)],
        name="oproj_post",
    )


def _post_kernel(y_ref, x_ref, gpost_ref, gt_ref, *rest, with_next):
    rows = x_ref.shape[0]
    if with_next:
        gpre_ref, sc_ref, sh_ref, xo_ref, xn_ref = rest
        nxt = _next_mods(gpre_ref, sc_ref, sh_ref, rows)
    else:
        (xo_ref,) = rest
        nxt = None
    x1, xn = _post_math(y_ref[...], x_ref[...], gpost_ref[...], _mod_rows(gt_ref, rows), nxt)
    xo_ref[...] = x1
    if with_next:
        xn_ref[...] = xn.astype(xn_ref.dtype)


def _post_specs(grp, tm, x, gpost, gt, nxt, imap):
    vec = lambda: pl.BlockSpec((1, D_MODEL), lambda *ids: (0, 0))
    row = lambda: pl.BlockSpec((tm, D_MODEL), imap)
    in_specs = [row(), vec(), grp.mod_spec(tm)]
    args = [x, gpost.reshape(1, D_MODEL), gt]
    out_specs = [row()]
    out_shape = [jax.ShapeDtypeStruct((grp.m, D_MODEL), F32)]
    if nxt is not None:
        gpre, sc, sh = nxt
        in_specs += [vec(), grp.mod_spec(tm), grp.mod_spec(tm)]
        args += [gpre.reshape(1, D_MODEL), sc, sh]
        out_specs.append(row())
        out_shape.append(jax.ShapeDtypeStruct((grp.m, D_MODEL), BF16))
    return in_specs, args, out_specs, out_shape


def _post(grp, y, x, gpost, gt, nxt):
    tm = 512
    imap = lambda i: (i, 0)
    in_specs, args, out_specs, out_shape = _post_specs(grp, tm, x, gpost, gt, nxt, imap)
    out = pl.pallas_call(
        functools.partial(_post_kernel, with_next=nxt is not None),
        grid=(grp.m // tm,),
        in_specs=[pl.BlockSpec((tm, D_MODEL), imap)] + in_specs,
        out_specs=out_specs,
        out_shape=out_shape,
        compiler_params=_params(("arbitrary",), [((tm, D_MODEL), F32)] * 8),
        name="post",
    )(y, *args)
    return (out[0], out[1]) if nxt is not None else (out[0], None)


def _swiglu_tile(x, wg_ref, wu_ref, wd_ref):
    g = jnp.dot(x, wg_ref[...], preferred_element_type=F32)
    u = jnp.dot(x, wu_ref[...], preferred_element_type=F32)
    h = _silu(g) * u
    return jnp.dot(h.astype(BF16), wd_ref[...], preferred_element_type=F32)


def _swiglu_kernel(x_ref, wg_ref, wu_ref, wd_ref, o_ref):
    @pl.when(pl.program_id(1) == 0)
    def _():
        o_ref[...] = jnp.zeros_like(o_ref)

    o_ref[...] += _swiglu_tile(x_ref[...], wg_ref, wu_ref, wd_ref)


def _swiglu(grp, x, wg, wu, wd):
    f = wg.shape[-1]
    tm, tf = 1024, 512
    return pl.pallas_call(
        _swiglu_kernel,
        grid=(grp.m // tm, f // tf),
        in_specs=[
            pl.BlockSpec((tm, D_MODEL), lambda i, j: (i, 0)),
            pl.BlockSpec((D_MODEL, tf), lambda i, j: (0, j)),
            pl.BlockSpec((D_MODEL, tf), lambda i, j: (0, j)),
            pl.BlockSpec((tf, D_MODEL), lambda i, j: (j, 0)),
        ],
        out_specs=pl.BlockSpec((tm, D_MODEL), lambda i, j: (i, 0)),
        out_shape=jax.ShapeDtypeStruct((grp.m, D_MODEL), F32),
        compiler_params=_params(("arbitrary", "arbitrary"),
                                [((tm, D_MODEL), BF16), ((tm, D_MODEL), F32)]
                                + [((D_MODEL, tf), BF16)] * 3,
                                [((tm, tf), F32)] * 4),
        name="swiglu_dense",
    )(x, wg, wu, wd)


META_E1, META_E2, META_R1, META_R2, META_W1, META_W2 = range(6)
DMA_UNROLL = 8


def _split_bf16(v):
    hi = v.astype(BF16)
    return hi, (v - hi.astype(F32)).astype(BF16)


def _router_kernel(x_ref, rw_ref, rb_ref, meta_ref, cnt_ref, tri_ref):
    tm = x_ref.shape[0]

    @pl.when(pl.program_id(0) == 0)
    def _():
        cnt_ref[...] = jnp.zeros_like(cnt_ref)
        tri_ref[...] = (lax.broadcasted_iota(jnp.int32, (tm, tm), 0)
                        >= lax.broadcasted_iota(jnp.int32, (tm, tm), 1)).astype(BF16)

    xh, xl = _split_bf16(x_ref[...])
    wh, wl = _split_bf16(rw_ref[...])
    logits = (jnp.dot(xh, wh, preferred_element_type=F32) + jnp.dot(xl, wh, preferred_element_type=F32)
              + jnp.dot(xh, wl, preferred_element_type=F32)) + rb_ref[...]
    lane = lax.broadcasted_iota(jnp.int32, logits.shape, 1)
    logits = jnp.where(lane < N_EXPERTS, logits, -jnp.inf)
    ex = jnp.exp(logits - jnp.max(logits, axis=-1, keepdims=True))
    p = ex / jnp.sum(ex, axis=-1, keepdims=True)
    p1 = jnp.max(p, axis=-1, keepdims=True)
    i1 = jnp.min(jnp.where(p == p1, lane, LANES), axis=-1, keepdims=True)
    rest = jnp.where(lane == i1, -1.0, p)
    p2 = jnp.max(rest, axis=-1, keepdims=True)
    i2 = jnp.min(jnp.where(rest == p2, lane, LANES), axis=-1, keepdims=True)
    denom = p1 + p2
    onehot = jnp.where(lane == i1, 1.0, 0.0) + jnp.where(lane == i2, 1.0, 0.0)
    cum = jnp.dot(tri_ref[...], onehot.astype(BF16), preferred_element_type=F32) + cnt_ref[...]
    cnt_ref[...] = cum[tm - 1:tm, :]
    r1 = jnp.sum(jnp.where(lane == i1, cum, 0.0), axis=-1, keepdims=True) - 1.0
    r2 = jnp.sum(jnp.where(lane == i2, cum, 0.0), axis=-1, keepdims=True) - 1.0
    meta = jnp.zeros_like(logits)
    for k, v in ((META_E1, i1.astype(F32)), (META_E2, i2.astype(F32)), (META_R1, r1), (META_R2, r2),
                 (META_W1, p1 / denom), (META_W2, p2 / denom)):
        meta = jnp.where(lane == k, v, meta)
    meta_ref[...] = meta


def _router(grp, x, rw_pad, rb_pad):
    tm = 512
    return pl.pallas_call(
        _router_kernel,
        grid=(grp.m // tm,),
        in_specs=[
            pl.BlockSpec((tm, D_MODEL), lambda i: (i, 0)),
            pl.BlockSpec((D_MODEL, LANES), lambda i: (0, 0)),
            pl.BlockSpec((1, LANES), lambda i: (0, 0)),
        ],
        out_specs=[
            pl.BlockSpec((tm, LANES), lambda i: (i, 0)),
            pl.BlockSpec((1, LANES), lambda i: (0, 0)),
        ],
        out_shape=[
            jax.ShapeDtypeStruct((grp.m, LANES), F32),
            jax.ShapeDtypeStruct((1, LANES), F32),
        ],
        scratch_shapes=[pltpu.VMEM((tm, tm), BF16)],
        compiler_params=_params(("arbitrary",),
                                [((tm, D_MODEL), F32), ((D_MODEL, LANES), F32), ((tm, LANES), F32)],
                                [((tm, tm), BF16), ((tm, D_MODEL), F32)]),
        name="router",
    )(x, rw_pad, rb_pad)


def _route_plan(metas, counts, tg, n_tiles):
    cnts = [c[0, :N_EXPERTS].astype(jnp.int32) for c in counts]
    total = sum(cnts)
    padded = (total + tg - 1) // tg * tg
    ends = jnp.cumsum(padded)
    offs = ends - padded
    slots = []
    for meta, before in zip(metas, [sum(cnts[:g], jnp.zeros_like(total)) for g in range(len(cnts))]):
        idx = meta[:, :META_W1].astype(jnp.int32)
        start = offs + before
        slots.append((start[idx[:, META_E1]] + idx[:, META_R1], start[idx[:, META_E2]] + idx[:, META_R2]))
    n_used = ends[-1] // tg
    tile = jnp.arange(n_tiles, dtype=jnp.int32)
    tile_c = jnp.minimum(tile, n_used - 1)
    tile_expert = jnp.sum(tile_c[:, None] * tg >= ends[None, :], axis=1).astype(jnp.int32)
    return slots, tile_expert, n_used.reshape(1).astype(jnp.int32)


def _dispatch_kernel(pos1_ref, pos2_ref, x_ref, xs_in_ref, xs_ref, sem):
    del xs_in_ref
    tm = x_ref.shape[0]
    base = pl.program_id(0) * tm

    def row_copy(t, slot):
        return pltpu.make_async_copy(x_ref.at[pl.ds(t, 1)], xs_ref.at[pl.ds(slot, 1)], sem)

    def start(t, c):
        row_copy(t, pos1_ref[base + t]).start()
        row_copy(t, pos2_ref[base + t]).start()
        return c

    def wait(t, c):
        row_copy(t, 0).wait()
        row_copy(t, 0).wait()
        return c

    lax.fori_loop(0, tm, start, 0, unroll=DMA_UNROLL)
    lax.fori_loop(0, tm, wait, 0, unroll=DMA_UNROLL)


def _dispatch(grp, x, pos1, pos2, xs):
    tm = 1024
    return pl.pallas_call(
        _dispatch_kernel,
        grid_spec=pltpu.PrefetchScalarGridSpec(
            num_scalar_prefetch=2,
            grid=(grp.m // tm,),
            in_specs=[
                pl.BlockSpec((tm, D_MODEL), lambda i, p1, p2: (i, 0)),
                pl.BlockSpec(memory_space=pl.ANY),
            ],
            out_specs=pl.BlockSpec(memory_space=pl.ANY),
            scratch_shapes=[pltpu.SemaphoreType.DMA(())],
        ),
        out_shape=jax.ShapeDtypeStruct(xs.shape, F32),
        input_output_aliases={3: 0},
        compiler_params=_params(("arbitrary",), [((tm, D_MODEL), F32)]),
        name="moe_dispatch",
    )(pos1, pos2, x, xs)


def _experts_kernel(te_ref, nu_ref, xs_ref, wg_ref, wu_ref, wd_ref, o_ref):
    del te_ref
    r, j = pl.program_id(0), pl.program_id(1)

    @pl.when(j == 0)
    def _():
        o_ref[...] = jnp.zeros_like(o_ref)

    @pl.when(r < nu_ref[0])
    def _():
        o_ref[...] += _swiglu_tile(xs_ref[...].astype(BF16), wg_ref, wu_ref, wd_ref)


def _experts(xs, tile_expert, n_used, wg, wu, wd, tg):
    n_rows = xs.shape[0]
    f = wg.shape[-1]
    tf = 512
    n_j = f // tf

    def used(r, nu):
        return r < nu[0]

    def x_map(r, j, te, nu):
        return (jnp.minimum(r, nu[0] - 1), 0)

    def up_map(r, j, te, nu):
        return (te[r], 0, jnp.where(used(r, nu), j, n_j - 1))

    def down_map(r, j, te, nu):
        return (te[r], jnp.where(used(r, nu), j, n_j - 1), 0)

    return pl.pallas_call(
        _experts_kernel,
        grid_spec=pltpu.PrefetchScalarGridSpec(
            num_scalar_prefetch=2,
            grid=(n_rows // tg, n_j),
            in_specs=[
                pl.BlockSpec((tg, D_MODEL), x_map),
                pl.BlockSpec((None, D_MODEL, tf), up_map),
                pl.BlockSpec((None, D_MODEL, tf), up_map),
                pl.BlockSpec((None, tf, D_MODEL), down_map),
            ],
            out_specs=pl.BlockSpec((tg, D_MODEL), lambda r, j, te, nu: (r, 0)),
        ),
        out_shape=jax.ShapeDtypeStruct((n_rows, D_MODEL), F32),
        compiler_params=_params(("arbitrary", "arbitrary"),
                                [((tg, D_MODEL), F32)] * 2 + [((D_MODEL, tf), BF16)] * 3,
                                [((tg, tf), F32)] * 4 + [((tg, D_MODEL), BF16)]),
        name="moe_experts",
    )(tile_expert, n_used, xs, wg, wu, wd)


def _combine_post_kernel(pos1_ref, pos2_ref, ys_ref, meta_ref, x_ref, gpost_ref, gt_ref, *rest,
                         with_next):
    *rest, buf1_ref, buf2_ref, sem = rest
    tm = x_ref.shape[0]
    step, n_steps = pl.program_id(0), pl.num_programs(0)
    cur = step % 2

    def row_copy(row, buf_ref, half, t):
        return pltpu.make_async_copy(ys_ref.at[pl.ds(row, 1)], buf_ref.at[half, pl.ds(t, 1)],
                                     sem.at[half])

    def gather(tile, half):
        def start(t, c):
            row_copy(pos1_ref[tile * tm + t], buf1_ref, half, t).start()
            row_copy(pos2_ref[tile * tm + t], buf2_ref, half, t).start()
            return c
        lax.fori_loop(0, tm, start, 0, unroll=DMA_UNROLL)

    @pl.when(step == 0)
    def _():
        gather(0, 0)

    @pl.when(step + 1 < n_steps)
    def _():
        gather(step + 1, 1 - cur)

    def wait(t, c):
        row_copy(0, buf1_ref, cur, t).wait()
        row_copy(0, buf2_ref, cur, t).wait()
        return c

    lax.fori_loop(0, tm, wait, 0, unroll=DMA_UNROLL)
    meta = meta_ref[...]
    y = meta[:, META_W1:META_W1 + 1] * buf1_ref[cur] + meta[:, META_W2:META_W2 + 1] * buf2_ref[cur]
    if with_next:
        gpre_ref, sc_ref, sh_ref, xo_ref, xn_ref = rest
        nxt = _next_mods(gpre_ref, sc_ref, sh_ref, tm)
    else:
        (xo_ref,) = rest
        nxt = None
    x1, xn = _post_math(y, x_ref[...], gpost_ref[...], _mod_rows(gt_ref, tm), nxt)
    xo_ref[...] = x1
    if with_next:
        xn_ref[...] = xn.astype(xn_ref.dtype)


def _combine_post(grp, ys, pos1, pos2, meta, x, gpost, gt, nxt):
    tm = 256
    imap = lambda i, p1, p2: (i, 0)
    in_specs, args, out_specs, out_shape = _post_specs(grp, tm, x, gpost, gt, nxt, imap)
    out = pl.pallas_call(
        functools.partial(_combine_post_kernel, with_next=nxt is not None),
        grid_spec=pltpu.PrefetchScalarGridSpec(
            num_scalar_prefetch=2,
            grid=(grp.m // tm,),
            in_specs=[pl.BlockSpec(memory_space=pl.ANY), pl.BlockSpec((tm, LANES), imap)] + in_specs,
            out_specs=out_specs,
            scratch_shapes=[pltpu.VMEM((2, tm, D_MODEL), F32), pltpu.VMEM((2, tm, D_MODEL), F32),
                            pltpu.SemaphoreType.DMA((2,))],
        ),
        out_shape=out_shape,
        compiler_params=_params(("arbitrary",), [((tm, D_MODEL), F32)] * 7,
                                [((tm, D_MODEL), F32)] * 4),
        name="moe_combine_post",
    )(pos1, pos2, ys, meta, *args)
    return (out[0], out[1]) if nxt is not None else (out[0], None)


def _moe_post(grps, xns, p, j, xs_res, gpost, gts, nxts):
    tg = 512
    n_tiles = 2 * sum(g.m for g in grps) // tg + N_EXPERTS
    rw_pad = jnp.pad(p['router_w'][j], ((0, 0), (0, LANES - N_EXPERTS)))
    rb_pad = jnp.pad(p['router_b'][j], (0, LANES - N_EXPERTS)).reshape(1, LANES)
    routed = [_router(g, xn, rw_pad, rb_pad) for g, xn in zip(grps, xns)]
    metas, counts = zip(*routed)
    slots, tile_expert, n_used = _route_plan(metas, counts, tg, n_tiles)
    xs = jnp.zeros((n_tiles * tg, D_MODEL), F32)
    for g, xn, (pos1, pos2) in zip(grps, xns, slots):
        xs = _dispatch(g, xn, pos1, pos2, xs)
    wg, wu, wd = (p['jobs'].result(k)[j] for k in ('moe_wg', 'moe_wu', 'moe_wd'))
    ys = _experts(xs, tile_expert, n_used, wg, wu, wd, tg)
    return [_combine_post(g, ys, pos1, pos2, meta, x, gpost, gt, nxt)
            for g, (pos1, pos2), meta, x, gt, nxt in zip(grps, slots, metas, xs_res, gts, nxts)]


def _mixer(grp, l, x, xn, mods, h0, conv0, p, a_tiles, xn_dtype):
    _, _, gt1, sh2, sc2, _ = mods
    w_in = p['w_in']
    ya, h_last, conv_new = _branch_a(
        grp, *a_tiles, xn, p['w_ag_bf'], l, conv0, h0, p['conv_w'][l], p['conv_b'][l],
        p['w_r'][l], p['w_i'][l], p['b_r'][l], p['b_i'][l], p['lru_lambda'][l], p['jobs'])
    gel = _mm_act(grp, xn, w_in, l, 2 * D_MODEL, 2 * D_MODEL, _gelu, BF16, "in_proj_gelu", p['jobs'])
    sig = _mm_act(grp, xn, w_in, l, 4 * D_MODEL, 2 * D_MODEL, _sigmoid, BF16, "in_proj_sigmoid", p['jobs'])
    if grp.t >= CHUNK:
        bias_full = jnp.repeat(p['b_s'][l].T, GROUP_B, axis=-1)
        yb = _branch_b(grp, gel, p['g_v'][l], p['w_s'][l], bias_full)
        vn = None
    else:
        yb, vn = _branch_b_short(grp, gel, p['g_v'][l], p['w_s'][l], p['b_s'][l])
    m = _merge(grp, ya, yb, sig, p['w_pa'], p['w_pb'], l, p['jobs'])
    x, xn2 = _oproj_post(grp, m, p['w_o_bf'], l, x, p['g_post_mix'][l], gt1,
                         p['g_pre_ffn'][l], sc2, sh2, xn_dtype, p['jobs'])
    return x, xn2, h_last.reshape(grp.b, D_MODEL), conv_new, vn


def _trunk(grps, xs, mods, h0s, conv0s, p, tiles):
    depth = p['w_in'].shape[0]
    n_g = len(grps)
    xns = [_prenorm(g, x, p['g_pre_mix'][0], mods[i][0][1], mods[i][0][0])
           for i, (g, x) in enumerate(zip(grps, xs))]
    states = [([], [], []) for _ in grps]
    for l in range(depth):
        dense = l % 2 == 0
        j = l // 2
        xn2s = []
        for i, g in enumerate(grps):
            xs[i], xn2, h_last, conv_new, vn = _mixer(
                g, l, xs[i], xns[i], mods[i][l], h0s[i][l], conv0s[i][l], p, tiles[i],
                BF16 if dense else F32)
            xn2s.append(xn2)
            for acc, v in zip(states[i], (h_last, conv_new, vn)):
                acc.append(v)
        if l + 1 < depth:
            nxts = [(p['g_pre_mix'][l + 1], mods[i][l + 1][1], mods[i][l + 1][0]) for i in range(n_g)]
        else:
            nxts = [None] * n_g
        gt2s = [mods[i][l][5] for i in range(n_g)]
        gpost = p['g_post_ffn'][l]
        if dense:
            outs = []
            for i, g in enumerate(grps):
                wg, wu, wd = (p['jobs'].result(k)[j] for k in ('ffn_wg', 'ffn_wu', 'ffn_wd'))
                y = _swiglu(g, xn2s[i], wg, wu, wd)
                outs.append(_post(g, y, xs[i], gpost, gt2s[i], nxts[i]))
        else:
            outs = _moe_post(grps, xn2s, p, j, xs, gpost, gt2s, nxts)
        xs = [o[0] for o in outs]
        xns = [o[1] for o in outs]
    return xs, states


def kernel(x_prompt, x_sample, c_prompt, c_sample, state_lru_h, state_lru_conv, w_ada, b_ada, g_pre_mix, g_post_mix, g_pre_ffn, g_post_ffn, w_in, conv_w, conv_b, w_r, b_r, w_i, b_i, lru_lambda, g_v, w_s, b_s, w_pa, w_pb, w_o, ffn_wg, ffn_wu, ffn_wd, router_w, router_b, moe_wg, moe_wu, moe_wd):
    p = dict(g_pre_mix=g_pre_mix, g_post_mix=g_post_mix, g_pre_ffn=g_pre_ffn, g_post_ffn=g_post_ffn,
             w_in=w_in, conv_w=conv_w, conv_b=conv_b, w_r=w_r, b_r=b_r, w_i=w_i, b_i=b_i,
             lru_lambda=lru_lambda, g_v=g_v, w_s=w_s, b_s=b_s, w_pa=w_pa, w_pb=w_pb,
             router_w=router_w, router_b=router_b, w_o_bf=w_o.astype(BF16),
             w_ag_bf=w_in[:, :, :2 * D_MODEL].astype(BF16),
             jobs=_CastJobs(dict(moe_wg=moe_wg, moe_wu=moe_wu, moe_wd=moe_wd,
                                 ffn_wg=ffn_wg, ffn_wu=ffn_wu, ffn_wd=ffn_wd)))
    depth = w_in.shape[0]
    bp, tp, _ = x_prompt.shape
    bs, ts, _ = x_sample.shape
    assert tp % CHUNK == 0 and ts < CHUNK and ts % SUBLANES == 0

    n_c = bp + bs
    r_pad = -n_c % 16
    c_all = jnp.concatenate([c_sample, c_prompt, jnp.zeros((r_pad, D_MODEL), F32)], axis=0)
    mod = _ada(c_all, w_ada, b_ada)
    mods_s = [[mod[l, k, :bs] for k in range(6)] for l in range(depth)]
    mods_p = [[mod[l, k, bs:n_c].reshape(bp, 1, D_MODEL) for k in range(6)] for l in range(depth)]

    grp_p = _Group(bp, tp, short=False)
    grp_s = _Group(bs, ts, short=True)
    zeros_h = jnp.zeros((depth, bp, D_MODEL), F32)
    zeros_conv = jnp.zeros((depth, bp, CONV_W - 1, D_MODEL), F32)

    (y_p, y_s), (st_p, st_s) = _trunk(
        [grp_p, grp_s],
        [x_prompt.reshape(bp * tp, D_MODEL), x_sample.reshape(bs * ts, D_MODEL)],
        [mods_p, mods_s], [zeros_h, state_lru_h], [zeros_conv, state_lru_conv], p,
        tiles=[(1024, 1024, 1, 256), (512, 1024, 16, ts)])
    return (y_p.reshape(bp, tp, D_MODEL), y_s.reshape(bs, ts, D_MODEL),
            jnp.stack(st_p[0]), jnp.stack(st_p[1]), jnp.stack(st_s[0]), jnp.stack(st_s[1]),
            jnp.stack(st_s[2]))
```

```python
import functools

import jax
import jax.numpy as jnp
from jax import lax
from jax.experimental import pallas as pl
from jax.experimental.pallas import tpu as pltpu

F32 = jnp.float32
BF16 = jnp.bfloat16

D_MODEL = 2048
N_HEADS_A = 16
HEAD_A = D_MODEL // N_HEADS_A
CONV_W = 4
C_GATE = 8.0
N_GROUPS_B = 16
GROUP_B = D_MODEL // N_GROUPS_B
CHUNK = 128
N_EXPERTS = 8
EPS = 1e-6

LANES = 128
SUBLANES = 8
VMEM_CAP = 58 * 2**20
CONV_PAD = SUBLANES
CAST_SLAB_BYTES = 4 * 2**20


def _nbytes(shape, dtype):
    n = 1
    for s in shape:
        n *= s
    return n * jnp.dtype(dtype).itemsize


def _params(sem, blocks, scratch=()):
    need = 2 * sum(_nbytes(s, d) for s, d in blocks) + sum(_nbytes(s, d) for s, d in scratch)
    limit = min(VMEM_CAP, need + 16 * 2**20)
    return pltpu.CompilerParams(dimension_semantics=sem, vmem_limit_bytes=limit)


def _rms(x, g):
    return x * lax.rsqrt(jnp.mean(x * x, axis=-1, keepdims=True) + EPS) * g


def _sigmoid(x):
    return 1.0 / (1.0 + jnp.exp(-x))


def _sigmoid_tanh(x):
    return 0.5 * jnp.tanh(0.5 * x) + 0.5


def _silu(x):
    return x * _sigmoid(x)


def _gelu(x):
    return jax.nn.gelu(x)


class _Group:
    def __init__(self, b, t, short):
        self.b, self.t, self.m = b, t, b * t
        self.short = short

    def mod_spec(self, tm):
        if self.short:
            return pl.BlockSpec((tm // self.t, D_MODEL), lambda *ids: (ids[0], 0))
        t = self.t
        return pl.BlockSpec((None, 1, D_MODEL), lambda *ids: ((ids[0] * tm) // t, 0, 0))


def _mod_rows(ref, rows, offset=None):
    n = ref.shape[0]
    vals = ref[...] if offset is None else ref[...] + offset
    if n == 1:
        return vals
    rep = rows // n
    return jnp.concatenate([jnp.broadcast_to(vals[r:r + 1, :], (rep, D_MODEL)) for r in range(n)], axis=0)


def _ada_kernel(c_ref, w_ref, b_ref, o_ref):
    s = _silu(c_ref[...]).astype(BF16)
    o_ref[...] = jnp.dot(s, w_ref[...].astype(BF16), preferred_element_type=F32) + b_ref[...]


def _ada(c, w_ada, b_ada):
    depth, _, n = w_ada.shape
    r = c.shape[0]
    tn = 1024
    per_item = D_MODEL // tn
    return pl.pallas_call(
        _ada_kernel,
        grid=(depth, n // tn),
        in_specs=[
            pl.BlockSpec((r, D_MODEL), lambda l, j: (0, 0)),
            pl.BlockSpec((None, D_MODEL, tn), lambda l, j: (l, 0, j)),
            pl.BlockSpec((None, 1, tn), lambda l, j: (l, 0, j)),
        ],
        out_specs=pl.BlockSpec((None, None, r, tn), lambda l, j: (l, j // per_item, 0, j % per_item)),
        out_shape=jax.ShapeDtypeStruct((depth, n // D_MODEL, r, D_MODEL), F32),
        compiler_params=_params(("arbitrary", "arbitrary"),
                                [((r, D_MODEL), F32), ((D_MODEL, tn), F32), ((r, tn), F32)],
                                [((D_MODEL, tn), BF16)]),
        name="ada_mod",
    )(c, w_ada, b_ada.reshape(depth, 1, n))


def _prenorm_kernel(x_ref, g_ref, sc_ref, sh_ref, o_ref):
    rows = x_ref.shape[0]
    o_ref[...] = (_rms(x_ref[...], g_ref[...]) * _mod_rows(sc_ref, rows, 1.0)
                  + _mod_rows(sh_ref, rows)).astype(BF16)


def _prenorm(grp, x, g, sc, sh):
    tm = 256 if grp.short else 512
    return pl.pallas_call(
        _prenorm_kernel,
        grid=(grp.m // tm,),
        in_specs=[
            pl.BlockSpec((tm, D_MODEL), lambda i: (i, 0)),
            pl.BlockSpec((1, D_MODEL), lambda i: (0, 0)),
            grp.mod_spec(tm), grp.mod_spec(tm),
        ],
        out_specs=pl.BlockSpec((tm, D_MODEL), lambda i: (i, 0)),
        out_shape=jax.ShapeDtypeStruct((grp.m, D_MODEL), BF16),
        compiler_params=_params(("arbitrary",), [((tm, D_MODEL), F32)] * 4),
        name="prenorm",
    )(x, g.reshape(1, D_MODEL), sc, sh)


class _CastJobs:
    def __init__(self, arrays):
        self.shapes = {k: v.shape for k, v in arrays.items()}
        self.pending = [(k, v.reshape(-1, v.shape[-1])) for k, v in arrays.items()]
        self.done = {}

    def take(self, steps, slab_limit):
        for n, (name, src) in enumerate(self.pending):
            rows = src.shape[0] // steps
            if src.shape[0] % (steps * 2 * SUBLANES) == 0 and _nbytes((rows, src.shape[1]), F32) <= slab_limit:
                return self.pending.pop(n)
        return None

    def put(self, name, out):
        self.done[name] = out.reshape(self.shapes[name])

    def result(self, name):
        for n, (k, src) in enumerate(self.pending):
            if k == name:
                self.pending.pop(n)
                self.put(name, src.astype(BF16))
                break
        return self.done[name]


def _hosted_call(kernel, args, jobs, slab_limit, *, grid, in_specs, out_specs, out_shape,
                 scratch_shapes, blocks, scratch, name):
    steps = 1
    for n in grid:
        steps *= n
    job = jobs.take(steps, slab_limit) if jobs is not None else None
    if job is not None:
        n_in, n_out = len(in_specs), len(out_specs)
        src = job[1]
        slab = (src.shape[0] // steps, src.shape[1])
        strides = [1] * len(grid)
        for k in range(len(grid) - 2, -1, -1):
            strides[k] = strides[k + 1] * grid[k + 1]
        spec = pl.BlockSpec(slab, lambda *ids: (sum(i * s for i, s in zip(ids, strides)), 0))
        in_specs, out_specs = [*in_specs, spec], [*out_specs, spec]
        out_shape = [*out_shape, jax.ShapeDtypeStruct(src.shape, BF16)]
        args, blocks = [*args, src], [*blocks, (slab, F32), (slab, BF16)]
        inner = kernel

        def kernel(*refs):
            refs[n_in + 1 + n_out][...] = refs[n_in][...].astype(BF16)
            inner(*refs[:n_in], *refs[n_in + 1:n_in + 1 + n_out], *refs[n_in + 2 + n_out:])

    out = pl.pallas_call(
        kernel, grid=grid, in_specs=in_specs, out_specs=out_specs, out_shape=out_shape,
        scratch_shapes=scratch_shapes,
        compiler_params=_params(("arbitrary",) * len(grid), blocks, scratch),
        name=name,
    )(*args)
    if job is not None:
        jobs.put(job[0], out[-1])
        out = out[:-1]
    return list(out)


def _mm_act_kernel(x_ref, w_ref, o_ref, wbf_ref, *, act):
    @pl.when(pl.program_id(1) == 0)
    def _():
        wbf_ref[...] = w_ref[...].astype(BF16)

    y = jnp.dot(x_ref[...], wbf_ref[...], preferred_element_type=F32)
    o_ref[...] = act(y).astype(o_ref.dtype)


def _mm_act(grp, x, w, layer, col0, ncols, act, out_dtype, name, jobs=None):
    tm, tn = 1024, 1024
    c0 = col0 // tn
    return _hosted_call(
        functools.partial(_mm_act_kernel, act=act), [x, w], jobs, CAST_SLAB_BYTES,
        grid=(ncols // tn, grp.m // tm),
        in_specs=[
            pl.BlockSpec((tm, D_MODEL), lambda j, i: (i, 0)),
            pl.BlockSpec((None, D_MODEL, tn), lambda j, i: (layer, 0, c0 + j)),
        ],
        out_specs=[pl.BlockSpec((tm, tn), lambda j, i: (i, j))],
        out_shape=[jax.ShapeDtypeStruct((grp.m, ncols), out_dtype)],
        scratch_shapes=[pltpu.VMEM((D_MODEL, tn), BF16)],
        blocks=[((tm, D_MODEL), BF16), ((D_MODEL, tn), F32), ((tm, tn), F32)],
        scratch=[((D_MODEL, tn), BF16), ((tm, tn), F32)],
        name=name,
    )[0]


HEADS_PER_DOT = 2


def _lru_unit(xp_ref, seqs, t0, xsl, ga, carry, cw_ref, cb_ref, wr, wi, br, bi, softplus, sl, nb, tt):
    rows = nb * tt
    n_blk = tt // SUBLANES
    blk_shape = (nb * n_blk, SUBLANES, HEAD_A)
    t_in_blk = lax.broadcasted_iota(jnp.int32, blk_shape, 1)
    xb = xp_ref[seqs, t0:t0 + CONV_PAD + tt, xsl].reshape(nb * (n_blk + 1), SUBLANES, HEAD_A)
    cur = xb.reshape(nb, n_blk + 1, SUBLANES, HEAD_A)[:, 1:].reshape(blk_shape)
    xc = cb_ref[:, sl][None] + cw_ref[CONV_W - 1:CONV_W, sl][None] * cur
    for s in range(1, CONV_W):
        rolled = pltpu.roll(xb, s, axis=1).reshape(nb, n_blk + 1, SUBLANES, HEAD_A)
        shifted = jnp.where(t_in_blk >= s, rolled[:, 1:].reshape(blk_shape), rolled[:, :-1].reshape(blk_shape))
        xc = xc + cw_ref[CONV_W - 1 - s:CONV_W - s, sl][None] * shifted
    xc = xc.reshape(rows, HEAD_A)
    xcb = xc.astype(BF16)
    r = _sigmoid_tanh(jnp.dot(xcb, wr, preferred_element_type=F32) + br)
    i = _sigmoid_tanh(jnp.dot(xcb, wi, preferred_element_type=F32) + bi)
    log_a = (-C_GATE * r) * softplus
    a = jnp.exp(log_a)
    th = jnp.tanh(log_a)
    one_m_a2 = -2.0 * th / (1.0 - th)
    b = jnp.where(one_m_a2 > 0.0, one_m_a2 * lax.rsqrt(one_m_a2), 0.0) * (i * xc)
    a = a.reshape(blk_shape)
    b = b.reshape(blk_shape)
    d = 1
    while d < SUBLANES:
        a_sh = pltpu.roll(a, d, axis=1)
        b_sh = pltpu.roll(b, d, axis=1)
        keep = t_in_blk >= d
        b = jnp.where(keep, a * b_sh + b, b)
        a = jnp.where(keep, a * a_sh, a)
        d *= 2
    a3 = a.reshape(nb, tt, HEAD_A)
    b3 = b.reshape(nb, tt, HEAD_A)
    h_blocks = []
    for k in range(n_blk):
        ks = slice(k * SUBLANES, (k + 1) * SUBLANES)
        h_k = a3[:, ks, :] * carry + b3[:, ks, :]
        carry = h_k[:, SUBLANES - 1:SUBLANES, :]
        h_blocks.append(h_k)
    h = jnp.concatenate(h_blocks, axis=1) if n_blk > 1 else h_blocks[0]
    return ga * h.reshape(rows, HEAD_A), carry


def _branch_a_kernel(x_ref, wxa_ref, wga_ref, conv0_ref, h0_ref, cw_ref, cb_ref, wr_ref, wi_ref,
                     br_ref, bi_ref, lam_ref, ya_ref, hlast_ref, convnew_ref, h_ref, *xp_refs,
                     n_seq, t_tile, nb, tt, tiles_per_seq):
    i = pl.program_id(1)
    hist = CONV_W - 1
    lo = CONV_PAD - hist
    n_col = HEADS_PER_DOT * HEAD_A
    pair_cols = [slice(pr * n_col, (pr + 1) * n_col) for pr in range(len(xp_refs))]

    @pl.when(i % tiles_per_seq == 0)
    def _():
        for xp_ref, ps in zip(xp_refs, pair_cols):
            xp_ref[:, 0:lo, :] = jnp.zeros((n_seq, lo, n_col), F32)
            xp_ref[:, lo:CONV_PAD, :] = conv0_ref[:, :, ps]
        h_ref[...] = h0_ref[...]

    z = -lam_ref[...]
    softplus = jnp.maximum(z, 0.0) + jnp.log1p(jnp.exp(-jnp.abs(z)))
    x = x_ref[...]

    def project(pr):
        ps = pair_cols[pr]
        xa = jnp.dot(x, wxa_ref[:, ps], preferred_element_type=F32)
        xp_refs[pr][:, CONV_PAD:CONV_PAD + t_tile, :] = xa.reshape(n_seq, t_tile, n_col)
        return _gelu(jnp.dot(x, wga_ref[:, ps], preferred_element_type=F32))

    ga_next = project(0)
    for pr, xp_ref in enumerate(xp_refs):
        ga = ga_next
        if pr + 1 < len(xp_refs):
            ga_next = project(pr + 1)
        for hh in range(HEADS_PER_DOT):
            hd = pr * HEADS_PER_DOT + hh
            xsl = slice(hh * HEAD_A, (hh + 1) * HEAD_A)
            sl = slice(hd * HEAD_A, (hd + 1) * HEAD_A)
            wr = wr_ref[hd].astype(BF16)
            wi = wi_ref[hd].astype(BF16)
            for us in range(n_seq // nb):
                seqs = slice(us * nb, (us + 1) * nb)
                carry = h_ref[seqs, :, sl]
                for ut in range(t_tile // tt):
                    r0 = us * nb * t_tile + ut * tt
                    y, carry = _lru_unit(
                        xp_ref, seqs, ut * tt, xsl, ga[r0:r0 + nb * tt, xsl], carry,
                        cw_ref, cb_ref, wr, wi, br_ref[:, sl], bi_ref[:, sl], softplus[:, sl], sl, nb, tt)
                    ya_ref[r0:r0 + nb * tt, sl] = y.astype(BF16)
                h_ref[seqs, :, sl] = carry

    for xp_ref in xp_refs:
        xp_ref[:, lo:CONV_PAD, :] = xp_ref[:, lo + t_tile:CONV_PAD + t_tile, :]

    @pl.when(i % tiles_per_seq == tiles_per_seq - 1)
    def _():
        hlast_ref[...] = h_ref[...]
        for xp_ref, ps in zip(xp_refs, pair_cols):
            convnew_ref[:, :, ps] = xp_ref[:, lo:CONV_PAD, :]


def _branch_a(grp, tm, tc, nb, tt, xn, w_ag, layer, conv0, h0, cw, cb, wr, wi, br, bi, lam, jobs=None):
    b, t = grp.b, grp.t
    t_tile = min(t, tm)
    n_seq = tm // t_tile
    tiles_per_seq = t // t_tile
    assert (nb == 1 or tt == t_tile) and n_seq % nb == 0 and t_tile % tt == 0
    hist = CONV_W - 1
    n_j = D_MODEL // tc
    seq_blk = lambda j, i: (i // tiles_per_seq, 0, j)
    vec = lambda: pl.BlockSpec((1, tc), lambda j, i: (0, j))
    head_w = lambda: pl.BlockSpec((tc // HEAD_A, HEAD_A, HEAD_A), lambda j, i: (j, 0, 0))
    return _hosted_call(
        functools.partial(_branch_a_kernel, n_seq=n_seq, t_tile=t_tile, nb=nb, tt=tt,
                          tiles_per_seq=tiles_per_seq),
        [xn, w_ag, w_ag, conv0, h0.reshape(b, 1, D_MODEL), cw, cb.reshape(1, D_MODEL),
         wr, wi, br.reshape(1, D_MODEL), bi.reshape(1, D_MODEL), lam.reshape(1, D_MODEL)],
        jobs, CAST_SLAB_BYTES,
        grid=(n_j, grp.m // tm),
        in_specs=[
            pl.BlockSpec((tm, D_MODEL), lambda j, i: (i, 0)),
            pl.BlockSpec((None, D_MODEL, tc), lambda j, i: (layer, 0, j)),
            pl.BlockSpec((None, D_MODEL, tc), lambda j, i: (layer, 0, n_j + j)),
            pl.BlockSpec((n_seq, hist, tc), seq_blk),
            pl.BlockSpec((n_seq, 1, tc), seq_blk),
            pl.BlockSpec((CONV_W, tc), lambda j, i: (0, j)),
            vec(), head_w(), head_w(), vec(), vec(), vec(),
        ],
        out_specs=[
            pl.BlockSpec((tm, tc), lambda j, i: (i, j)),
            pl.BlockSpec((n_seq, 1, tc), seq_blk),
            pl.BlockSpec((n_seq, hist, tc), seq_blk),
        ],
        out_shape=[
            jax.ShapeDtypeStruct((grp.m, D_MODEL), BF16),
            jax.ShapeDtypeStruct((b, 1, D_MODEL), F32),
            jax.ShapeDtypeStruct((b, hist, D_MODEL), F32),
        ],
        scratch_shapes=[pltpu.VMEM((n_seq, 1, tc), F32)]
        + [pltpu.VMEM((n_seq, CONV_PAD + t_tile, HEADS_PER_DOT * HEAD_A), F32)]
        * (tc // (HEADS_PER_DOT * HEAD_A)),
        blocks=[((tm, D_MODEL), BF16), ((D_MODEL, tc), BF16), ((D_MODEL, tc), BF16),
                ((tm, tc), BF16)] + [((n_seq, SUBLANES, tc), F32)] * 4,
        scratch=[((n_seq, CONV_PAD + t_tile, tc), F32), ((n_seq, SUBLANES, tc), F32),
                 ((tm, 2 * HEADS_PER_DOT * HEAD_A), F32)],
        name="branch_a",
    )


def _branch_b_kernel(u_ref, v_ref, gv_ref, ws_ref, bias_ref, yb_ref, wm_ref, *, n_chunks):
    @pl.when(pl.program_id(0) == 0)
    def _():
        tri = (lax.broadcasted_iota(jnp.int32, (CHUNK, CHUNK), 0)
               >= lax.broadcasted_iota(jnp.int32, (CHUNK, CHUNK), 1))
        for g in range(N_GROUPS_B):
            wm_ref[g] = jnp.where(tri, ws_ref[g], 0.0).astype(BF16)

    vn = _rms(v_ref[...].astype(F32), gv_ref[...]).astype(BF16)
    for g in range(N_GROUPS_B):
        cs = slice(g * GROUP_B, (g + 1) * GROUP_B)
        rhs = jnp.concatenate([vn[c * CHUNK:(c + 1) * CHUNK, cs] for c in range(n_chunks)], axis=1)
        mixed = jnp.dot(wm_ref[g], rhs, preferred_element_type=F32)
        for c in range(n_chunks):
            rs = slice(c * CHUNK, (c + 1) * CHUNK)
            mixed_c = mixed[:, c * GROUP_B:(c + 1) * GROUP_B] + bias_ref[:, cs]
            yb_ref[rs, cs] = (u_ref[rs, cs].astype(F32) * mixed_c).astype(BF16)


def _branch_b(grp, gel, gv, ws, bias_full):
    tm = 512
    return pl.pallas_call(
        functools.partial(_branch_b_kernel, n_chunks=tm // CHUNK),
        grid=(grp.m // tm,),
        in_specs=[
            pl.BlockSpec((tm, D_MODEL), lambda i: (i, 0)),
            pl.BlockSpec((tm, D_MODEL), lambda i: (i, 1)),
            pl.BlockSpec((1, D_MODEL), lambda i: (0, 0)),
            pl.BlockSpec((N_GROUPS_B, CHUNK, CHUNK), lambda i: (0, 0, 0)),
            pl.BlockSpec((CHUNK, D_MODEL), lambda i: (0, 0)),
        ],
        out_specs=pl.BlockSpec((tm, D_MODEL), lambda i: (i, 0)),
        out_shape=jax.ShapeDtypeStruct((grp.m, D_MODEL), BF16),
        scratch_shapes=[pltpu.VMEM((N_GROUPS_B, CHUNK, CHUNK), BF16)],
        compiler_params=_params(("arbitrary",),
                                [((tm, D_MODEL), BF16)] * 3 + [((N_GROUPS_B, CHUNK, CHUNK), F32),
                                                                ((CHUNK, D_MODEL), F32)],
                                [((tm, D_MODEL), F32)] * 2),
        name="branch_b",
    )(gel, gel, gv.reshape(1, D_MODEL), ws, bias_full)


def _branch_b_short_kernel(u_ref, v_ref, gv_ref, wc_ref, bias_ref, yb_ref, vn_ref, *, nb, t):
    vn = _rms(v_ref[...].astype(F32), gv_ref[...]).reshape(nb, t, D_MODEL)
    vn_ref[...] = vn
    mixed = bias_ref[...][None]
    for s in range(t):
        mixed = mixed + wc_ref[s][None] * vn[:, s:s + 1, :]
    yb = u_ref[...].astype(F32).reshape(nb, t, D_MODEL) * mixed
    yb_ref[...] = yb.reshape(nb * t, D_MODEL).astype(BF16)


def _branch_b_short(grp, gel, gv, ws, bs):
    b, t = grp.b, grp.t
    nb = 16
    rows = nb * t
    tri = jnp.tril(jnp.ones((t, t), F32))
    wc = jnp.repeat(jnp.transpose(ws[:, :t, :t] * tri, (2, 1, 0)), GROUP_B, axis=-1)
    bias = jnp.repeat(bs[:, :t].T, GROUP_B, axis=-1)
    return pl.pallas_call(
        functools.partial(_branch_b_short_kernel, nb=nb, t=t),
        grid=(b // nb,),
        in_specs=[
            pl.BlockSpec((rows, D_MODEL), lambda i: (i, 0)),
            pl.BlockSpec((rows, D_MODEL), lambda i: (i, 1)),
            pl.BlockSpec((1, D_MODEL), lambda i: (0, 0)),
            pl.BlockSpec((t, t, D_MODEL), lambda i: (0, 0, 0)),
            pl.BlockSpec((t, D_MODEL), lambda i: (0, 0)),
        ],
        out_specs=[
            pl.BlockSpec((rows, D_MODEL), lambda i: (i, 0)),
            pl.BlockSpec((nb, t, D_MODEL), lambda i: (i, 0, 0)),
        ],
        out_shape=[
            jax.ShapeDtypeStruct((grp.m, D_MODEL), BF16),
            jax.ShapeDtypeStruct((b, t, D_MODEL), F32),
        ],
        compiler_params=_params(("arbitrary",),
                                [((rows, D_MODEL), BF16)] * 3 + [((rows, D_MODEL), F32),
                                                                  ((t, t, D_MODEL), F32)],
                                [((rows, D_MODEL), F32)] * 3),
        name="branch_b_short",
    )(gel, gel, gv.reshape(1, D_MODEL), wc, bias)


def _merge_kernel(ya_ref, yb_ref, sa_ref, sb_ref, wpa_ref, wpb_ref, o_ref, wa_ref, wb_ref):
    @pl.when(pl.program_id(1) == 0)
    def _():
        wa_ref[...] = wpa_ref[...].astype(BF16)
        wb_ref[...] = wpb_ref[...].astype(BF16)

    pa = jnp.dot(ya_ref[...], wa_ref[...], preferred_element_type=F32)
    pb = jnp.dot(yb_ref[...], wb_ref[...], preferred_element_type=F32)
    o_ref[...] = (sa_ref[...].astype(F32) * pa + sb_ref[...].astype(F32) * pb).astype(BF16)


def _merge(grp, ya, yb, sig, wpa, wpb, layer, jobs=None):
    tm, tn = 1024, 512
    nj = D_MODEL // tn
    return _hosted_call(
        _merge_kernel, [ya, yb, sig, sig, wpa, wpb], jobs, CAST_SLAB_BYTES // 2,
        grid=(nj, grp.m // tm),
        in_specs=[
            pl.BlockSpec((tm, D_MODEL), lambda j, i: (i, 0)),
            pl.BlockSpec((tm, D_MODEL), lambda j, i: (i, 0)),
            pl.BlockSpec((tm, tn), lambda j, i: (i, j)),
            pl.BlockSpec((tm, tn), lambda j, i: (i, nj + j)),
            pl.BlockSpec((None, D_MODEL, tn), lambda j, i: (layer, 0, j)),
            pl.BlockSpec((None, D_MODEL, tn), lambda j, i: (layer, 0, j)),
        ],
        out_specs=[pl.BlockSpec((tm, tn), lambda j, i: (i, j))],
        out_shape=[jax.ShapeDtypeStruct((grp.m, D_MODEL), BF16)],
        scratch_shapes=[pltpu.VMEM((D_MODEL, tn), BF16), pltpu.VMEM((D_MODEL, tn), BF16)],
        blocks=[((tm, D_MODEL), BF16)] * 2 + [((tm, tn), BF16)] * 3 + [((D_MODEL, tn), F32)] * 2,
        scratch=[((D_MODEL, tn), BF16)] * 2 + [((tm, tn), F32)] * 2,
        name="merge",
    )[0]


def _post_math(y, x, gpost, gt, nxt):
    x1 = x + gt * _rms(y, gpost)
    if nxt is None:
        return x1, None
    gpre, one_plus_sc, sh = nxt
    return x1, _rms(x1, gpre) * one_plus_sc + sh


def _next_mods(gpre_ref, sc_ref, sh_ref, rows):
    return gpre_ref[...], _mod_rows(sc_ref, rows, 1.0), _mod_rows(sh_ref, rows)


def _oproj_post_kernel(m_ref, wo_ref, x_ref, gpost_ref, gt_ref, gpre_ref, sc_ref, sh_ref,
                       xo_ref, xn_ref):
    y = jnp.dot(m_ref[...], wo_ref[...], preferred_element_type=F32)
    rows = x_ref.shape[0]
    x1, xn = _post_math(y, x_ref[...], gpost_ref[...], _mod_rows(gt_ref, rows),
                        _next_mods(gpre_ref, sc_ref, sh_ref, rows))
    xo_ref[...] = x1
    xn_ref[...] = xn.astype(xn_ref.dtype)


def _oproj_post(grp, m, wo, layer, x, gpost, gt, gpre, sc, sh, xn_dtype, jobs=None):
    tm = 512
    vec = lambda: pl.BlockSpec((1, D_MODEL), lambda i: (0, 0))
    row = lambda: pl.BlockSpec((tm, D_MODEL), lambda i: (i, 0))
    return _hosted_call(
        _oproj_post_kernel,
        [m, wo, x, gpost.reshape(1, D_MODEL), gt, gpre.reshape(1, D_MODEL), sc, sh],
        jobs, CAST_SLAB_BYTES,
        grid=(grp.m // tm,),
        in_specs=[
            row(),
            pl.BlockSpec((None, D_MODEL, D_MODEL), lambda i: (layer, 0, 0),
                         pipeline_mode=pl.Buffered(1)),
            row(), vec(), grp.mod_spec(tm), vec(), grp.mod_spec(tm), grp.mod_spec(tm),
        ],
        out_specs=[row(), row()],
        out_shape=[
            jax.ShapeDtypeStruct((grp.m, D_MODEL), F32),
            jax.ShapeDtypeStruct((grp.m, D_MODEL), xn_dtype),
        ],
        scratch_shapes=[],
        blocks=[((tm, D_MODEL), F32)] * 5,
        scratch=[((D_MODEL, D_MODEL), BF16), ((tm, D_MODEL), F32)],
        name="oproj_post",
    )


def _post_kernel(y_ref, x_ref, gpost_ref, gt_ref, *rest, with_next):
    rows = x_ref.shape[0]
    if with_next:
        gpre_ref, sc_ref, sh_ref, xo_ref, xn_ref = rest
        nxt = _next_mods(gpre_ref, sc_ref, sh_ref, rows)
    else:
        (xo_ref,) = rest
        nxt = None
    x1, xn = _post_math(y_ref[...], x_ref[...], gpost_ref[...], _mod_rows(gt_ref, rows), nxt)
    xo_ref[...] = x1
    if with_next:
        xn_ref[...] = xn.astype(xn_ref.dtype)


def _post_specs(grp, tm, x, gpost, gt, nxt, imap):
    vec = lambda: pl.BlockSpec((1, D_MODEL), lambda *ids: (0, 0))
    row = lambda: pl.BlockSpec((tm, D_MODEL), imap)
    in_specs = [row(), vec(), grp.mod_spec(tm)]
    args = [x, gpost.reshape(1, D_MODEL), gt]
    out_specs = [row()]
    out_shape = [jax.ShapeDtypeStruct((grp.m, D_MODEL), F32)]
    if nxt is not None:
        gpre, sc, sh = nxt
        in_specs += [vec(), grp.mod_spec(tm), grp.mod_spec(tm)]
        args += [gpre.reshape(1, D_MODEL), sc, sh]
        out_specs.append(row())
        out_shape.append(jax.ShapeDtypeStruct((grp.m, D_MODEL), BF16))
    return in_specs, args, out_specs, out_shape


def _post(grp, y, x, gpost, gt, nxt):
    tm = 512
    imap = lambda i: (i, 0)
    in_specs, args, out_specs, out_shape = _post_specs(grp, tm, x, gpost, gt, nxt, imap)
    out = pl.pallas_call(
        functools.partial(_post_kernel, with_next=nxt is not None),
        grid=(grp.m // tm,),
        in_specs=[pl.BlockSpec((tm, D_MODEL), imap)] + in_specs,
        out_specs=out_specs,
        out_shape=out_shape,
        compiler_params=_params(("arbitrary",), [((tm, D_MODEL), F32)] * 8),
        name="post",
    )(y, *args)
    return (out[0], out[1]) if nxt is not None else (out[0], None)


def _swiglu_tile(x, wg_ref, wu_ref, wd_ref):
    g = jnp.dot(x, wg_ref[...], preferred_element_type=F32)
    u = jnp.dot(x, wu_ref[...], preferred_element_type=F32)
    h = _silu(g) * u
    return jnp.dot(h.astype(BF16), wd_ref[...], preferred_element_type=F32)


def _swiglu_kernel(x_ref, wg_ref, wu_ref, wd_ref, o_ref):
    @pl.when(pl.program_id(1) == 0)
    def _():
        o_ref[...] = jnp.zeros_like(o_ref)

    o_ref[...] += _swiglu_tile(x_ref[...], wg_ref, wu_ref, wd_ref)


def _swiglu(grp, x, wg, wu, wd):
    f = wg.shape[-1]
    tm, tf = 1024, 512
    return pl.pallas_call(
        _swiglu_kernel,
        grid=(grp.m // tm, f // tf),
        in_specs=[
            pl.BlockSpec((tm, D_MODEL), lambda i, j: (i, 0)),
            pl.BlockSpec((D_MODEL, tf), lambda i, j: (0, j)),
            pl.BlockSpec((D_MODEL, tf), lambda i, j: (0, j)),
            pl.BlockSpec((tf, D_MODEL), lambda i, j: (j, 0)),
        ],
        out_specs=pl.BlockSpec((tm, D_MODEL), lambda i, j: (i, 0)),
        out_shape=jax.ShapeDtypeStruct((grp.m, D_MODEL), F32),
        compiler_params=_params(("arbitrary", "arbitrary"),
                                [((tm, D_MODEL), BF16), ((tm, D_MODEL), F32)]
                                + [((D_MODEL, tf), BF16)] * 3,
                                [((tm, tf), F32)] * 4),
        name="swiglu_dense",
    )(x, wg, wu, wd)


META_E1, META_E2, META_R1, META_R2, META_W1, META_W2 = range(6)
DMA_UNROLL = 8


def _split_bf16(v):
    hi = v.astype(BF16)
    return hi, (v - hi.astype(F32)).astype(BF16)


def _router_kernel(x_ref, rw_ref, rb_ref, meta_ref, cnt_ref, tri_ref):
    tm = x_ref.shape[0]

    @pl.when(pl.program_id(0) == 0)
    def _():
        cnt_ref[...] = jnp.zeros_like(cnt_ref)
        tri_ref[...] = (lax.broadcasted_iota(jnp.int32, (tm, tm), 0)
                        >= lax.broadcasted_iota(jnp.int32, (tm, tm), 1)).astype(BF16)

    xh, xl = _split_bf16(x_ref[...])
    wh, wl = _split_bf16(rw_ref[...])
    logits = (jnp.dot(xh, wh, preferred_element_type=F32) + jnp.dot(xl, wh, preferred_element_type=F32)
              + jnp.dot(xh, wl, preferred_element_type=F32)) + rb_ref[...]
    lane = lax.broadcasted_iota(jnp.int32, logits.shape, 1)
    logits = jnp.where(lane < N_EXPERTS, logits, -jnp.inf)
    ex = jnp.exp(logits - jnp.max(logits, axis=-1, keepdims=True))
    p = ex / jnp.sum(ex, axis=-1, keepdims=True)
    p1 = jnp.max(p, axis=-1, keepdims=True)
    i1 = jnp.min(jnp.where(p == p1, lane, LANES), axis=-1, keepdims=True)
    rest = jnp.where(lane == i1, -1.0, p)
    p2 = jnp.max(rest, axis=-1, keepdims=True)
    i2 = jnp.min(jnp.where(rest == p2, lane, LANES), axis=-1, keepdims=True)
    denom = p1 + p2
    onehot = jnp.where(lane == i1, 1.0, 0.0) + jnp.where(lane == i2, 1.0, 0.0)
    cum = jnp.dot(tri_ref[...], onehot.astype(BF16), preferred_element_type=F32) + cnt_ref[...]
    cnt_ref[...] = cum[tm - 1:tm, :]
    r1 = jnp.sum(jnp.where(lane == i1, cum, 0.0), axis=-1, keepdims=True) - 1.0
    r2 = jnp.sum(jnp.where(lane == i2, cum, 0.0), axis=-1, keepdims=True) - 1.0
    meta = jnp.zeros_like(logits)
    for k, v in ((META_E1, i1.astype(F32)), (META_E2, i2.astype(F32)), (META_R1, r1), (META_R2, r2),
                 (META_W1, p1 / denom), (META_W2, p2 / denom)):
        meta = jnp.where(lane == k, v, meta)
    meta_ref[...] = meta


def _router(grp, x, rw_pad, rb_pad):
    tm = 512
    return pl.pallas_call(
        _router_kernel,
        grid=(grp.m // tm,),
        in_specs=[
            pl.BlockSpec((tm, D_MODEL), lambda i: (i, 0)),
            pl.BlockSpec((D_MODEL, LANES), lambda i: (0, 0)),
            pl.BlockSpec((1, LANES), lambda i: (0, 0)),
        ],
        out_specs=[
            pl.BlockSpec((tm, LANES), lambda i: (i, 0)),
            pl.BlockSpec((1, LANES), lambda i: (0, 0)),
        ],
        out_shape=[
            jax.ShapeDtypeStruct((grp.m, LANES), F32),
            jax.ShapeDtypeStruct((1, LANES), F32),
        ],
        scratch_shapes=[pltpu.VMEM((tm, tm), BF16)],
        compiler_params=_params(("arbitrary",),
                                [((tm, D_MODEL), F32), ((D_MODEL, LANES), F32), ((tm, LANES), F32)],
                                [((tm, tm), BF16), ((tm, D_MODEL), F32)]),
        name="router",
    )(x, rw_pad, rb_pad)


def _route_plan(metas, counts, tg, n_tiles):
    cnts = [c[0, :N_EXPERTS].astype(jnp.int32) for c in counts]
    total = sum(cnts)
    padded = (total + tg - 1) // tg * tg
    ends = jnp.cumsum(padded)
    offs = ends - padded
    slots = []
    for meta, before in zip(metas, [sum(cnts[:g], jnp.zeros_like(total)) for g in range(len(cnts))]):
        idx = meta[:, :META_W1].astype(jnp.int32)
        start = offs + before
        slots.append((start[idx[:, META_E1]] + idx[:, META_R1], start[idx[:, META_E2]] + idx[:, META_R2]))
    n_used = ends[-1] // tg
    tile = jnp.arange(n_tiles, dtype=jnp.int32)
    tile_c = jnp.minimum(tile, n_used - 1)
    tile_expert = jnp.sum(tile_c[:, None] * tg >= ends[None, :], axis=1).astype(jnp.int32)
    return slots, tile_expert, n_used.reshape(1).astype(jnp.int32)


def _dispatch_kernel(pos1_ref, pos2_ref, x_ref, xs_in_ref, xs_ref, sem):
    del xs_in_ref
    tm = x_ref.shape[0]
    base = pl.program_id(0) * tm

    def row_copy(t, slot):
        return pltpu.make_async_copy(x_ref.at[pl.ds(t, 1)], xs_ref.at[pl.ds(slot, 1)], sem)

    def start(t, c):
        row_copy(t, pos1_ref[base + t]).start()
        row_copy(t, pos2_ref[base + t]).start()
        return c

    def wait(t, c):
        row_copy(t, 0).wait()
        row_copy(t, 0).wait()
        return c

    lax.fori_loop(0, tm, start, 0, unroll=DMA_UNROLL)
    lax.fori_loop(0, tm, wait, 0, unroll=DMA_UNROLL)


def _dispatch(grp, x, pos1, pos2, xs):
    tm = 1024
    return pl.pallas_call(
        _dispatch_kernel,
        grid_spec=pltpu.PrefetchScalarGridSpec(
            num_scalar_prefetch=2,
            grid=(grp.m // tm,),
            in_specs=[
                pl.BlockSpec((tm, D_MODEL), lambda i, p1, p2: (i, 0)),
                pl.BlockSpec(memory_space=pl.ANY),
            ],
            out_specs=pl.BlockSpec(memory_space=pl.ANY),
            scratch_shapes=[pltpu.SemaphoreType.DMA(())],
        ),
        out_shape=jax.ShapeDtypeStruct(xs.shape, F32),
        input_output_aliases={3: 0},
        compiler_params=_params(("arbitrary",), [((tm, D_MODEL), F32)]),
        name="moe_dispatch",
    )(pos1, pos2, x, xs)


def _experts_kernel(te_ref, nu_ref, xs_ref, wg_ref, wu_ref, wd_ref, o_ref):
    del te_ref
    r, j = pl.program_id(0), pl.program_id(1)

    @pl.when(j == 0)
    def _():
        o_ref[...] = jnp.zeros_like(o_ref)

    @pl.when(r < nu_ref[0])
    def _():
        o_ref[...] += _swiglu_tile(xs_ref[...].astype(BF16), wg_ref, wu_ref, wd_ref)


def _experts(xs, tile_expert, n_used, wg, wu, wd, tg):
    n_rows = xs.shape[0]
    f = wg.shape[-1]
    tf = 1024
    n_j = f // tf

    def used(r, nu):
        return r < nu[0]

    def x_map(r, j, te, nu):
        return (jnp.minimum(r, nu[0] - 1), 0)

    def up_map(r, j, te, nu):
        return (te[r], 0, jnp.where(used(r, nu), j, n_j - 1))

    def down_map(r, j, te, nu):
        return (te[r], jnp.where(used(r, nu), j, n_j - 1), 0)

    return pl.pallas_call(
        _experts_kernel,
        grid_spec=pltpu.PrefetchScalarGridSpec(
            num_scalar_prefetch=2,
            grid=(n_rows // tg, n_j),
            in_specs=[
                pl.BlockSpec((tg, D_MODEL), x_map),
                pl.BlockSpec((None, D_MODEL, tf), up_map),
                pl.BlockSpec((None, D_MODEL, tf), up_map),
                pl.BlockSpec((None, tf, D_MODEL), down_map),
            ],
            out_specs=pl.BlockSpec((tg, D_MODEL), lambda r, j, te, nu: (r, 0)),
        ),
        out_shape=jax.ShapeDtypeStruct((n_rows, D_MODEL), F32),
        compiler_params=_params(("arbitrary", "arbitrary"),
                                [((tg, D_MODEL), F32)] * 2 + [((D_MODEL, tf), BF16)] * 3,
                                [((tg, tf), F32)] * 4 + [((tg, D_MODEL), BF16)]),
        name="moe_experts",
    )(tile_expert, n_used, xs, wg, wu, wd)


def _combine_post_kernel(pos1_ref, pos2_ref, ys_ref, meta_ref, x_ref, gpost_ref, gt_ref, *rest,
                         with_next):
    *rest, buf1_ref, buf2_ref, sem = rest
    tm = x_ref.shape[0]
    step, n_steps = pl.program_id(0), pl.num_programs(0)
    cur = step % 2

    def row_copy(row, buf_ref, half, t):
        return pltpu.make_async_copy(ys_ref.at[pl.ds(row, 1)], buf_ref.at[half, pl.ds(t, 1)],
                                     sem.at[half])

    def gather(tile, half):
        def start(t, c):
            row_copy(pos1_ref[tile * tm + t], buf1_ref, half, t).start()
            row_copy(pos2_ref[tile * tm + t], buf2_ref, half, t).start()
            return c
        lax.fori_loop(0, tm, start, 0, unroll=DMA_UNROLL)

    @pl.when(step == 0)
    def _():
        gather(0, 0)

    @pl.when(step + 1 < n_steps)
    def _():
        gather(step + 1, 1 - cur)

    def wait(t, c):
        row_copy(0, buf1_ref, cur, t).wait()
        row_copy(0, buf2_ref, cur, t).wait()
        return c

    lax.fori_loop(0, tm, wait, 0, unroll=DMA_UNROLL)
    meta = meta_ref[...]
    y = meta[:, META_W1:META_W1 + 1] * buf1_ref[cur] + meta[:, META_W2:META_W2 + 1] * buf2_ref[cur]
    if with_next:
        gpre_ref, sc_ref, sh_ref, xo_ref, xn_ref = rest
        nxt = _next_mods(gpre_ref, sc_ref, sh_ref, tm)
    else:
        (xo_ref,) = rest
        nxt = None
    x1, xn = _post_math(y, x_ref[...], gpost_ref[...], _mod_rows(gt_ref, tm), nxt)
    xo_ref[...] = x1
    if with_next:
        xn_ref[...] = xn.astype(xn_ref.dtype)


def _combine_post(grp, ys, pos1, pos2, meta, x, gpost, gt, nxt):
    tm = 512
    imap = lambda i, p1, p2: (i, 0)
    in_specs, args, out_specs, out_shape = _post_specs(grp, tm, x, gpost, gt, nxt, imap)
    out = pl.pallas_call(
        functools.partial(_combine_post_kernel, with_next=nxt is not None),
        grid_spec=pltpu.PrefetchScalarGridSpec(
            num_scalar_prefetch=2,
            grid=(grp.m // tm,),
            in_specs=[pl.BlockSpec(memory_space=pl.ANY), pl.BlockSpec((tm, LANES), imap)] + in_specs,
            out_specs=out_specs,
            scratch_shapes=[pltpu.VMEM((2, tm, D_MODEL), F32), pltpu.VMEM((2, tm, D_MODEL), F32),
                            pltpu.SemaphoreType.DMA((2,))],
        ),
        out_shape=out_shape,
        compiler_params=_params(("arbitrary",), [((tm, D_MODEL), F32)] * 7,
                                [((tm, D_MODEL), F32)] * 4),
        name="moe_combine_post",
    )(pos1, pos2, ys, meta, *args)
    return (out[0], out[1]) if nxt is not None else (out[0], None)


def _moe_post(grps, xns, p, j, xs_res, gpost, gts, nxts):
    tg = 512
    n_tiles = 2 * sum(g.m for g in grps) // tg + N_EXPERTS
    rw_pad = jnp.pad(p['router_w'][j], ((0, 0), (0, LANES - N_EXPERTS)))
    rb_pad = jnp.pad(p['router_b'][j], (0, LANES - N_EXPERTS)).reshape(1, LANES)
    routed = [_router(g, xn, rw_pad, rb_pad) for g, xn in zip(grps, xns)]
    metas, counts = zip(*routed)
    slots, tile_expert, n_used = _route_plan(metas, counts, tg, n_tiles)
    xs = jnp.zeros((n_tiles * tg, D_MODEL), F32)
    for g, xn, (pos1, pos2) in zip(grps, xns, slots):
        xs = _dispatch(g, xn, pos1, pos2, xs)
    wg, wu, wd = (p['jobs'].result(k)[j] for k in ('moe_wg', 'moe_wu', 'moe_wd'))
    ys = _experts(xs, tile_expert, n_used, wg, wu, wd, tg)
    return [_combine_post(g, ys, pos1, pos2, meta, x, gpost, gt, nxt)
            for g, (pos1, pos2), meta, x, gt, nxt in zip(grps, slots, metas, xs_res, gts, nxts)]


def _mixer(grp, l, x, xn, mods, h0, conv0, p, a_tiles, xn_dtype):
    _, _, gt1, sh2, sc2, _ = mods
    w_in = p['w_in']
    ya, h_last, conv_new = _branch_a(
        grp, *a_tiles, xn, p['w_ag_bf'], l, conv0, h0, p['conv_w'][l], p['conv_b'][l],
        p['w_r'][l], p['w_i'][l], p['b_r'][l], p['b_i'][l], p['lru_lambda'][l], p['jobs'])
    gel = _mm_act(grp, xn, w_in, l, 2 * D_MODEL, 2 * D_MODEL, _gelu, BF16, "in_proj_gelu", p['jobs'])
    sig = _mm_act(grp, xn, w_in, l, 4 * D_MODEL, 2 * D_MODEL, _sigmoid, BF16, "in_proj_sigmoid", p['jobs'])
    if grp.t >= CHUNK:
        bias_full = jnp.repeat(p['b_s'][l].T, GROUP_B, axis=-1)
        yb = _branch_b(grp, gel, p['g_v'][l], p['w_s'][l], bias_full)
        vn = None
    else:
        yb, vn = _branch_b_short(grp, gel, p['g_v'][l], p['w_s'][l], p['b_s'][l])
    m = _merge(grp, ya, yb, sig, p['w_pa'], p['w_pb'], l, p['jobs'])
    x, xn2 = _oproj_post(grp, m, p['w_o_bf'], l, x, p['g_post_mix'][l], gt1,
                         p['g_pre_ffn'][l], sc2, sh2, xn_dtype, p['jobs'])
    return x, xn2, h_last.reshape(grp.b, D_MODEL), conv_new, vn


def _trunk(grps, xs, mods, h0s, conv0s, p, tiles):
    depth = p['w_in'].shape[0]
    n_g = len(grps)
    xns = [_prenorm(g, x, p['g_pre_mix'][0], mods[i][0][1], mods[i][0][0])
           for i, (g, x) in enumerate(zip(grps, xs))]
    states = [([], [], []) for _ in grps]
    for l in range(depth):
        dense = l % 2 == 0
        j = l // 2
        xn2s = []
        for i, g in enumerate(grps):
            xs[i], xn2, h_last, conv_new, vn = _mixer(
                g, l, xs[i], xns[i], mods[i][l], h0s[i][l], conv0s[i][l], p, tiles[i],
                BF16 if dense else F32)
            xn2s.append(xn2)
            for acc, v in zip(states[i], (h_last, conv_new, vn)):
                acc.append(v)
        if l + 1 < depth:
            nxts = [(p['g_pre_mix'][l + 1], mods[i][l + 1][1], mods[i][l + 1][0]) for i in range(n_g)]
        else:
            nxts = [None] * n_g
        gt2s = [mods[i][l][5] for i in range(n_g)]
        gpost = p['g_post_ffn'][l]
        if dense:
            outs = []
            for i, g in enumerate(grps):
                wg, wu, wd = (p['jobs'].result(k)[j] for k in ('ffn_wg', 'ffn_wu', 'ffn_wd'))
                y = _swiglu(g, xn2s[i], wg, wu, wd)
                outs.append(_post(g, y, xs[i], gpost, gt2s[i], nxts[i]))
        else:
            outs = _moe_post(grps, xn2s, p, j, xs, gpost, gt2s, nxts)
        xs = [o[0] for o in outs]
        xns = [o[1] for o in outs]
    return xs, states


def kernel(x_prompt, x_sample, c_prompt, c_sample, state_lru_h, state_lru_conv, w_ada, b_ada, g_pre_mix, g_post_mix, g_pre_ffn, g_post_ffn, w_in, conv_w, conv_b, w_r, b_r, w_i, b_i, lru_lambda, g_v, w_s, b_s, w_pa, w_pb, w_o, ffn_wg, ffn_wu, ffn_wd, router_w, router_b, moe_wg, moe_wu, moe_wd):
    p = dict(g_pre_mix=g_pre_mix, g_post_mix=g_post_mix, g_pre_ffn=g_pre_ffn, g_post_ffn=g_post_ffn,
             w_in=w_in, conv_w=conv_w, conv_b=conv_b, w_r=w_r, b_r=b_r, w_i=w_i, b_i=b_i,
             lru_lambda=lru_lambda, g_v=g_v, w_s=w_s, b_s=b_s, w_pa=w_pa, w_pb=w_pb,
             router_w=router_w, router_b=router_b, w_o_bf=w_o.astype(BF16),
             w_ag_bf=w_in[:, :, :2 * D_MODEL].astype(BF16),
             jobs=_CastJobs(dict(moe_wg=moe_wg, moe_wu=moe_wu, moe_wd=moe_wd,
                                 ffn_wg=ffn_wg, ffn_wu=ffn_wu, ffn_wd=ffn_wd)))
    depth = w_in.shape[0]
    bp, tp, _ = x_prompt.shape
    bs, ts, _ = x_sample.shape
    assert tp % CHUNK == 0 and ts < CHUNK and ts % SUBLANES == 0

    n_c = bp + bs
    r_pad = -n_c % 16
    c_all = jnp.concatenate([c_sample, c_prompt, jnp.zeros((r_pad, D_MODEL), F32)], axis=0)
    mod = _ada(c_all, w_ada, b_ada)
    mods_s = [[mod[l, k, :bs] for k in range(6)] for l in range(depth)]
    mods_p = [[mod[l, k, bs:n_c].reshape(bp, 1, D_MODEL) for k in range(6)] for l in range(depth)]

    grp_p = _Group(bp, tp, short=False)
    grp_s = _Group(bs, ts, short=True)
    zeros_h = jnp.zeros((depth, bp, D_MODEL), F32)
    zeros_conv = jnp.zeros((depth, bp, CONV_W - 1, D_MODEL), F32)

    (y_p, y_s), (st_p, st_s) = _trunk(
        [grp_p, grp_s],
        [x_prompt.reshape(bp * tp, D_MODEL), x_sample.reshape(bs * ts, D_MODEL)],
        [mods_p, mods_s], [zeros_h, state_lru_h], [zeros_conv, state_lru_conv], p,
        tiles=[(1024, 1024, 1, 256), (512, 1024, 16, ts)])
    return (y_p.reshape(bp, tp, D_MODEL), y_s.reshape(bs, ts, D_MODEL),
            jnp.stack(st_p[0]), jnp.stack(st_p[1]), jnp.stack(st_s[0]), jnp.stack(st_s[1]),
            jnp.stack(st_s[2]))
```

```python
import functools

import jax
import jax.numpy as jnp
from jax import lax
from jax.experimental import pallas as pl
from jax.experimental.pallas import tpu as pltpu

F32 = jnp.float32
BF16 = jnp.bfloat16

D_MODEL = 2048
N_HEADS_A = 16
HEAD_A = D_MODEL // N_HEADS_A
CONV_W = 4
C_GATE = 8.0
N_GROUPS_B = 16
GROUP_B = D_MODEL // N_GROUPS_B
CHUNK = 128
N_EXPERTS = 8
EPS = 1e-6

LANES = 128
SUBLANES = 8
VMEM_CAP = 58 * 2**20
CONV_PAD = SUBLANES
CAST_SLAB_BYTES = 4 * 2**20


def _nbytes(shape, dtype):
    n = 1
    for s in shape:
        n *= s
    return n * jnp.dtype(dtype).itemsize


def _params(sem, blocks, scratch=()):
    need = 2 * sum(_nbytes(s, d) for s, d in blocks) + sum(_nbytes(s, d) for s, d in scratch)
    limit = min(VMEM_CAP, need + 16 * 2**20)
    return pltpu.CompilerParams(dimension_semantics=sem, vmem_limit_bytes=limit)


def _rms(x, g):
    return x * lax.rsqrt(jnp.mean(x * x, axis=-1, keepdims=True) + EPS) * g


def _sigmoid(x):
    return 1.0 / (1.0 + jnp.exp(-x))


def _sigmoid_tanh(x):
    return 0.5 * jnp.tanh(0.5 * x) + 0.5


def _silu(x):
    return x * _sigmoid(x)


def _gelu(x):
    return jax.nn.gelu(x)


class _Group:
    def __init__(self, b, t, short):
        self.b, self.t, self.m = b, t, b * t
        self.short = short

    def mod_spec(self, tm):
        if self.short:
            return pl.BlockSpec((tm // self.t, D_MODEL), lambda *ids: (ids[0], 0))
        t = self.t
        return pl.BlockSpec((None, 1, D_MODEL), lambda *ids: ((ids[0] * tm) // t, 0, 0))


def _mod_rows(ref, rows, offset=None):
    n = ref.shape[0]
    vals = ref[...] if offset is None else ref[...] + offset
    if n == 1:
        return vals
    rep = rows // n
    return jnp.concatenate([jnp.broadcast_to(vals[r:r + 1, :], (rep, D_MODEL)) for r in range(n)], axis=0)


def _ada_kernel(c_ref, w_ref, b_ref, o_ref):
    s = _silu(c_ref[...]).astype(BF16)
    o_ref[...] = jnp.dot(s, w_ref[...].astype(BF16), preferred_element_type=F32) + b_ref[...]


def _ada(c, w_ada, b_ada):
    depth, _, n = w_ada.shape
    r = c.shape[0]
    tn = 1024
    per_item = D_MODEL // tn
    return pl.pallas_call(
        _ada_kernel,
        grid=(depth, n // tn),
        in_specs=[
            pl.BlockSpec((r, D_MODEL), lambda l, j: (0, 0)),
            pl.BlockSpec((None, D_MODEL, tn), lambda l, j: (l, 0, j)),
            pl.BlockSpec((None, 1, tn), lambda l, j: (l, 0, j)),
        ],
        out_specs=pl.BlockSpec((None, None, r, tn), lambda l, j: (l, j // per_item, 0, j % per_item)),
        out_shape=jax.ShapeDtypeStruct((depth, n // D_MODEL, r, D_MODEL), F32),
        compiler_params=_params(("arbitrary", "arbitrary"),
                                [((r, D_MODEL), F32), ((D_MODEL, tn), F32), ((r, tn), F32)],
                                [((D_MODEL, tn), BF16)]),
        name="ada_mod",
    )(c, w_ada, b_ada.reshape(depth, 1, n))


def _prenorm_kernel(x_ref, g_ref, sc_ref, sh_ref, o_ref):
    rows = x_ref.shape[0]
    o_ref[...] = (_rms(x_ref[...], g_ref[...]) * _mod_rows(sc_ref, rows, 1.0)
                  + _mod_rows(sh_ref, rows)).astype(BF16)


def _prenorm(grp, x, g, sc, sh):
    tm = 256 if grp.short else 512
    return pl.pallas_call(
        _prenorm_kernel,
        grid=(grp.m // tm,),
        in_specs=[
            pl.BlockSpec((tm, D_MODEL), lambda i: (i, 0)),
            pl.BlockSpec((1, D_MODEL), lambda i: (0, 0)),
            grp.mod_spec(tm), grp.mod_spec(tm),
        ],
        out_specs=pl.BlockSpec((tm, D_MODEL), lambda i: (i, 0)),
        out_shape=jax.ShapeDtypeStruct((grp.m, D_MODEL), BF16),
        compiler_params=_params(("arbitrary",), [((tm, D_MODEL), F32)] * 4),
        name="prenorm",
    )(x, g.reshape(1, D_MODEL), sc, sh)


class _CastJobs:
    def __init__(self, arrays):
        self.shapes = {k: v.shape for k, v in arrays.items()}
        self.pending = [(k, v.reshape(-1, v.shape[-1])) for k, v in arrays.items()]
        self.done = {}

    def take(self, steps, slab_limit):
        for n, (name, src) in enumerate(self.pending):
            rows = src.shape[0] // steps
            if src.shape[0] % (steps * 2 * SUBLANES) == 0 and _nbytes((rows, src.shape[1]), F32) <= slab_limit:
                return self.pending.pop(n)
        return None

    def put(self, name, out):
        self.done[name] = out.reshape(self.shapes[name])

    def result(self, name):
        for n, (k, src) in enumerate(self.pending):
            if k == name:
                self.pending.pop(n)
                self.put(name, src.astype(BF16))
                break
        return self.done[name]


def _hosted_call(kernel, args, jobs, slab_limit, *, grid, in_specs, out_specs, out_shape,
                 scratch_shapes, blocks, scratch, name):
    steps = 1
    for n in grid:
        steps *= n
    job = jobs.take(steps, slab_limit) if jobs is not None else None
    if job is not None:
        n_in, n_out = len(in_specs), len(out_specs)
        src = job[1]
        slab = (src.shape[0] // steps, src.shape[1])
        strides = [1] * len(grid)
        for k in range(len(grid) - 2, -1, -1):
            strides[k] = strides[k + 1] * grid[k + 1]
        spec = pl.BlockSpec(slab, lambda *ids: (sum(i * s for i, s in zip(ids, strides)), 0))
        in_specs, out_specs = [*in_specs, spec], [*out_specs, spec]
        out_shape = [*out_shape, jax.ShapeDtypeStruct(src.shape, BF16)]
        args, blocks = [*args, src], [*blocks, (slab, F32), (slab, BF16)]
        inner = kernel

        def kernel(*refs):
            refs[n_in + 1 + n_out][...] = refs[n_in][...].astype(BF16)
            inner(*refs[:n_in], *refs[n_in + 1:n_in + 1 + n_out], *refs[n_in + 2 + n_out:])

    out = pl.pallas_call(
        kernel, grid=grid, in_specs=in_specs, out_specs=out_specs, out_shape=out_shape,
        scratch_shapes=scratch_shapes,
        compiler_params=_params(("arbitrary",) * len(grid), blocks, scratch),
        name=name,
    )(*args)
    if job is not None:
        jobs.put(job[0], out[-1])
        out = out[:-1]
    return list(out)


def _mm_act_kernel(x_ref, w_ref, o_ref, wbf_ref, *, act):
    @pl.when(pl.program_id(1) == 0)
    def _():
        wbf_ref[...] = w_ref[...].astype(BF16)

    y = jnp.dot(x_ref[...], wbf_ref[...], preferred_element_type=F32)
    o_ref[...] = act(y).astype(o_ref.dtype)


def _mm_act(grp, x, w, layer, col0, ncols, act, out_dtype, name, jobs=None):
    tm, tn = 1024, 1024
    c0 = col0 // tn
    return _hosted_call(
        functools.partial(_mm_act_kernel, act=act), [x, w], jobs, CAST_SLAB_BYTES,
        grid=(ncols // tn, grp.m // tm),
        in_specs=[
            pl.BlockSpec((tm, D_MODEL), lambda j, i: (i, 0)),
            pl.BlockSpec((None, D_MODEL, tn), lambda j, i: (layer, 0, c0 + j)),
        ],
        out_specs=[pl.BlockSpec((tm, tn), lambda j, i: (i, j))],
        out_shape=[jax.ShapeDtypeStruct((grp.m, ncols), out_dtype)],
        scratch_shapes=[pltpu.VMEM((D_MODEL, tn), BF16)],
        blocks=[((tm, D_MODEL), BF16), ((D_MODEL, tn), F32), ((tm, tn), F32)],
        scratch=[((D_MODEL, tn), BF16), ((tm, tn), F32)],
        name=name,
    )[0]


HEADS_PER_DOT = 2


def _lru_unit(xp_ref, seqs, t0, xsl, ga, carry, cw_ref, cb_ref, wr, wi, br, bi, softplus, sl, nb, tt):
    rows = nb * tt
    n_blk = tt // SUBLANES
    blk_shape = (nb * n_blk, SUBLANES, HEAD_A)
    t_in_blk = lax.broadcasted_iota(jnp.int32, blk_shape, 1)
    xb = xp_ref[seqs, t0:t0 + CONV_PAD + tt, xsl].reshape(nb * (n_blk + 1), SUBLANES, HEAD_A)
    cur = xb.reshape(nb, n_blk + 1, SUBLANES, HEAD_A)[:, 1:].reshape(blk_shape)
    xc = cb_ref[:, sl][None] + cw_ref[CONV_W - 1:CONV_W, sl][None] * cur
    for s in range(1, CONV_W):
        rolled = pltpu.roll(xb, s, axis=1).reshape(nb, n_blk + 1, SUBLANES, HEAD_A)
        shifted = jnp.where(t_in_blk >= s, rolled[:, 1:].reshape(blk_shape), rolled[:, :-1].reshape(blk_shape))
        xc = xc + cw_ref[CONV_W - 1 - s:CONV_W - s, sl][None] * shifted
    xc = xc.reshape(rows, HEAD_A)
    xcb = xc.astype(BF16)
    r = _sigmoid_tanh(jnp.dot(xcb, wr, preferred_element_type=F32) + br)
    i = _sigmoid_tanh(jnp.dot(xcb, wi, preferred_element_type=F32) + bi)
    log_a = (-C_GATE * r) * softplus
    a = jnp.exp(log_a)
    th = jnp.tanh(log_a)
    one_m_a2 = -2.0 * th / (1.0 - th)
    b = jnp.where(one_m_a2 > 0.0, one_m_a2 * lax.rsqrt(one_m_a2), 0.0) * (i * xc)
    a = a.reshape(blk_shape)
    b = b.reshape(blk_shape)
    d = 1
    while d < SUBLANES:
        a_sh = pltpu.roll(a, d, axis=1)
        b_sh = pltpu.roll(b, d, axis=1)
        keep = t_in_blk >= d
        b = jnp.where(keep, a * b_sh + b, b)
        a = jnp.where(keep, a * a_sh, a)
        d *= 2
    a3 = a.reshape(nb, tt, HEAD_A)
    b3 = b.reshape(nb, tt, HEAD_A)
    h_blocks = []
    for k in range(n_blk):
        ks = slice(k * SUBLANES, (k + 1) * SUBLANES)
        h_k = a3[:, ks, :] * carry + b3[:, ks, :]
        carry = h_k[:, SUBLANES - 1:SUBLANES, :]
        h_blocks.append(h_k)
    h = jnp.concatenate(h_blocks, axis=1) if n_blk > 1 else h_blocks[0]
    return ga * h.reshape(rows, HEAD_A), carry


def _branch_a_kernel(x_ref, wxa_ref, wga_ref, conv0_ref, h0_ref, cw_ref, cb_ref, wr_ref, wi_ref,
                     br_ref, bi_ref, lam_ref, ya_ref, hlast_ref, convnew_ref, h_ref, *xp_refs,
                     n_seq, t_tile, nb, tt, tiles_per_seq):
    i = pl.program_id(1)
    hist = CONV_W - 1
    lo = CONV_PAD - hist
    n_col = HEADS_PER_DOT * HEAD_A
    pair_cols = [slice(pr * n_col, (pr + 1) * n_col) for pr in range(len(xp_refs))]

    @pl.when(i % tiles_per_seq == 0)
    def _():
        for xp_ref, ps in zip(xp_refs, pair_cols):
            xp_ref[:, 0:lo, :] = jnp.zeros((n_seq, lo, n_col), F32)
            xp_ref[:, lo:CONV_PAD, :] = conv0_ref[:, :, ps]
        h_ref[...] = h0_ref[...]

    z = -lam_ref[...]
    softplus = jnp.maximum(z, 0.0) + jnp.log1p(jnp.exp(-jnp.abs(z)))
    x = x_ref[...]

    def project(pr):
        ps = pair_cols[pr]
        xa = jnp.dot(x, wxa_ref[:, ps], preferred_element_type=F32)
        xp_refs[pr][:, CONV_PAD:CONV_PAD + t_tile, :] = xa.reshape(n_seq, t_tile, n_col)
        return _gelu(jnp.dot(x, wga_ref[:, ps], preferred_element_type=F32))

    ga_next = project(0)
    for pr, xp_ref in enumerate(xp_refs):
        ga = ga_next
        if pr + 1 < len(xp_refs):
            ga_next = project(pr + 1)
        for hh in range(HEADS_PER_DOT):
            hd = pr * HEADS_PER_DOT + hh
            xsl = slice(hh * HEAD_A, (hh + 1) * HEAD_A)
            sl = slice(hd * HEAD_A, (hd + 1) * HEAD_A)
            wr = wr_ref[hd].astype(BF16)
            wi = wi_ref[hd].astype(BF16)
            for us in range(n_seq // nb):
                seqs = slice(us * nb, (us + 1) * nb)
                carry = h_ref[seqs, :, sl]
                for ut in range(t_tile // tt):
                    r0 = us * nb * t_tile + ut * tt
                    y, carry = _lru_unit(
                        xp_ref, seqs, ut * tt, xsl, ga[r0:r0 + nb * tt, xsl], carry,
                        cw_ref, cb_ref, wr, wi, br_ref[:, sl], bi_ref[:, sl], softplus[:, sl], sl, nb, tt)
                    ya_ref[r0:r0 + nb * tt, sl] = y.astype(BF16)
                h_ref[seqs, :, sl] = carry

    for xp_ref in xp_refs:
        xp_ref[:, lo:CONV_PAD, :] = xp_ref[:, lo + t_tile:CONV_PAD + t_tile, :]

    @pl.when(i % tiles_per_seq == tiles_per_seq - 1)
    def _():
        hlast_ref[...] = h_ref[...]
        for xp_ref, ps in zip(xp_refs, pair_cols):
            convnew_ref[:, :, ps] = xp_ref[:, lo:CONV_PAD, :]


def _branch_a(grp, tm, tc, nb, tt, xn, w_ag, layer, conv0, h0, cw, cb, wr, wi, br, bi, lam, jobs=None):
    b, t = grp.b, grp.t
    t_tile = min(t, tm)
    n_seq = tm // t_tile
    tiles_per_seq = t // t_tile
    assert (nb == 1 or tt == t_tile) and n_seq % nb == 0 and t_tile % tt == 0
    hist = CONV_W - 1
    n_j = D_MODEL // tc
    seq_blk = lambda j, i: (i // tiles_per_seq, 0, j)
    vec = lambda: pl.BlockSpec((1, tc), lambda j, i: (0, j))
    head_w = lambda: pl.BlockSpec((tc // HEAD_A, HEAD_A, HEAD_A), lambda j, i: (j, 0, 0))
    return _hosted_call(
        functools.partial(_branch_a_kernel, n_seq=n_seq, t_tile=t_tile, nb=nb, tt=tt,
                          tiles_per_seq=tiles_per_seq),
        [xn, w_ag, w_ag, conv0, h0.reshape(b, 1, D_MODEL), cw, cb.reshape(1, D_MODEL),
         wr, wi, br.reshape(1, D_MODEL), bi.reshape(1, D_MODEL), lam.reshape(1, D_MODEL)],
        jobs, CAST_SLAB_BYTES,
        grid=(n_j, grp.m // tm),
        in_specs=[
            pl.BlockSpec((tm, D_MODEL), lambda j, i: (i, 0)),
            pl.BlockSpec((None, D_MODEL, tc), lambda j, i: (layer, 0, j)),
            pl.BlockSpec((None, D_MODEL, tc), lambda j, i: (layer, 0, n_j + j)),
            pl.BlockSpec((n_seq, hist, tc), seq_blk),
            pl.BlockSpec((n_seq, 1, tc), seq_blk),
            pl.BlockSpec((CONV_W, tc), lambda j, i: (0, j)),
            vec(), head_w(), head_w(), vec(), vec(), vec(),
        ],
        out_specs=[
            pl.BlockSpec((tm, tc), lambda j, i: (i, j)),
            pl.BlockSpec((n_seq, 1, tc), seq_blk),
            pl.BlockSpec((n_seq, hist, tc), seq_blk),
        ],
        out_shape=[
            jax.ShapeDtypeStruct((grp.m, D_MODEL), BF16),
            jax.ShapeDtypeStruct((b, 1, D_MODEL), F32),
            jax.ShapeDtypeStruct((b, hist, D_MODEL), F32),
        ],
        scratch_shapes=[pltpu.VMEM((n_seq, 1, tc), F32)]
        + [pltpu.VMEM((n_seq, CONV_PAD + t_tile, HEADS_PER_DOT * HEAD_A), F32)]
        * (tc // (HEADS_PER_DOT * HEAD_A)),
        blocks=[((tm, D_MODEL), BF16), ((D_MODEL, tc), BF16), ((D_MODEL, tc), BF16),
                ((tm, tc), BF16)] + [((n_seq, SUBLANES, tc), F32)] * 4,
        scratch=[((n_seq, CONV_PAD + t_tile, tc), F32), ((n_seq, SUBLANES, tc), F32),
                 ((tm, 2 * HEADS_PER_DOT * HEAD_A), F32)],
        name="branch_a",
    )


def _branch_b_kernel(u_ref, v_ref, gv_ref, ws_ref, bias_ref, yb_ref, wm_ref, *, n_chunks):
    @pl.when(pl.program_id(0) == 0)
    def _():
        tri = (lax.broadcasted_iota(jnp.int32, (CHUNK, CHUNK), 0)
               >= lax.broadcasted_iota(jnp.int32, (CHUNK, CHUNK), 1))
        for g in range(N_GROUPS_B):
            wm_ref[g] = jnp.where(tri, ws_ref[g], 0.0).astype(BF16)

    vn = _rms(v_ref[...].astype(F32), gv_ref[...]).astype(BF16)
    for g in range(N_GROUPS_B):
        cs = slice(g * GROUP_B, (g + 1) * GROUP_B)
        rhs = jnp.concatenate([vn[c * CHUNK:(c + 1) * CHUNK, cs] for c in range(n_chunks)], axis=1)
        mixed = jnp.dot(wm_ref[g], rhs, preferred_element_type=F32)
        for c in range(n_chunks):
            rs = slice(c * CHUNK, (c + 1) * CHUNK)
            mixed_c = mixed[:, c * GROUP_B:(c + 1) * GROUP_B] + bias_ref[:, cs]
            yb_ref[rs, cs] = (u_ref[rs, cs].astype(F32) * mixed_c).astype(BF16)


def _branch_b(grp, gel, gv, ws, bias_full):
    tm = 512
    return pl.pallas_call(
        functools.partial(_branch_b_kernel, n_chunks=tm // CHUNK),
        grid=(grp.m // tm,),
        in_specs=[
            pl.BlockSpec((tm, D_MODEL), lambda i: (i, 0)),
            pl.BlockSpec((tm, D_MODEL), lambda i: (i, 1)),
            pl.BlockSpec((1, D_MODEL), lambda i: (0, 0)),
            pl.BlockSpec((N_GROUPS_B, CHUNK, CHUNK), lambda i: (0, 0, 0)),
            pl.BlockSpec((CHUNK, D_MODEL), lambda i: (0, 0)),
        ],
        out_specs=pl.BlockSpec((tm, D_MODEL), lambda i: (i, 0)),
        out_shape=jax.ShapeDtypeStruct((grp.m, D_MODEL), BF16),
        scratch_shapes=[pltpu.VMEM((N_GROUPS_B, CHUNK, CHUNK), BF16)],
        compiler_params=_params(("arbitrary",),
                                [((tm, D_MODEL), BF16)] * 3 + [((N_GROUPS_B, CHUNK, CHUNK), F32),
                                                                ((CHUNK, D_MODEL), F32)],
                                [((tm, D_MODEL), F32)] * 2),
        name="branch_b",
    )(gel, gel, gv.reshape(1, D_MODEL), ws, bias_full)


def _branch_b_short_kernel(u_ref, v_ref, gv_ref, wc_ref, bias_ref, yb_ref, vn_ref, *, nb, t):
    vn = _rms(v_ref[...].astype(F32), gv_ref[...]).reshape(nb, t, D_MODEL)
    vn_ref[...] = vn
    mixed = bias_ref[...][None]
    for s in range(t):
        mixed = mixed + wc_ref[s][None] * vn[:, s:s + 1, :]
    yb = u_ref[...].astype(F32).reshape(nb, t, D_MODEL) * mixed
    yb_ref[...] = yb.reshape(nb * t, D_MODEL).astype(BF16)


def _branch_b_short(grp, gel, gv, ws, bs):
    b, t = grp.b, grp.t
    nb = 16
    rows = nb * t
    tri = jnp.tril(jnp.ones((t, t), F32))
    wc = jnp.repeat(jnp.transpose(ws[:, :t, :t] * tri, (2, 1, 0)), GROUP_B, axis=-1)
    bias = jnp.repeat(bs[:, :t].T, GROUP_B, axis=-1)
    return pl.pallas_call(
        functools.partial(_branch_b_short_kernel, nb=nb, t=t),
        grid=(b // nb,),
        in_specs=[
            pl.BlockSpec((rows, D_MODEL), lambda i: (i, 0)),
            pl.BlockSpec((rows, D_MODEL), lambda i: (i, 1)),
            pl.BlockSpec((1, D_MODEL), lambda i: (0, 0)),
            pl.BlockSpec((t, t, D_MODEL), lambda i: (0, 0, 0)),
            pl.BlockSpec((t, D_MODEL), lambda i: (0, 0)),
        ],
        out_specs=[
            pl.BlockSpec((rows, D_MODEL), lambda i: (i, 0)),
            pl.BlockSpec((nb, t, D_MODEL), lambda i: (i, 0, 0)),
        ],
        out_shape=[
            jax.ShapeDtypeStruct((grp.m, D_MODEL), BF16),
            jax.ShapeDtypeStruct((b, t, D_MODEL), F32),
        ],
        compiler_params=_params(("arbitrary",),
                                [((rows, D_MODEL), BF16)] * 3 + [((rows, D_MODEL), F32),
                                                                  ((t, t, D_MODEL), F32)],
                                [((rows, D_MODEL), F32)] * 3),
        name="branch_b_short",
    )(gel, gel, gv.reshape(1, D_MODEL), wc, bias)


def _merge_kernel(ya_ref, yb_ref, sa_ref, sb_ref, wpa_ref, wpb_ref, o_ref, wa_ref, wb_ref):
    @pl.when(pl.program_id(1) == 0)
    def _():
        wa_ref[...] = wpa_ref[...].astype(BF16)
        wb_ref[...] = wpb_ref[...].astype(BF16)

    pa = jnp.dot(ya_ref[...], wa_ref[...], preferred_element_type=F32)
    pb = jnp.dot(yb_ref[...], wb_ref[...], preferred_element_type=F32)
    o_ref[...] = (sa_ref[...].astype(F32) * pa + sb_ref[...].astype(F32) * pb).astype(BF16)


def _merge(grp, ya, yb, sig, wpa, wpb, layer, jobs=None):
    tm, tn = 1024, 512
    nj = D_MODEL // tn
    return _hosted_call(
        _merge_kernel, [ya, yb, sig, sig, wpa, wpb], jobs, CAST_SLAB_BYTES // 2,
        grid=(nj, grp.m // tm),
        in_specs=[
            pl.BlockSpec((tm, D_MODEL), lambda j, i: (i, 0)),
            pl.BlockSpec((tm, D_MODEL), lambda j, i: (i, 0)),
            pl.BlockSpec((tm, tn), lambda j, i: (i, j)),
            pl.BlockSpec((tm, tn), lambda j, i: (i, nj + j)),
            pl.BlockSpec((None, D_MODEL, tn), lambda j, i: (layer, 0, j)),
            pl.BlockSpec((None, D_MODEL, tn), lambda j, i: (layer, 0, j)),
        ],
        out_specs=[pl.BlockSpec((tm, tn), lambda j, i: (i, j))],
        out_shape=[jax.ShapeDtypeStruct((grp.m, D_MODEL), BF16)],
        scratch_shapes=[pltpu.VMEM((D_MODEL, tn), BF16), pltpu.VMEM((D_MODEL, tn), BF16)],
        blocks=[((tm, D_MODEL), BF16)] * 2 + [((tm, tn), BF16)] * 3 + [((D_MODEL, tn), F32)] * 2,
        scratch=[((D_MODEL, tn), BF16)] * 2 + [((tm, tn), F32)] * 2,
        name="merge",
    )[0]


def _post_math(y, x, gpost, gt, nxt):
    x1 = x + gt * _rms(y, gpost)
    if nxt is None:
        return x1, None
    gpre, one_plus_sc, sh = nxt
    return x1, _rms(x1, gpre) * one_plus_sc + sh


def _next_mods(gpre_ref, sc_ref, sh_ref, rows):
    return gpre_ref[...], _mod_rows(sc_ref, rows, 1.0), _mod_rows(sh_ref, rows)


def _oproj_post_kernel(m_ref, wo_ref, x_ref, gpost_ref, gt_ref, gpre_ref, sc_ref, sh_ref,
                       xo_ref, xn_ref):
    y = jnp.dot(m_ref[...], wo_ref[...], preferred_element_type=F32)
    rows = x_ref.shape[0]
    x1, xn = _post_math(y, x_ref[...], gpost_ref[...], _mod_rows(gt_ref, rows),
                        _next_mods(gpre_ref, sc_ref, sh_ref, rows))
    xo_ref[...] = x1
    xn_ref[...] = xn.astype(xn_ref.dtype)


def _oproj_post(grp, m, wo, layer, x, gpost, gt, gpre, sc, sh, xn_dtype, jobs=None):
    tm = 512
    vec = lambda: pl.BlockSpec((1, D_MODEL), lambda i: (0, 0))
    row = lambda: pl.BlockSpec((tm, D_MODEL), lambda i: (i, 0))
    return _hosted_call(
        _oproj_post_kernel,
        [m, wo, x, gpost.reshape(1, D_MODEL), gt, gpre.reshape(1, D_MODEL), sc, sh],
        jobs, CAST_SLAB_BYTES,
        grid=(grp.m // tm,),
        in_specs=[
            row(),
            pl.BlockSpec((None, D_MODEL, D_MODEL), lambda i: (layer, 0, 0),
                         pipeline_mode=pl.Buffered(1)),
            row(), vec(), grp.mod_spec(tm), vec(), grp.mod_spec(tm), grp.mod_spec(tm),
        ],
        out_specs=[row(), row()],
        out_shape=[
            jax.ShapeDtypeStruct((grp.m, D_MODEL), F32),
            jax.ShapeDtypeStruct((grp.m, D_MODEL), xn_dtype),
        ],
        scratch_shapes=[],
        blocks=[((tm, D_MODEL), F32)] * 5,
        scratch=[((D_MODEL, D_MODEL), BF16), ((tm, D_MODEL), F32)],
        name="oproj_post",
    )


def _post_kernel(y_ref, x_ref, gpost_ref, gt_ref, *rest, with_next):
    rows = x_ref.shape[0]
    if with_next:
        gpre_ref, sc_ref, sh_ref, xo_ref, xn_ref = rest
        nxt = _next_mods(gpre_ref, sc_ref, sh_ref, rows)
    else:
        (xo_ref,) = rest
        nxt = None
    x1, xn = _post_math(y_ref[...], x_ref[...], gpost_ref[...], _mod_rows(gt_ref, rows), nxt)
    xo_ref[...] = x1
    if with_next:
        xn_ref[...] = xn.astype(xn_ref.dtype)


def _post_specs(grp, tm, x, gpost, gt, nxt, imap):
    vec = lambda: pl.BlockSpec((1, D_MODEL), lambda *ids: (0, 0))
    row = lambda: pl.BlockSpec((tm, D_MODEL), imap)
    in_specs = [row(), vec(), grp.mod_spec(tm)]
    args = [x, gpost.reshape(1, D_MODEL), gt]
    out_specs = [row()]
    out_shape = [jax.ShapeDtypeStruct((grp.m, D_MODEL), F32)]
    if nxt is not None:
        gpre, sc, sh = nxt
        in_specs += [vec(), grp.mod_spec(tm), grp.mod_spec(tm)]
        args += [gpre.reshape(1, D_MODEL), sc, sh]
        out_specs.append(row())
        out_shape.append(jax.ShapeDtypeStruct((grp.m, D_MODEL), BF16))
    return in_specs, args, out_specs, out_shape


def _post(grp, y, x, gpost, gt, nxt):
    tm = 512
    imap = lambda i: (i, 0)
    in_specs, args, out_specs, out_shape = _post_specs(grp, tm, x, gpost, gt, nxt, imap)
    out = pl.pallas_call(
        functools.partial(_post_kernel, with_next=nxt is not None),
        grid=(grp.m // tm,),
        in_specs=[pl.BlockSpec((tm, D_MODEL), imap)] + in_specs,
        out_specs=out_specs,
        out_shape=out_shape,
        compiler_params=_params(("arbitrary",), [((tm, D_MODEL), F32)] * 8),
        name="post",
    )(y, *args)
    return (out[0], out[1]) if nxt is not None else (out[0], None)


def _swiglu_tile(x, wg_ref, wu_ref, wd_ref):
    g = jnp.dot(x, wg_ref[...], preferred_element_type=F32)
    u = jnp.dot(x, wu_ref[...], preferred_element_type=F32)
    h = _silu(g) * u
    return jnp.dot(h.astype(BF16), wd_ref[...], preferred_element_type=F32)


def _swiglu_kernel(x_ref, wg_ref, wu_ref, wd_ref, o_ref):
    @pl.when(pl.program_id(1) == 0)
    def _():
        o_ref[...] = jnp.zeros_like(o_ref)

    o_ref[...] += _swiglu_tile(x_ref[...], wg_ref, wu_ref, wd_ref)


def _swiglu(grp, x, wg, wu, wd):
    f = wg.shape[-1]
    tm, tf = 1024, 512
    return pl.pallas_call(
        _swiglu_kernel,
        grid=(grp.m // tm, f // tf),
        in_specs=[
            pl.BlockSpec((tm, D_MODEL), lambda i, j: (i, 0)),
            pl.BlockSpec((D_MODEL, tf), lambda i, j: (0, j)),
            pl.BlockSpec((D_MODEL, tf), lambda i, j: (0, j)),
            pl.BlockSpec((tf, D_MODEL), lambda i, j: (j, 0)),
        ],
        out_specs=pl.BlockSpec((tm, D_MODEL), lambda i, j: (i, 0)),
        out_shape=jax.ShapeDtypeStruct((grp.m, D_MODEL), F32),
        compiler_params=_params(("arbitrary", "arbitrary"),
                                [((tm, D_MODEL), BF16), ((tm, D_MODEL), F32)]
                                + [((D_MODEL, tf), BF16)] * 3,
                                [((tm, tf), F32)] * 4),
        name="swiglu_dense",
    )(x, wg, wu, wd)


META_E1, META_E2, META_R1, META_R2, META_W1, META_W2 = range(6)
DMA_UNROLL = 16


def _split_bf16(v):
    hi = v.astype(BF16)
    return hi, (v - hi.astype(F32)).astype(BF16)


def _router_kernel(x_ref, rw_ref, rb_ref, meta_ref, cnt_ref, tri_ref):
    tm = x_ref.shape[0]

    @pl.when(pl.program_id(0) == 0)
    def _():
        cnt_ref[...] = jnp.zeros_like(cnt_ref)
        tri_ref[...] = (lax.broadcasted_iota(jnp.int32, (tm, tm), 0)
                        >= lax.broadcasted_iota(jnp.int32, (tm, tm), 1)).astype(BF16)

    xh, xl = _split_bf16(x_ref[...])
    wh, wl = _split_bf16(rw_ref[...])
    logits = (jnp.dot(xh, wh, preferred_element_type=F32) + jnp.dot(xl, wh, preferred_element_type=F32)
              + jnp.dot(xh, wl, preferred_element_type=F32)) + rb_ref[...]
    lane = lax.broadcasted_iota(jnp.int32, logits.shape, 1)
    logits = jnp.where(lane < N_EXPERTS, logits, -jnp.inf)
    ex = jnp.exp(logits - jnp.max(logits, axis=-1, keepdims=True))
    p = ex / jnp.sum(ex, axis=-1, keepdims=True)
    p1 = jnp.max(p, axis=-1, keepdims=True)
    i1 = jnp.min(jnp.where(p == p1, lane, LANES), axis=-1, keepdims=True)
    rest = jnp.where(lane == i1, -1.0, p)
    p2 = jnp.max(rest, axis=-1, keepdims=True)
    i2 = jnp.min(jnp.where(rest == p2, lane, LANES), axis=-1, keepdims=True)
    denom = p1 + p2
    onehot = jnp.where(lane == i1, 1.0, 0.0) + jnp.where(lane == i2, 1.0, 0.0)
    cum = jnp.dot(tri_ref[...], onehot.astype(BF16), preferred_element_type=F32) + cnt_ref[...]
    cnt_ref[...] = cum[tm - 1:tm, :]
    r1 = jnp.sum(jnp.where(lane == i1, cum, 0.0), axis=-1, keepdims=True) - 1.0
    r2 = jnp.sum(jnp.where(lane == i2, cum, 0.0), axis=-1, keepdims=True) - 1.0
    meta = jnp.zeros_like(logits)
    for k, v in ((META_E1, i1.astype(F32)), (META_E2, i2.astype(F32)), (META_R1, r1), (META_R2, r2),
                 (META_W1, p1 / denom), (META_W2, p2 / denom)):
        meta = jnp.where(lane == k, v, meta)
    meta_ref[...] = meta


def _router(grp, x, rw_pad, rb_pad):
    tm = 512
    return pl.pallas_call(
        _router_kernel,
        grid=(grp.m // tm,),
        in_specs=[
            pl.BlockSpec((tm, D_MODEL), lambda i: (i, 0)),
            pl.BlockSpec((D_MODEL, LANES), lambda i: (0, 0)),
            pl.BlockSpec((1, LANES), lambda i: (0, 0)),
        ],
        out_specs=[
            pl.BlockSpec((tm, LANES), lambda i: (i, 0)),
            pl.BlockSpec((1, LANES), lambda i: (0, 0)),
        ],
        out_shape=[
            jax.ShapeDtypeStruct((grp.m, LANES), F32),
            jax.ShapeDtypeStruct((1, LANES), F32),
        ],
        scratch_shapes=[pltpu.VMEM((tm, tm), BF16)],
        compiler_params=_params(("arbitrary",),
                                [((tm, D_MODEL), F32), ((D_MODEL, LANES), F32), ((tm, LANES), F32)],
                                [((tm, tm), BF16), ((tm, D_MODEL), F32)]),
        name="router",
    )(x, rw_pad, rb_pad)


def _route_plan(metas, counts, tg, n_tiles):
    cnts = [c[0, :N_EXPERTS].astype(jnp.int32) for c in counts]
    total = sum(cnts)
    padded = (total + tg - 1) // tg * tg
    ends = jnp.cumsum(padded)
    offs = ends - padded
    slots = []
    for meta, before in zip(metas, [sum(cnts[:g], jnp.zeros_like(total)) for g in range(len(cnts))]):
        idx = meta[:, :META_W1].astype(jnp.int32)
        start = offs + before
        slots.append((start[idx[:, META_E1]] + idx[:, META_R1], start[idx[:, META_E2]] + idx[:, META_R2]))
    n_used = ends[-1] // tg
    tile = jnp.arange(n_tiles, dtype=jnp.int32)
    tile_c = jnp.minimum(tile, n_used - 1)
    tile_expert = jnp.sum(tile_c[:, None] * tg >= ends[None, :], axis=1).astype(jnp.int32)
    return slots, tile_expert, n_used.reshape(1).astype(jnp.int32)


def _dispatch_kernel(pos1_ref, pos2_ref, x_ref, xs_in_ref, xs_ref, sem):
    del xs_in_ref
    tm = x_ref.shape[0]
    base = pl.program_id(0) * tm

    def row_copy(t, slot):
        return pltpu.make_async_copy(x_ref.at[pl.ds(t, 1)], xs_ref.at[pl.ds(slot, 1)], sem)

    def start(t, c):
        row_copy(t, pos1_ref[base + t]).start()
        row_copy(t, pos2_ref[base + t]).start()
        return c

    def wait(t, c):
        row_copy(t, 0).wait()
        row_copy(t, 0).wait()
        return c

    lax.fori_loop(0, tm, start, 0, unroll=DMA_UNROLL)
    lax.fori_loop(0, tm, wait, 0, unroll=DMA_UNROLL)


def _dispatch(grp, x, pos1, pos2, xs):
    tm = 1024
    return pl.pallas_call(
        _dispatch_kernel,
        grid_spec=pltpu.PrefetchScalarGridSpec(
            num_scalar_prefetch=2,
            grid=(grp.m // tm,),
            in_specs=[
                pl.BlockSpec((tm, D_MODEL), lambda i, p1, p2: (i, 0)),
                pl.BlockSpec(memory_space=pl.ANY),
            ],
            out_specs=pl.BlockSpec(memory_space=pl.ANY),
            scratch_shapes=[pltpu.SemaphoreType.DMA(())],
        ),
        out_shape=jax.ShapeDtypeStruct(xs.shape, F32),
        input_output_aliases={3: 0},
        compiler_params=_params(("arbitrary",), [((tm, D_MODEL), F32)]),
        name="moe_dispatch",
    )(pos1, pos2, x, xs)


def _experts_kernel(te_ref, nu_ref, xs_ref, wg_ref, wu_ref, wd_ref, o_ref):
    del te_ref
    r, j = pl.program_id(0), pl.program_id(1)

    @pl.when(j == 0)
    def _():
        o_ref[...] = jnp.zeros_like(o_ref)

    @pl.when(r < nu_ref[0])
    def _():
        o_ref[...] += _swiglu_tile(xs_ref[...].astype(BF16), wg_ref, wu_ref, wd_ref)


def _experts(xs, tile_expert, n_used, wg, wu, wd, tg):
    n_rows = xs.shape[0]
    f = wg.shape[-1]
    tf = 1024
    n_j = f // tf

    def used(r, nu):
        return r < nu[0]

    def x_map(r, j, te, nu):
        return (jnp.minimum(r, nu[0] - 1), 0)

    def up_map(r, j, te, nu):
        return (te[r], 0, jnp.where(used(r, nu), j, n_j - 1))

    def down_map(r, j, te, nu):
        return (te[r], jnp.where(used(r, nu), j, n_j - 1), 0)

    return pl.pallas_call(
        _experts_kernel,
        grid_spec=pltpu.PrefetchScalarGridSpec(
            num_scalar_prefetch=2,
            grid=(n_rows // tg, n_j),
            in_specs=[
                pl.BlockSpec((tg, D_MODEL), x_map),
                pl.BlockSpec((None, D_MODEL, tf), up_map),
                pl.BlockSpec((None, D_MODEL, tf), up_map),
                pl.BlockSpec((None, tf, D_MODEL), down_map),
            ],
            out_specs=pl.BlockSpec((tg, D_MODEL), lambda r, j, te, nu: (r, 0)),
        ),
        out_shape=jax.ShapeDtypeStruct((n_rows, D_MODEL), F32),
        compiler_params=_params(("arbitrary", "arbitrary"),
                                [((tg, D_MODEL), F32)] * 2 + [((D_MODEL, tf), BF16)] * 3,
                                [((tg, tf), F32)] * 4 + [((tg, D_MODEL), BF16)]),
        name="moe_experts",
    )(tile_expert, n_used, xs, wg, wu, wd)


def _combine_post_kernel(pos1_ref, pos2_ref, ys_ref, meta_ref, x_ref, gpost_ref, gt_ref, *rest,
                         with_next):
    *rest, buf1_ref, buf2_ref, sem = rest
    tm = x_ref.shape[0]
    step, n_steps = pl.program_id(0), pl.num_programs(0)
    cur = step % 2

    def row_copy(row, buf_ref, half, t):
        return pltpu.make_async_copy(ys_ref.at[pl.ds(row, 1)], buf_ref.at[half, pl.ds(t, 1)],
                                     sem.at[half])

    def gather(tile, half):
        def start(t, c):
            row_copy(pos1_ref[tile * tm + t], buf1_ref, half, t).start()
            row_copy(pos2_ref[tile * tm + t], buf2_ref, half, t).start()
            return c
        lax.fori_loop(0, tm, start, 0, unroll=DMA_UNROLL)

    @pl.when(step == 0)
    def _():
        gather(0, 0)

    @pl.when(step + 1 < n_steps)
    def _():
        gather(step + 1, 1 - cur)

    def wait(t, c):
        row_copy(0, buf1_ref, cur, t).wait()
        row_copy(0, buf2_ref, cur, t).wait()
        return c

    lax.fori_loop(0, tm, wait, 0, unroll=DMA_UNROLL)
    meta = meta_ref[...]
    y = meta[:, META_W1:META_W1 + 1] * buf1_ref[cur] + meta[:, META_W2:META_W2 + 1] * buf2_ref[cur]
    if with_next:
        gpre_ref, sc_ref, sh_ref, xo_ref, xn_ref = rest
        nxt = _next_mods(gpre_ref, sc_ref, sh_ref, tm)
    else:
        (xo_ref,) = rest
        nxt = None
    x1, xn = _post_math(y, x_ref[...], gpost_ref[...], _mod_rows(gt_ref, tm), nxt)
    xo_ref[...] = x1
    if with_next:
        xn_ref[...] = xn.astype(xn_ref.dtype)


def _combine_post(grp, ys, pos1, pos2, meta, x, gpost, gt, nxt):
    tm = 256
    imap = lambda i, p1, p2: (i, 0)
    in_specs, args, out_specs, out_shape = _post_specs(grp, tm, x, gpost, gt, nxt, imap)
    out = pl.pallas_call(
        functools.partial(_combine_post_kernel, with_next=nxt is not None),
        grid_spec=pltpu.PrefetchScalarGridSpec(
            num_scalar_prefetch=2,
            grid=(grp.m // tm,),
            in_specs=[pl.BlockSpec(memory_space=pl.ANY), pl.BlockSpec((tm, LANES), imap)] + in_specs,
            out_specs=out_specs,
            scratch_shapes=[pltpu.VMEM((2, tm, D_MODEL), F32), pltpu.VMEM((2, tm, D_MODEL), F32),
                            pltpu.SemaphoreType.DMA((2,))],
        ),
        out_shape=out_shape,
        compiler_params=_params(("arbitrary",), [((tm, D_MODEL), F32)] * 7,
                                [((tm, D_MODEL), F32)] * 4),
        name="moe_combine_post",
    )(pos1, pos2, ys, meta, *args)
    return (out[0], out[1]) if nxt is not None else (out[0], None)


def _moe_post(grps, xns, p, j, xs_res, gpost, gts, nxts):
    tg = 512
    n_tiles = 2 * sum(g.m for g in grps) // tg + N_EXPERTS
    rw_pad = jnp.pad(p['router_w'][j], ((0, 0), (0, LANES - N_EXPERTS)))
    rb_pad = jnp.pad(p['router_b'][j], (0, LANES - N_EXPERTS)).reshape(1, LANES)
    routed = [_router(g, xn, rw_pad, rb_pad) for g, xn in zip(grps, xns)]
    metas, counts = zip(*routed)
    slots, tile_expert, n_used = _route_plan(metas, counts, tg, n_tiles)
    xs = jnp.zeros((n_tiles * tg, D_MODEL), F32)
    for g, xn, (pos1, pos2) in zip(grps, xns, slots):
        xs = _dispatch(g, xn, pos1, pos2, xs)
    wg, wu, wd = (p['jobs'].result(k)[j] for k in ('moe_wg', 'moe_wu', 'moe_wd'))
    ys = _experts(xs, tile_expert, n_used, wg, wu, wd, tg)
    return [_combine_post(g, ys, pos1, pos2, meta, x, gpost, gt, nxt)
            for g, (pos1, pos2), meta, x, gt, nxt in zip(grps, slots, metas, xs_res, gts, nxts)]


def _mixer(grp, l, x, xn, mods, h0, conv0, p, a_tiles, xn_dtype):
    _, _, gt1, sh2, sc2, _ = mods
    w_in = p['w_in']
    ya, h_last, conv_new = _branch_a(
        grp, *a_tiles, xn, p['w_ag_bf'], l, conv0, h0, p['conv_w'][l], p['conv_b'][l],
        p['w_r'][l], p['w_i'][l], p['b_r'][l], p['b_i'][l], p['lru_lambda'][l], p['jobs'])
    gel = _mm_act(grp, xn, w_in, l, 2 * D_MODEL, 2 * D_MODEL, _gelu, BF16, "in_proj_gelu", p['jobs'])
    sig = _mm_act(grp, xn, w_in, l, 4 * D_MODEL, 2 * D_MODEL, _sigmoid, BF16, "in_proj_sigmoid", p['jobs'])
    if grp.t >= CHUNK:
        bias_full = jnp.repeat(p['b_s'][l].T, GROUP_B, axis=-1)
        yb = _branch_b(grp, gel, p['g_v'][l], p['w_s'][l], bias_full)
        vn = None
    else:
        yb, vn = _branch_b_short(grp, gel, p['g_v'][l], p['w_s'][l], p['b_s'][l])
    m = _merge(grp, ya, yb, sig, p['w_pa'], p['w_pb'], l, p['jobs'])
    x, xn2 = _oproj_post(grp, m, p['w_o_bf'], l, x, p['g_post_mix'][l], gt1,
                         p['g_pre_ffn'][l], sc2, sh2, xn_dtype, p['jobs'])
    return x, xn2, h_last.reshape(grp.b, D_MODEL), conv_new, vn


def _trunk(grps, xs, mods, h0s, conv0s, p, tiles):
    depth = p['w_in'].shape[0]
    n_g = len(grps)
    xns = [_prenorm(g, x, p['g_pre_mix'][0], mods[i][0][1], mods[i][0][0])
           for i, (g, x) in enumerate(zip(grps, xs))]
    states = [([], [], []) for _ in grps]
    for l in range(depth):
        dense = l % 2 == 0
        j = l // 2
        xn2s = []
        for i, g in enumerate(grps):
            xs[i], xn2, h_last, conv_new, vn = _mixer(
                g, l, xs[i], xns[i], mods[i][l], h0s[i][l], conv0s[i][l], p, tiles[i],
                BF16 if dense else F32)
            xn2s.append(xn2)
            for acc, v in zip(states[i], (h_last, conv_new, vn)):
                acc.append(v)
        if l + 1 < depth:
            nxts = [(p['g_pre_mix'][l + 1], mods[i][l + 1][1], mods[i][l + 1][0]) for i in range(n_g)]
        else:
            nxts = [None] * n_g
        gt2s = [mods[i][l][5] for i in range(n_g)]
        gpost = p['g_post_ffn'][l]
        if dense:
            outs = []
            for i, g in enumerate(grps):
                wg, wu, wd = (p['jobs'].result(k)[j] for k in ('ffn_wg', 'ffn_wu', 'ffn_wd'))
                y = _swiglu(g, xn2s[i], wg, wu, wd)
                outs.append(_post(g, y, xs[i], gpost, gt2s[i], nxts[i]))
        else:
            outs = _moe_post(grps, xn2s, p, j, xs, gpost, gt2s, nxts)
        xs = [o[0] for o in outs]
        xns = [o[1] for o in outs]
    return xs, states


def kernel(x_prompt, x_sample, c_prompt, c_sample, state_lru_h, state_lru_conv, w_ada, b_ada, g_pre_mix, g_post_mix, g_pre_ffn, g_post_ffn, w_in, conv_w, conv_b, w_r, b_r, w_i, b_i, lru_lambda, g_v, w_s, b_s, w_pa, w_pb, w_o, ffn_wg, ffn_wu, ffn_wd, router_w, router_b, moe_wg, moe_wu, moe_wd):
    p = dict(g_pre_mix=g_pre_mix, g_post_mix=g_post_mix, g_pre_ffn=g_pre_ffn, g_post_ffn=g_post_ffn,
             w_in=w_in, conv_w=conv_w, conv_b=conv_b, w_r=w_r, b_r=b_r, w_i=w_i, b_i=b_i,
             lru_lambda=lru_lambda, g_v=g_v, w_s=w_s, b_s=b_s, w_pa=w_pa, w_pb=w_pb,
             router_w=router_w, router_b=router_b, w_o_bf=w_o.astype(BF16),
             w_ag_bf=w_in[:, :, :2 * D_MODEL].astype(BF16),
             jobs=_CastJobs(dict(moe_wg=moe_wg, moe_wu=moe_wu, moe_wd=moe_wd,
                                 ffn_wg=ffn_wg, ffn_wu=ffn_wu, ffn_wd=ffn_wd)))
    depth = w_in.shape[0]
    bp, tp, _ = x_prompt.shape
    bs, ts, _ = x_sample.shape
    assert tp % CHUNK == 0 and ts < CHUNK and ts % SUBLANES == 0

    n_c = bp + bs
    r_pad = -n_c % 16
    c_all = jnp.concatenate([c_sample, c_prompt, jnp.zeros((r_pad, D_MODEL), F32)], axis=0)
    mod = _ada(c_all, w_ada, b_ada)
    mods_s = [[mod[l, k, :bs] for k in range(6)] for l in range(depth)]
    mods_p = [[mod[l, k, bs:n_c].reshape(bp, 1, D_MODEL) for k in range(6)] for l in range(depth)]

    grp_p = _Group(bp, tp, short=False)
    grp_s = _Group(bs, ts, short=True)
    zeros_h = jnp.zeros((depth, bp, D_MODEL), F32)
    zeros_conv = jnp.zeros((depth, bp, CONV_W - 1, D_MODEL), F32)

    (y_p, y_s), (st_p, st_s) = _trunk(
        [grp_p, grp_s],
        [x_prompt.reshape(bp * tp, D_MODEL), x_sample.reshape(bs * ts, D_MODEL)],
        [mods_p, mods_s], [zeros_h, state_lru_h], [zeros_conv, state_lru_conv], p,
        tiles=[(1024, 1024, 1, 256), (512, 1024, 16, ts)])
    return (y_p.reshape(bp, tp, D_MODEL), y_s.reshape(bs, ts, D_MODEL),
            jnp.stack(st_p[0]), jnp.stack(st_p[1]), jnp.stack(st_s[0]), jnp.stack(st_s[1]),
            jnp.stack(st_s[2]))
```

```python
import functools

import jax
import jax.numpy as jnp
from jax import lax
from jax.experimental import pallas as pl
from jax.experimental.pallas import tpu as pltpu

F32 = jnp.float32
BF16 = jnp.bfloat16

D_MODEL = 2048
N_HEADS_A = 16
HEAD_A = D_MODEL // N_HEADS_A
CONV_W = 4
C_GATE = 8.0
N_GROUPS_B = 16
GROUP_B = D_MODEL // N_GROUPS_B
CHUNK = 128
N_EXPERTS = 8
EPS = 1e-6

LANES = 128
SUBLANES = 8
VMEM_CAP = 58 * 2**20
CONV_PAD = SUBLANES
CAST_SLAB_BYTES = 4 * 2**20


def _nbytes(shape, dtype):
    n = 1
    for s in shape:
        n *= s
    return n * jnp.dtype(dtype).itemsize


def _params(sem, blocks, scratch=()):
    need = 2 * sum(_nbytes(s, d) for s, d in blocks) + sum(_nbytes(s, d) for s, d in scratch)
    limit = min(VMEM_CAP, need + 16 * 2**20)
    return pltpu.CompilerParams(dimension_semantics=sem, vmem_limit_bytes=limit)


def _rms(x, g):
    return x * lax.rsqrt(jnp.mean(x * x, axis=-1, keepdims=True) + EPS) * g


def _sigmoid(x):
    return 1.0 / (1.0 + jnp.exp(-x))


def _sigmoid_tanh(x):
    return 0.5 * jnp.tanh(0.5 * x) + 0.5


def _silu(x):
    return x * _sigmoid(x)


def _gelu(x):
    return jax.nn.gelu(x)


class _Group:
    def __init__(self, b, t, short):
        self.b, self.t, self.m = b, t, b * t
        self.short = short

    def mod_spec(self, tm):
        if self.short:
            return pl.BlockSpec((tm // self.t, D_MODEL), lambda *ids: (ids[0], 0))
        t = self.t
        return pl.BlockSpec((None, 1, D_MODEL), lambda *ids: ((ids[0] * tm) // t, 0, 0))


def _mod_rows(ref, rows, offset=None):
    n = ref.shape[0]
    vals = ref[...] if offset is None else ref[...] + offset
    if n == 1:
        return vals
    rep = rows // n
    return jnp.concatenate([jnp.broadcast_to(vals[r:r + 1, :], (rep, D_MODEL)) for r in range(n)], axis=0)


def _ada_kernel(c_ref, w_ref, b_ref, o_ref):
    s = _silu(c_ref[...]).astype(BF16)
    o_ref[...] = jnp.dot(s, w_ref[...].astype(BF16), preferred_element_type=F32) + b_ref[...]


def _ada(c, w_ada, b_ada):
    depth, _, n = w_ada.shape
    r = c.shape[0]
    tn = 1024
    per_item = D_MODEL // tn
    return pl.pallas_call(
        _ada_kernel,
        grid=(depth, n // tn),
        in_specs=[
            pl.BlockSpec((r, D_MODEL), lambda l, j: (0, 0)),
            pl.BlockSpec((None, D_MODEL, tn), lambda l, j: (l, 0, j)),
            pl.BlockSpec((None, 1, tn), lambda l, j: (l, 0, j)),
        ],
        out_specs=pl.BlockSpec((None, None, r, tn), lambda l, j: (l, j // per_item, 0, j % per_item)),
        out_shape=jax.ShapeDtypeStruct((depth, n // D_MODEL, r, D_MODEL), F32),
        compiler_params=_params(("arbitrary", "arbitrary"),
                                [((r, D_MODEL), F32), ((D_MODEL, tn), F32), ((r, tn), F32)],
                                [((D_MODEL, tn), BF16)]),
        name="ada_mod",
    )(c, w_ada, b_ada.reshape(depth, 1, n))


def _prenorm_kernel(x_ref, g_ref, sc_ref, sh_ref, o_ref):
    rows = x_ref.shape[0]
    o_ref[...] = (_rms(x_ref[...], g_ref[...]) * _mod_rows(sc_ref, rows, 1.0)
                  + _mod_rows(sh_ref, rows)).astype(BF16)


def _prenorm(grp, x, g, sc, sh):
    tm = 256 if grp.short else 512
    return pl.pallas_call(
        _prenorm_kernel,
        grid=(grp.m // tm,),
        in_specs=[
            pl.BlockSpec((tm, D_MODEL), lambda i: (i, 0)),
            pl.BlockSpec((1, D_MODEL), lambda i: (0, 0)),
            grp.mod_spec(tm), grp.mod_spec(tm),
        ],
        out_specs=pl.BlockSpec((tm, D_MODEL), lambda i: (i, 0)),
        out_shape=jax.ShapeDtypeStruct((grp.m, D_MODEL), BF16),
        compiler_params=_params(("arbitrary",), [((tm, D_MODEL), F32)] * 4),
        name="prenorm",
    )(x, g.reshape(1, D_MODEL), sc, sh)


class _CastJobs:
    def __init__(self, arrays):
        self.shapes = {k: v.shape for k, v in arrays.items()}
        self.pending = [(k, v.reshape(-1, v.shape[-1])) for k, v in arrays.items()]
        self.done = {}

    def take(self, steps, slab_limit):
        for n, (name, src) in enumerate(self.pending):
            rows = src.shape[0] // steps
            if src.shape[0] % (steps * 2 * SUBLANES) == 0 and _nbytes((rows, src.shape[1]), F32) <= slab_limit:
                return self.pending.pop(n)
        return None

    def put(self, name, out):
        self.done[name] = out.reshape(self.shapes[name])

    def result(self, name):
        for n, (k, src) in enumerate(self.pending):
            if k == name:
                self.pending.pop(n)
                self.put(name, src.astype(BF16))
                break
        return self.done[name]


def _hosted_call(kernel, args, jobs, slab_limit, *, grid, in_specs, out_specs, out_shape,
                 scratch_shapes, blocks, scratch, name):
    steps = 1
    for n in grid:
        steps *= n
    job = jobs.take(steps, slab_limit) if jobs is not None else None
    if job is not None:
        n_in, n_out = len(in_specs), len(out_specs)
        src = job[1]
        slab = (src.shape[0] // steps, src.shape[1])
        strides = [1] * len(grid)
        for k in range(len(grid) - 2, -1, -1):
            strides[k] = strides[k + 1] * grid[k + 1]
        spec = pl.BlockSpec(slab, lambda *ids: (sum(i * s for i, s in zip(ids, strides)), 0))
        in_specs, out_specs = [*in_specs, spec], [*out_specs, spec]
        out_shape = [*out_shape, jax.ShapeDtypeStruct(src.shape, BF16)]
        args, blocks = [*args, src], [*blocks, (slab, F32), (slab, BF16)]
        inner = kernel

        def kernel(*refs):
            refs[n_in + 1 + n_out][...] = refs[n_in][...].astype(BF16)
            inner(*refs[:n_in], *refs[n_in + 1:n_in + 1 + n_out], *refs[n_in + 2 + n_out:])

    out = pl.pallas_call(
        kernel, grid=grid, in_specs=in_specs, out_specs=out_specs, out_shape=out_shape,
        scratch_shapes=scratch_shapes,
        compiler_params=_params(("arbitrary",) * len(grid), blocks, scratch),
        name=name,
    )(*args)
    if job is not None:
        jobs.put(job[0], out[-1])
        out = out[:-1]
    return list(out)


def _mm_act_kernel(x_ref, w_ref, o_ref, wbf_ref, *, act):
    @pl.when(pl.program_id(1) == 0)
    def _():
        wbf_ref[...] = w_ref[...].astype(BF16)

    y = jnp.dot(x_ref[...], wbf_ref[...], preferred_element_type=F32)
    o_ref[...] = act(y).astype(o_ref.dtype)


def _mm_act(grp, x, w, layer, col0, ncols, act, out_dtype, name, jobs=None):
    tm, tn = 1024, 1024
    c0 = col0 // tn
    return _hosted_call(
        functools.partial(_mm_act_kernel, act=act), [x, w], jobs, CAST_SLAB_BYTES,
        grid=(ncols // tn, grp.m // tm),
        in_specs=[
            pl.BlockSpec((tm, D_MODEL), lambda j, i: (i, 0)),
            pl.BlockSpec((None, D_MODEL, tn), lambda j, i: (layer, 0, c0 + j)),
        ],
        out_specs=[pl.BlockSpec((tm, tn), lambda j, i: (i, j))],
        out_shape=[jax.ShapeDtypeStruct((grp.m, ncols), out_dtype)],
        scratch_shapes=[pltpu.VMEM((D_MODEL, tn), BF16)],
        blocks=[((tm, D_MODEL), BF16), ((D_MODEL, tn), F32), ((tm, tn), F32)],
        scratch=[((D_MODEL, tn), BF16), ((tm, tn), F32)],
        name=name,
    )[0]


HEADS_PER_DOT = 2


def _lru_unit(xp_ref, seqs, t0, xsl, ga, carry, cw_ref, cb_ref, wr, wi, br, bi, softplus, sl, nb, tt):
    rows = nb * tt
    n_blk = tt // SUBLANES
    blk_shape = (nb * n_blk, SUBLANES, HEAD_A)
    t_in_blk = lax.broadcasted_iota(jnp.int32, blk_shape, 1)
    xb = xp_ref[seqs, t0:t0 + CONV_PAD + tt, xsl].reshape(nb * (n_blk + 1), SUBLANES, HEAD_A)
    cur = xb.reshape(nb, n_blk + 1, SUBLANES, HEAD_A)[:, 1:].reshape(blk_shape)
    xc = cb_ref[:, sl][None] + cw_ref[CONV_W - 1:CONV_W, sl][None] * cur
    for s in range(1, CONV_W):
        rolled = pltpu.roll(xb, s, axis=1).reshape(nb, n_blk + 1, SUBLANES, HEAD_A)
        shifted = jnp.where(t_in_blk >= s, rolled[:, 1:].reshape(blk_shape), rolled[:, :-1].reshape(blk_shape))
        xc = xc + cw_ref[CONV_W - 1 - s:CONV_W - s, sl][None] * shifted
    xc = xc.reshape(rows, HEAD_A)
    xcb = xc.astype(BF16)
    r = _sigmoid_tanh(jnp.dot(xcb, wr, preferred_element_type=F32) + br)
    i = _sigmoid_tanh(jnp.dot(xcb, wi, preferred_element_type=F32) + bi)
    log_a = (-C_GATE * r) * softplus
    a = jnp.exp(log_a)
    th = jnp.tanh(log_a)
    one_m_a2 = -2.0 * th / (1.0 - th)
    b = jnp.where(one_m_a2 > 0.0, one_m_a2 * lax.rsqrt(one_m_a2), 0.0) * (i * xc)
    a = a.reshape(blk_shape)
    b = b.reshape(blk_shape)
    d = 1
    while d < SUBLANES:
        a_sh = pltpu.roll(a, d, axis=1)
        b_sh = pltpu.roll(b, d, axis=1)
        keep = t_in_blk >= d
        b = jnp.where(keep, a * b_sh + b, b)
        a = jnp.where(keep, a * a_sh, a)
        d *= 2
    a3 = a.reshape(nb, tt, HEAD_A)
    b3 = b.reshape(nb, tt, HEAD_A)
    h_blocks = []
    for k in range(n_blk):
        ks = slice(k * SUBLANES, (k + 1) * SUBLANES)
        h_k = a3[:, ks, :] * carry + b3[:, ks, :]
        carry = h_k[:, SUBLANES - 1:SUBLANES, :]
        h_blocks.append(h_k)
    h = jnp.concatenate(h_blocks, axis=1) if n_blk > 1 else h_blocks[0]
    return ga * h.reshape(rows, HEAD_A), carry


def _branch_a_kernel(x_ref, wxa_ref, wga_ref, conv0_ref, h0_ref, cw_ref, cb_ref, wr_ref, wi_ref,
                     br_ref, bi_ref, lam_ref, ya_ref, hlast_ref, convnew_ref, h_ref, *xp_refs,
                     n_seq, t_tile, nb, tt, tiles_per_seq):
    i = pl.program_id(1)
    hist = CONV_W - 1
    lo = CONV_PAD - hist
    n_col = HEADS_PER_DOT * HEAD_A
    pair_cols = [slice(pr * n_col, (pr + 1) * n_col) for pr in range(len(xp_refs))]

    @pl.when(i % tiles_per_seq == 0)
    def _():
        for xp_ref, ps in zip(xp_refs, pair_cols):
            xp_ref[:, 0:lo, :] = jnp.zeros((n_seq, lo, n_col), F32)
            xp_ref[:, lo:CONV_PAD, :] = conv0_ref[:, :, ps]
        h_ref[...] = h0_ref[...]

    z = -lam_ref[...]
    softplus = jnp.maximum(z, 0.0) + jnp.log1p(jnp.exp(-jnp.abs(z)))
    x = x_ref[...]

    def project(pr):
        ps = pair_cols[pr]
        xa = jnp.dot(x, wxa_ref[:, ps], preferred_element_type=F32)
        xp_refs[pr][:, CONV_PAD:CONV_PAD + t_tile, :] = xa.reshape(n_seq, t_tile, n_col)
        return _gelu(jnp.dot(x, wga_ref[:, ps], preferred_element_type=F32))

    ga_next = project(0)
    for pr, xp_ref in enumerate(xp_refs):
        ga = ga_next
        if pr + 1 < len(xp_refs):
            ga_next = project(pr + 1)
        for hh in range(HEADS_PER_DOT):
            hd = pr * HEADS_PER_DOT + hh
            xsl = slice(hh * HEAD_A, (hh + 1) * HEAD_A)
            sl = slice(hd * HEAD_A, (hd + 1) * HEAD_A)
            wr = wr_ref[hd].astype(BF16)
            wi = wi_ref[hd].astype(BF16)
            for us in range(n_seq // nb):
                seqs = slice(us * nb, (us + 1) * nb)
                carry = h_ref[seqs, :, sl]
                for ut in range(t_tile // tt):
                    r0 = us * nb * t_tile + ut * tt
                    y, carry = _lru_unit(
                        xp_ref, seqs, ut * tt, xsl, ga[r0:r0 + nb * tt, xsl], carry,
                        cw_ref, cb_ref, wr, wi, br_ref[:, sl], bi_ref[:, sl], softplus[:, sl], sl, nb, tt)
                    ya_ref[r0:r0 + nb * tt, sl] = y.astype(BF16)
                h_ref[seqs, :, sl] = carry

    for xp_ref in xp_refs:
        xp_ref[:, lo:CONV_PAD, :] = xp_ref[:, lo + t_tile:CONV_PAD + t_tile, :]

    @pl.when(i % tiles_per_seq == tiles_per_seq - 1)
    def _():
        hlast_ref[...] = h_ref[...]
        for xp_ref, ps in zip(xp_refs, pair_cols):
            convnew_ref[:, :, ps] = xp_ref[:, lo:CONV_PAD, :]


def _branch_a(grp, tm, tc, nb, tt, xn, w_ag, layer, conv0, h0, cw, cb, wr, wi, br, bi, lam, jobs=None):
    b, t = grp.b, grp.t
    t_tile = min(t, tm)
    n_seq = tm // t_tile
    tiles_per_seq = t // t_tile
    assert (nb == 1 or tt == t_tile) and n_seq % nb == 0 and t_tile % tt == 0
    hist = CONV_W - 1
    n_j = D_MODEL // tc
    seq_blk = lambda j, i: (i // tiles_per_seq, 0, j)
    vec = lambda: pl.BlockSpec((1, tc), lambda j, i: (0, j))
    head_w = lambda: pl.BlockSpec((tc // HEAD_A, HEAD_A, HEAD_A), lambda j, i: (j, 0, 0))
    return _hosted_call(
        functools.partial(_branch_a_kernel, n_seq=n_seq, t_tile=t_tile, nb=nb, tt=tt,
                          tiles_per_seq=tiles_per_seq),
        [xn, w_ag, w_ag, conv0, h0.reshape(b, 1, D_MODEL), cw, cb.reshape(1, D_MODEL),
         wr, wi, br.reshape(1, D_MODEL), bi.reshape(1, D_MODEL), lam.reshape(1, D_MODEL)],
        jobs, CAST_SLAB_BYTES,
        grid=(n_j, grp.m // tm),
        in_specs=[
            pl.BlockSpec((tm, D_MODEL), lambda j, i: (i, 0)),
            pl.BlockSpec((None, D_MODEL, tc), lambda j, i: (layer, 0, j)),
            pl.BlockSpec((None, D_MODEL, tc), lambda j, i: (layer, 0, n_j + j)),
            pl.BlockSpec((n_seq, hist, tc), seq_blk),
            pl.BlockSpec((n_seq, 1, tc), seq_blk),
            pl.BlockSpec((CONV_W, tc), lambda j, i: (0, j)),
            vec(), head_w(), head_w(), vec(), vec(), vec(),
        ],
        out_specs=[
            pl.BlockSpec((tm, tc), lambda j, i: (i, j)),
            pl.BlockSpec((n_seq, 1, tc), seq_blk),
            pl.BlockSpec((n_seq, hist, tc), seq_blk),
        ],
        out_shape=[
            jax.ShapeDtypeStruct((grp.m, D_MODEL), BF16),
            jax.ShapeDtypeStruct((b, 1, D_MODEL), F32),
            jax.ShapeDtypeStruct((b, hist, D_MODEL), F32),
        ],
        scratch_shapes=[pltpu.VMEM((n_seq, 1, tc), F32)]
        + [pltpu.VMEM((n_seq, CONV_PAD + t_tile, HEADS_PER_DOT * HEAD_A), F32)]
        * (tc // (HEADS_PER_DOT * HEAD_A)),
        blocks=[((tm, D_MODEL), BF16), ((D_MODEL, tc), BF16), ((D_MODEL, tc), BF16),
                ((tm, tc), BF16)] + [((n_seq, SUBLANES, tc), F32)] * 4,
        scratch=[((n_seq, CONV_PAD + t_tile, tc), F32), ((n_seq, SUBLANES, tc), F32),
                 ((tm, 2 * HEADS_PER_DOT * HEAD_A), F32)],
        name="branch_a",
    )


def _branch_b_kernel(u_ref, v_ref, gv_ref, ws_ref, bias_ref, yb_ref, wm_ref, *, n_chunks):
    @pl.when(pl.program_id(0) == 0)
    def _():
        tri = (lax.broadcasted_iota(jnp.int32, (CHUNK, CHUNK), 0)
               >= lax.broadcasted_iota(jnp.int32, (CHUNK, CHUNK), 1))
        for g in range(N_GROUPS_B):
            wm_ref[g] = jnp.where(tri, ws_ref[g], 0.0).astype(BF16)

    vn = _rms(v_ref[...].astype(F32), gv_ref[...]).astype(BF16)
    for g in range(N_GROUPS_B):
        cs = slice(g * GROUP_B, (g + 1) * GROUP_B)
        rhs = jnp.concatenate([vn[c * CHUNK:(c + 1) * CHUNK, cs] for c in range(n_chunks)], axis=1)
        mixed = jnp.dot(wm_ref[g], rhs, preferred_element_type=F32)
        for c in range(n_chunks):
            rs = slice(c * CHUNK, (c + 1) * CHUNK)
            mixed_c = mixed[:, c * GROUP_B:(c + 1) * GROUP_B] + bias_ref[:, cs]
            yb_ref[rs, cs] = (u_ref[rs, cs].astype(F32) * mixed_c).astype(BF16)


def _branch_b(grp, gel, gv, ws, bias_full):
    tm = 512
    return pl.pallas_call(
        functools.partial(_branch_b_kernel, n_chunks=tm // CHUNK),
        grid=(grp.m // tm,),
        in_specs=[
            pl.BlockSpec((tm, D_MODEL), lambda i: (i, 0)),
            pl.BlockSpec((tm, D_MODEL), lambda i: (i, 1)),
            pl.BlockSpec((1, D_MODEL), lambda i: (0, 0)),
            pl.BlockSpec((N_GROUPS_B, CHUNK, CHUNK), lambda i: (0, 0, 0)),
            pl.BlockSpec((CHUNK, D_MODEL), lambda i: (0, 0)),
        ],
        out_specs=pl.BlockSpec((tm, D_MODEL), lambda i: (i, 0)),
        out_shape=jax.ShapeDtypeStruct((grp.m, D_MODEL), BF16),
        scratch_shapes=[pltpu.VMEM((N_GROUPS_B, CHUNK, CHUNK), BF16)],
        compiler_params=_params(("arbitrary",),
                                [((tm, D_MODEL), BF16)] * 3 + [((N_GROUPS_B, CHUNK, CHUNK), F32),
                                                                ((CHUNK, D_MODEL), F32)],
                                [((tm, D_MODEL), F32)] * 2),
        name="branch_b",
    )(gel, gel, gv.reshape(1, D_MODEL), ws, bias_full)


def _branch_b_short_kernel(u_ref, v_ref, gv_ref, wc_ref, bias_ref, yb_ref, vn_ref, *, nb, t):
    vn = _rms(v_ref[...].astype(F32), gv_ref[...]).reshape(nb, t, D_MODEL)
    vn_ref[...] = vn
    mixed = bias_ref[...][None]
    for s in range(t):
        mixed = mixed + wc_ref[s][None] * vn[:, s:s + 1, :]
    yb = u_ref[...].astype(F32).reshape(nb, t, D_MODEL) * mixed
    yb_ref[...] = yb.reshape(nb * t, D_MODEL).astype(BF16)


def _branch_b_short(grp, gel, gv, ws, bs):
    b, t = grp.b, grp.t
    nb = 16
    rows = nb * t
    tri = jnp.tril(jnp.ones((t, t), F32))
    wc = jnp.repeat(jnp.transpose(ws[:, :t, :t] * tri, (2, 1, 0)), GROUP_B, axis=-1)
    bias = jnp.repeat(bs[:, :t].T, GROUP_B, axis=-1)
    return pl.pallas_call(
        functools.partial(_branch_b_short_kernel, nb=nb, t=t),
        grid=(b // nb,),
        in_specs=[
            pl.BlockSpec((rows, D_MODEL), lambda i: (i, 0)),
            pl.BlockSpec((rows, D_MODEL), lambda i: (i, 1)),
            pl.BlockSpec((1, D_MODEL), lambda i: (0, 0)),
            pl.BlockSpec((t, t, D_MODEL), lambda i: (0, 0, 0)),
            pl.BlockSpec((t, D_MODEL), lambda i: (0, 0)),
        ],
        out_specs=[
            pl.BlockSpec((rows, D_MODEL), lambda i: (i, 0)),
            pl.BlockSpec((nb, t, D_MODEL), lambda i: (i, 0, 0)),
        ],
        out_shape=[
            jax.ShapeDtypeStruct((grp.m, D_MODEL), BF16),
            jax.ShapeDtypeStruct((b, t, D_MODEL), F32),
        ],
        compiler_params=_params(("arbitrary",),
                                [((rows, D_MODEL), BF16)] * 3 + [((rows, D_MODEL), F32),
                                                                  ((t, t, D_MODEL), F32)],
                                [((rows, D_MODEL), F32)] * 3),
        name="branch_b_short",
    )(gel, gel, gv.reshape(1, D_MODEL), wc, bias)


def _merge_kernel(ya_ref, yb_ref, sa_ref, sb_ref, wpa_ref, wpb_ref, o_ref, wa_ref, wb_ref):
    @pl.when(pl.program_id(1) == 0)
    def _():
        wa_ref[...] = wpa_ref[...].astype(BF16)
        wb_ref[...] = wpb_ref[...].astype(BF16)

    pa = jnp.dot(ya_ref[...], wa_ref[...], preferred_element_type=F32)
    pb = jnp.dot(yb_ref[...], wb_ref[...], preferred_element_type=F32)
    o_ref[...] = (sa_ref[...].astype(F32) * pa + sb_ref[...].astype(F32) * pb).astype(BF16)


def _merge(grp, ya, yb, sig, wpa, wpb, layer, jobs=None):
    tm, tn = 1024, 512
    nj = D_MODEL // tn
    return _hosted_call(
        _merge_kernel, [ya, yb, sig, sig, wpa, wpb], jobs, CAST_SLAB_BYTES // 2,
        grid=(nj, grp.m // tm),
        in_specs=[
            pl.BlockSpec((tm, D_MODEL), lambda j, i: (i, 0)),
            pl.BlockSpec((tm, D_MODEL), lambda j, i: (i, 0)),
            pl.BlockSpec((tm, tn), lambda j, i: (i, j)),
            pl.BlockSpec((tm, tn), lambda j, i: (i, nj + j)),
            pl.BlockSpec((None, D_MODEL, tn), lambda j, i: (layer, 0, j)),
            pl.BlockSpec((None, D_MODEL, tn), lambda j, i: (layer, 0, j)),
        ],
        out_specs=[pl.BlockSpec((tm, tn), lambda j, i: (i, j))],
        out_shape=[jax.ShapeDtypeStruct((grp.m, D_MODEL), BF16)],
        scratch_shapes=[pltpu.VMEM((D_MODEL, tn), BF16), pltpu.VMEM((D_MODEL, tn), BF16)],
        blocks=[((tm, D_MODEL), BF16)] * 2 + [((tm, tn), BF16)] * 3 + [((D_MODEL, tn), F32)] * 2,
        scratch=[((D_MODEL, tn), BF16)] * 2 + [((tm, tn), F32)] * 2,
        name="merge",
    )[0]


def _post_math(y, x, gpost, gt, nxt):
    x1 = x + gt * _rms(y, gpost)
    if nxt is None:
        return x1, None
    gpre, one_plus_sc, sh = nxt
    return x1, _rms(x1, gpre) * one_plus_sc + sh


def _next_mods(gpre_ref, sc_ref, sh_ref, rows):
    return gpre_ref[...], _mod_rows(sc_ref, rows, 1.0), _mod_rows(sh_ref, rows)


def _oproj_post_kernel(m_ref, wo_ref, x_ref, gpost_ref, gt_ref, gpre_ref, sc_ref, sh_ref,
                       xo_ref, xn_ref):
    y = jnp.dot(m_ref[...], wo_ref[...], preferred_element_type=F32)
    rows = x_ref.shape[0]
    x1, xn = _post_math(y, x_ref[...], gpost_ref[...], _mod_rows(gt_ref, rows),
                        _next_mods(gpre_ref, sc_ref, sh_ref, rows))
    xo_ref[...] = x1
    xn_ref[...] = xn.astype(xn_ref.dtype)


def _oproj_post(grp, m, wo, layer, x, gpost, gt, gpre, sc, sh, xn_dtype, jobs=None):
    tm = 512
    vec = lambda: pl.BlockSpec((1, D_MODEL), lambda i: (0, 0))
    row = lambda: pl.BlockSpec((tm, D_MODEL), lambda i: (i, 0))
    return _hosted_call(
        _oproj_post_kernel,
        [m, wo, x, gpost.reshape(1, D_MODEL), gt, gpre.reshape(1, D_MODEL), sc, sh],
        jobs, CAST_SLAB_BYTES,
        grid=(grp.m // tm,),
        in_specs=[
            row(),
            pl.BlockSpec((None, D_MODEL, D_MODEL), lambda i: (layer, 0, 0),
                         pipeline_mode=pl.Buffered(1)),
            row(), vec(), grp.mod_spec(tm), vec(), grp.mod_spec(tm), grp.mod_spec(tm),
        ],
        out_specs=[row(), row()],
        out_shape=[
            jax.ShapeDtypeStruct((grp.m, D_MODEL), F32),
            jax.ShapeDtypeStruct((grp.m, D_MODEL), xn_dtype),
        ],
        scratch_shapes=[],
        blocks=[((tm, D_MODEL), F32)] * 5,
        scratch=[((D_MODEL, D_MODEL), BF16), ((tm, D_MODEL), F32)],
        name="oproj_post",
    )


def _post_kernel(y_ref, x_ref, gpost_ref, gt_ref, *rest, with_next):
    rows = x_ref.shape[0]
    if with_next:
        gpre_ref, sc_ref, sh_ref, xo_ref, xn_ref = rest
        nxt = _next_mods(gpre_ref, sc_ref, sh_ref, rows)
    else:
        (xo_ref,) = rest
        nxt = None
    x1, xn = _post_math(y_ref[...], x_ref[...], gpost_ref[...], _mod_rows(gt_ref, rows), nxt)
    xo_ref[...] = x1
    if with_next:
        xn_ref[...] = xn.astype(xn_ref.dtype)


def _post_specs(grp, tm, x, gpost, gt, nxt, imap):
    vec = lambda: pl.BlockSpec((1, D_MODEL), lambda *ids: (0, 0))
    row = lambda: pl.BlockSpec((tm, D_MODEL), imap)
    in_specs = [row(), vec(), grp.mod_spec(tm)]
    args = [x, gpost.reshape(1, D_MODEL), gt]
    out_specs = [row()]
    out_shape = [jax.ShapeDtypeStruct((grp.m, D_MODEL), F32)]
    if nxt is not None:
        gpre, sc, sh = nxt
        in_specs += [vec(), grp.mod_spec(tm), grp.mod_spec(tm)]
        args += [gpre.reshape(1, D_MODEL), sc, sh]
        out_specs.append(row())
        out_shape.append(jax.ShapeDtypeStruct((grp.m, D_MODEL), BF16))
    return in_specs, args, out_specs, out_shape


def _post(grp, y, x, gpost, gt, nxt):
    tm = 512
    imap = lambda i: (i, 0)
    in_specs, args, out_specs, out_shape = _post_specs(grp, tm, x, gpost, gt, nxt, imap)
    out = pl.pallas_call(
        functools.partial(_post_kernel, with_next=nxt is not None),
        grid=(grp.m // tm,),
        in_specs=[pl.BlockSpec((tm, D_MODEL), imap)] + in_specs,
        out_specs=out_specs,
        out_shape=out_shape,
        compiler_params=_params(("arbitrary",), [((tm, D_MODEL), F32)] * 8),
        name="post",
    )(y, *args)
    return (out[0], out[1]) if nxt is not None else (out[0], None)


def _swiglu_tile(x, wg_ref, wu_ref, wd_ref):
    g = jnp.dot(x, wg_ref[...], preferred_element_type=F32)
    u = jnp.dot(x, wu_ref[...], preferred_element_type=F32)
    h = _silu(g) * u
    return jnp.dot(h.astype(BF16), wd_ref[...], preferred_element_type=F32)


def _swiglu_kernel(x_ref, wg_ref, wu_ref, wd_ref, o_ref):
    @pl.when(pl.program_id(1) == 0)
    def _():
        o_ref[...] = jnp.zeros_like(o_ref)

    o_ref[...] += _swiglu_tile(x_ref[...], wg_ref, wu_ref, wd_ref)


def _swiglu(grp, x, wg, wu, wd):
    f = wg.shape[-1]
    tm, tf = 1024, 512
    return pl.pallas_call(
        _swiglu_kernel,
        grid=(grp.m // tm, f // tf),
        in_specs=[
            pl.BlockSpec((tm, D_MODEL), lambda i, j: (i, 0)),
            pl.BlockSpec((D_MODEL, tf), lambda i, j: (0, j)),
            pl.BlockSpec((D_MODEL, tf), lambda i, j: (0, j)),
            pl.BlockSpec((tf, D_MODEL), lambda i, j: (j, 0)),
        ],
        out_specs=pl.BlockSpec((tm, D_MODEL), lambda i, j: (i, 0)),
        out_shape=jax.ShapeDtypeStruct((grp.m, D_MODEL), F32),
        compiler_params=_params(("arbitrary", "arbitrary"),
                                [((tm, D_MODEL), BF16), ((tm, D_MODEL), F32)]
                                + [((D_MODEL, tf), BF16)] * 3,
                                [((tm, tf), F32)] * 4),
        name="swiglu_dense",
    )(x, wg, wu, wd)


META_E1, META_E2, META_R1, META_R2, META_W1, META_W2 = range(6)
DMA_UNROLL = 16


def _split_bf16(v):
    hi = v.astype(BF16)
    return hi, (v - hi.astype(F32)).astype(BF16)


def _router_kernel(x_ref, rw_ref, rb_ref, meta_ref, cnt_ref, tri_ref):
    tm = x_ref.shape[0]

    @pl.when(pl.program_id(0) == 0)
    def _():
        cnt_ref[...] = jnp.zeros_like(cnt_ref)
        tri_ref[...] = (lax.broadcasted_iota(jnp.int32, (tm, tm), 0)
                        >= lax.broadcasted_iota(jnp.int32, (tm, tm), 1)).astype(BF16)

    xh, xl = _split_bf16(x_ref[...])
    wh, wl = _split_bf16(rw_ref[...])
    logits = (jnp.dot(xh, wh, preferred_element_type=F32) + jnp.dot(xl, wh, preferred_element_type=F32)
              + jnp.dot(xh, wl, preferred_element_type=F32)) + rb_ref[...]
    lane = lax.broadcasted_iota(jnp.int32, logits.shape, 1)
    logits = jnp.where(lane < N_EXPERTS, logits, -jnp.inf)
    ex = jnp.exp(logits - jnp.max(logits, axis=-1, keepdims=True))
    p = ex / jnp.sum(ex, axis=-1, keepdims=True)
    p1 = jnp.max(p, axis=-1, keepdims=True)
    i1 = jnp.min(jnp.where(p == p1, lane, LANES), axis=-1, keepdims=True)
    rest = jnp.where(lane == i1, -1.0, p)
    p2 = jnp.max(rest, axis=-1, keepdims=True)
    i2 = jnp.min(jnp.where(rest == p2, lane, LANES), axis=-1, keepdims=True)
    denom = p1 + p2
    onehot = jnp.where(lane == i1, 1.0, 0.0) + jnp.where(lane == i2, 1.0, 0.0)
    cum = jnp.dot(tri_ref[...], onehot.astype(BF16), preferred_element_type=F32) + cnt_ref[...]
    cnt_ref[...] = cum[tm - 1:tm, :]
    r1 = jnp.sum(jnp.where(lane == i1, cum, 0.0), axis=-1, keepdims=True) - 1.0
    r2 = jnp.sum(jnp.where(lane == i2, cum, 0.0), axis=-1, keepdims=True) - 1.0
    meta = jnp.zeros_like(logits)
    for k, v in ((META_E1, i1.astype(F32)), (META_E2, i2.astype(F32)), (META_R1, r1), (META_R2, r2),
                 (META_W1, p1 / denom), (META_W2, p2 / denom)):
        meta = jnp.where(lane == k, v, meta)
    meta_ref[...] = meta


def _router(grp, x, rw_pad, rb_pad):
    tm = 512
    return pl.pallas_call(
        _router_kernel,
        grid=(grp.m // tm,),
        in_specs=[
            pl.BlockSpec((tm, D_MODEL), lambda i: (i, 0)),
            pl.BlockSpec((D_MODEL, LANES), lambda i: (0, 0)),
            pl.BlockSpec((1, LANES), lambda i: (0, 0)),
        ],
        out_specs=[
            pl.BlockSpec((tm, LANES), lambda i: (i, 0)),
            pl.BlockSpec((1, LANES), lambda i: (0, 0)),
        ],
        out_shape=[
            jax.ShapeDtypeStruct((grp.m, LANES), F32),
            jax.ShapeDtypeStruct((1, LANES), F32),
        ],
        scratch_shapes=[pltpu.VMEM((tm, tm), BF16)],
        compiler_params=_params(("arbitrary",),
                                [((tm, D_MODEL), F32), ((D_MODEL, LANES), F32), ((tm, LANES), F32)],
                                [((tm, tm), BF16), ((tm, D_MODEL), F32)]),
        name="router",
    )(x, rw_pad, rb_pad)


def _route_plan(metas, counts, tg, n_tiles):
    cnts = [c[0, :N_EXPERTS].astype(jnp.int32) for c in counts]
    total = sum(cnts)
    padded = (total + tg - 1) // tg * tg
    ends = jnp.cumsum(padded)
    offs = ends - padded
    slots = []
    for meta, before in zip(metas, [sum(cnts[:g], jnp.zeros_like(total)) for g in range(len(cnts))]):
        idx = meta[:, :META_W1].astype(jnp.int32)
        start = offs + before
        slots.append((start[idx[:, META_E1]] + idx[:, META_R1], start[idx[:, META_E2]] + idx[:, META_R2]))
    n_used = ends[-1] // tg
    tile = jnp.arange(n_tiles, dtype=jnp.int32)
    tile_c = jnp.minimum(tile, n_used - 1)
    tile_expert = jnp.sum(tile_c[:, None] * tg >= ends[None, :], axis=1).astype(jnp.int32)
    return slots, tile_expert, n_used.reshape(1).astype(jnp.int32)


def _dispatch_kernel(pos1_ref, pos2_ref, x_ref, xs_in_ref, xs_ref, sem):
    del xs_in_ref
    tm = x_ref.shape[0]
    base = pl.program_id(0) * tm

    def row_copy(t, slot):
        return pltpu.make_async_copy(x_ref.at[pl.ds(t, 1)], xs_ref.at[pl.ds(slot, 1)], sem)

    def start(t, c):
        row_copy(t, pos1_ref[base + t]).start(priority=0)
        row_copy(t, pos2_ref[base + t]).start(priority=1)
        return c

    def wait(t, c):
        row_copy(t, 0).wait()
        row_copy(t, 0).wait()
        return c

    lax.fori_loop(0, tm, start, 0, unroll=DMA_UNROLL)
    lax.fori_loop(0, tm, wait, 0, unroll=DMA_UNROLL)


def _dispatch(grp, x, pos1, pos2, xs):
    tm = 1024
    return pl.pallas_call(
        _dispatch_kernel,
        grid_spec=pltpu.PrefetchScalarGridSpec(
            num_scalar_prefetch=2,
            grid=(grp.m // tm,),
            in_specs=[
                pl.BlockSpec((tm, D_MODEL), lambda i, p1, p2: (i, 0)),
                pl.BlockSpec(memory_space=pl.ANY),
            ],
            out_specs=pl.BlockSpec(memory_space=pl.ANY),
            scratch_shapes=[pltpu.SemaphoreType.DMA(())],
        ),
        out_shape=jax.ShapeDtypeStruct(xs.shape, F32),
        input_output_aliases={3: 0},
        compiler_params=_params(("arbitrary",), [((tm, D_MODEL), F32)]),
        name="moe_dispatch",
    )(pos1, pos2, x, xs)


def _experts_kernel(te_ref, nu_ref, xs_ref, wg_ref, wu_ref, wd_ref, o_ref):
    del te_ref
    r, j = pl.program_id(0), pl.program_id(1)

    @pl.when(j == 0)
    def _():
        o_ref[...] = jnp.zeros_like(o_ref)

    @pl.when(r < nu_ref[0])
    def _():
        o_ref[...] += _swiglu_tile(xs_ref[...].astype(BF16), wg_ref, wu_ref, wd_ref)


def _experts(xs, tile_expert, n_used, wg, wu, wd, tg):
    n_rows = xs.shape[0]
    f = wg.shape[-1]
    tf = 1024
    n_j = f // tf

    def used(r, nu):
        return r < nu[0]

    def x_map(r, j, te, nu):
        return (jnp.minimum(r, nu[0] - 1), 0)

    def up_map(r, j, te, nu):
        return (te[r], 0, jnp.where(used(r, nu), j, n_j - 1))

    def down_map(r, j, te, nu):
        return (te[r], jnp.where(used(r, nu), j, n_j - 1), 0)

    return pl.pallas_call(
        _experts_kernel,
        grid_spec=pltpu.PrefetchScalarGridSpec(
            num_scalar_prefetch=2,
            grid=(n_rows // tg, n_j),
            in_specs=[
                pl.BlockSpec((tg, D_MODEL), x_map),
                pl.BlockSpec((None, D_MODEL, tf), up_map),
                pl.BlockSpec((None, D_MODEL, tf), up_map),
                pl.BlockSpec((None, tf, D_MODEL), down_map),
            ],
            out_specs=pl.BlockSpec((tg, D_MODEL), lambda r, j, te, nu: (r, 0)),
        ),
        out_shape=jax.ShapeDtypeStruct((n_rows, D_MODEL), F32),
        compiler_params=_params(("arbitrary", "arbitrary"),
                                [((tg, D_MODEL), F32)] * 2 + [((D_MODEL, tf), BF16)] * 3,
                                [((tg, tf), F32)] * 4 + [((tg, D_MODEL), BF16)]),
        name="moe_experts",
    )(tile_expert, n_used, xs, wg, wu, wd)


def _combine_post_kernel(pos1_ref, pos2_ref, ys_ref, meta_ref, x_ref, gpost_ref, gt_ref, *rest,
                         with_next):
    *rest, buf1_ref, buf2_ref, sem = rest
    tm = x_ref.shape[0]
    step, n_steps = pl.program_id(0), pl.num_programs(0)
    cur = step % 2

    def row_copy(row, buf_ref, half, t):
        return pltpu.make_async_copy(ys_ref.at[pl.ds(row, 1)], buf_ref.at[half, pl.ds(t, 1)],
                                     sem.at[half])

    def gather(tile, half):
        def start(t, c):
            row_copy(pos1_ref[tile * tm + t], buf1_ref, half, t).start(priority=0)
            row_copy(pos2_ref[tile * tm + t], buf2_ref, half, t).start(priority=1)
            return c
        lax.fori_loop(0, tm, start, 0, unroll=DMA_UNROLL)

    @pl.when(step == 0)
    def _():
        gather(0, 0)

    @pl.when(step + 1 < n_steps)
    def _():
        gather(step + 1, 1 - cur)

    def wait(t, c):
        row_copy(0, buf1_ref, cur, t).wait()
        row_copy(0, buf2_ref, cur, t).wait()
        return c

    lax.fori_loop(0, tm, wait, 0, unroll=DMA_UNROLL)
    meta = meta_ref[...]
    y = meta[:, META_W1:META_W1 + 1] * buf1_ref[cur] + meta[:, META_W2:META_W2 + 1] * buf2_ref[cur]
    if with_next:
        gpre_ref, sc_ref, sh_ref, xo_ref, xn_ref = rest
        nxt = _next_mods(gpre_ref, sc_ref, sh_ref, tm)
    else:
        (xo_ref,) = rest
        nxt = None
    x1, xn = _post_math(y, x_ref[...], gpost_ref[...], _mod_rows(gt_ref, tm), nxt)
    xo_ref[...] = x1
    if with_next:
        xn_ref[...] = xn.astype(xn_ref.dtype)


def _combine_post(grp, ys, pos1, pos2, meta, x, gpost, gt, nxt):
    tm = 256
    imap = lambda i, p1, p2: (i, 0)
    in_specs, args, out_specs, out_shape = _post_specs(grp, tm, x, gpost, gt, nxt, imap)
    out = pl.pallas_call(
        functools.partial(_combine_post_kernel, with_next=nxt is not None),
        grid_spec=pltpu.PrefetchScalarGridSpec(
            num_scalar_prefetch=2,
            grid=(grp.m // tm,),
            in_specs=[pl.BlockSpec(memory_space=pl.ANY), pl.BlockSpec((tm, LANES), imap)] + in_specs,
            out_specs=out_specs,
            scratch_shapes=[pltpu.VMEM((2, tm, D_MODEL), F32), pltpu.VMEM((2, tm, D_MODEL), F32),
                            pltpu.SemaphoreType.DMA((2,))],
        ),
        out_shape=out_shape,
        compiler_params=_params(("arbitrary",), [((tm, D_MODEL), F32)] * 7,
                                [((tm, D_MODEL), F32)] * 4),
        name="moe_combine_post",
    )(pos1, pos2, ys, meta, *args)
    return (out[0], out[1]) if nxt is not None else (out[0], None)


def _moe_post(grps, xns, p, j, xs_res, gpost, gts, nxts):
    tg = 512
    n_tiles = 2 * sum(g.m for g in grps) // tg + N_EXPERTS
    rw_pad = jnp.pad(p['router_w'][j], ((0, 0), (0, LANES - N_EXPERTS)))
    rb_pad = jnp.pad(p['router_b'][j], (0, LANES - N_EXPERTS)).reshape(1, LANES)
    routed = [_router(g, xn, rw_pad, rb_pad) for g, xn in zip(grps, xns)]
    metas, counts = zip(*routed)
    slots, tile_expert, n_used = _route_plan(metas, counts, tg, n_tiles)
    xs = jnp.zeros((n_tiles * tg, D_MODEL), F32)
    for g, xn, (pos1, pos2) in zip(grps, xns, slots):
        xs = _dispatch(g, xn, pos1, pos2, xs)
    wg, wu, wd = (p['jobs'].result(k)[j] for k in ('moe_wg', 'moe_wu', 'moe_wd'))
    ys = _experts(xs, tile_expert, n_used, wg, wu, wd, tg)
    return [_combine_post(g, ys, pos1, pos2, meta, x, gpost, gt, nxt)
            for g, (pos1, pos2), meta, x, gt, nxt in zip(grps, slots, metas, xs_res, gts, nxts)]


def _mixer(grp, l, x, xn, mods, h0, conv0, p, a_tiles, xn_dtype):
    _, _, gt1, sh2, sc2, _ = mods
    w_in = p['w_in']
    ya, h_last, conv_new = _branch_a(
        grp, *a_tiles, xn, p['w_ag_bf'], l, conv0, h0, p['conv_w'][l], p['conv_b'][l],
        p['w_r'][l], p['w_i'][l], p['b_r'][l], p['b_i'][l], p['lru_lambda'][l], p['jobs'])
    gel = _mm_act(grp, xn, w_in, l, 2 * D_MODEL, 2 * D_MODEL, _gelu, BF16, "in_proj_gelu", p['jobs'])
    sig = _mm_act(grp, xn, w_in, l, 4 * D_MODEL, 2 * D_MODEL, _sigmoid, BF16, "in_proj_sigmoid", p['jobs'])
    if grp.t >= CHUNK:
        bias_full = jnp.repeat(p['b_s'][l].T, GROUP_B, axis=-1)
        yb = _branch_b(grp, gel, p['g_v'][l], p['w_s'][l], bias_full)
        vn = None
    else:
        yb, vn = _branch_b_short(grp, gel, p['g_v'][l], p['w_s'][l], p['b_s'][l])
    m = _merge(grp, ya, yb, sig, p['w_pa'], p['w_pb'], l, p['jobs'])
    x, xn2 = _oproj_post(grp, m, p['w_o_bf'], l, x, p['g_post_mix'][l], gt1,
                         p['g_pre_ffn'][l], sc2, sh2, xn_dtype, p['jobs'])
    return x, xn2, h_last.reshape(grp.b, D_MODEL), conv_new, vn


def _trunk(grps, xs, mods, h0s, conv0s, p, tiles):
    depth = p['w_in'].shape[0]
    n_g = len(grps)
    xns = [_prenorm(g, x, p['g_pre_mix'][0], mods[i][0][1], mods[i][0][0])
           for i, (g, x) in enumerate(zip(grps, xs))]
    states = [([], [], []) for _ in grps]
    for l in range(depth):
        dense = l % 2 == 0
        j = l // 2
        xn2s = []
        for i, g in enumerate(grps):
            xs[i], xn2, h_last, conv_new, vn = _mixer(
                g, l, xs[i], xns[i], mods[i][l], h0s[i][l], conv0s[i][l], p, tiles[i],
                BF16 if dense else F32)
            xn2s.append(xn2)
            for acc, v in zip(states[i], (h_last, conv_new, vn)):
                acc.append(v)
        if l + 1 < depth:
            nxts = [(p['g_pre_mix'][l + 1], mods[i][l + 1][1], mods[i][l + 1][0]) for i in range(n_g)]
        else:
            nxts = [None] * n_g
        gt2s = [mods[i][l][5] for i in range(n_g)]
        gpost = p['g_post_ffn'][l]
        if dense:
            outs = []
            for i, g in enumerate(grps):
                wg, wu, wd = (p['jobs'].result(k)[j] for k in ('ffn_wg', 'ffn_wu', 'ffn_wd'))
                y = _swiglu(g, xn2s[i], wg, wu, wd)
                outs.append(_post(g, y, xs[i], gpost, gt2s[i], nxts[i]))
        else:
            outs = _moe_post(grps, xn2s, p, j, xs, gpost, gt2s, nxts)
        xs = [o[0] for o in outs]
        xns = [o[1] for o in outs]
    return xs, states


def kernel(x_prompt, x_sample, c_prompt, c_sample, state_lru_h, state_lru_conv, w_ada, b_ada, g_pre_mix, g_post_mix, g_pre_ffn, g_post_ffn, w_in, conv_w, conv_b, w_r, b_r, w_i, b_i, lru_lambda, g_v, w_s, b_s, w_pa, w_pb, w_o, ffn_wg, ffn_wu, ffn_wd, router_w, router_b, moe_wg, moe_wu, moe_wd):
    p = dict(g_pre_mix=g_pre_mix, g_post_mix=g_post_mix, g_pre_ffn=g_pre_ffn, g_post_ffn=g_post_ffn,
             w_in=w_in, conv_w=conv_w, conv_b=conv_b, w_r=w_r, b_r=b_r, w_i=w_i, b_i=b_i,
             lru_lambda=lru_lambda, g_v=g_v, w_s=w_s, b_s=b_s, w_pa=w_pa, w_pb=w_pb,
             router_w=router_w, router_b=router_b, w_o_bf=w_o.astype(BF16),
             w_ag_bf=w_in[:, :, :2 * D_MODEL].astype(BF16),
             jobs=_CastJobs(dict(moe_wg=moe_wg, moe_wu=moe_wu, moe_wd=moe_wd,
                                 ffn_wg=ffn_wg, ffn_wu=ffn_wu, ffn_wd=ffn_wd)))
    depth = w_in.shape[0]
    bp, tp, _ = x_prompt.shape
    bs, ts, _ = x_sample.shape
    assert tp % CHUNK == 0 and ts < CHUNK and ts % SUBLANES == 0

    n_c = bp + bs
    r_pad = -n_c % 16
    c_all = jnp.concatenate([c_sample, c_prompt, jnp.zeros((r_pad, D_MODEL), F32)], axis=0)
    mod = _ada(c_all, w_ada, b_ada)
    mods_s = [[mod[l, k, :bs] for k in range(6)] for l in range(depth)]
    mods_p = [[mod[l, k, bs:n_c].reshape(bp, 1, D_MODEL) for k in range(6)] for l in range(depth)]

    grp_p = _Group(bp, tp, short=False)
    grp_s = _Group(bs, ts, short=True)
    zeros_h = jnp.zeros((depth, bp, D_MODEL), F32)
    zeros_conv = jnp.zeros((depth, bp, CONV_W - 1, D_MODEL), F32)

    (y_p, y_s), (st_p, st_s) = _trunk(
        [grp_p, grp_s],
        [x_prompt.reshape(bp * tp, D_MODEL), x_sample.reshape(bs * ts, D_MODEL)],
        [mods_p, mods_s], [zeros_h, state_lru_h], [zeros_conv, state_lru_conv], p,
        tiles=[(1024, 1024, 1, 256), (512, 1024, 16, ts)])
    return (y_p.reshape(bp, tp, D_MODEL), y_s.reshape(bs, ts, D_MODEL),
            jnp.stack(st_p[0]), jnp.stack(st_p[1]), jnp.stack(st_s[0]), jnp.stack(st_s[1]),
            jnp.stack(st_s[2]))
```
